```python
import math
import jax, jax.numpy as jnp
from jax import lax
import numpy as np


D_MODEL = 1024
BATCH = 16
SEQ = 2048
DEPTH = 4

CTX_LEN = 256
GRID_W = 64
W_BRANCH = 512
N_BRANCH = 3
N_HEADS = 8
N_KV_HEADS = 2
GROUP = N_HEADS // N_KV_HEADS
HEAD_DIM = 64
ROPE_FREQS = HEAD_DIM // 4
ROPE_BASE = 10000.0
Q_BLOCK = 128
ATTN_SCALE = HEAD_DIM ** -0.5
HYENA_ORDER = 2
FILTER_EMB = 33
FILTER_BANDS = (FILTER_EMB - 1) // 2
FILTER_HIDDEN = 64
HYENA_TARGET = 1e-2
FAST_DECAY_PCT = 0.3
SLOW_DECAY_PCT = 1.5
MIN_DECAY = math.log(HYENA_TARGET) / SLOW_DECAY_PCT
MAX_DECAY = math.log(HYENA_TARGET) / FAST_DECAY_PCT
EPS = 1e-6

A_COLS = 4 * W_BRANCH
H_COLS = 4 * W_BRANCH
Q_COLS = N_HEADS * HEAD_DIM
KV_COLS = N_KV_HEADS * HEAD_DIM
H_OFF = A_COLS
C_OFF = H_OFF + H_COLS
K_OFF = C_OFF + Q_COLS
V_OFF = K_OFF + KV_COLS
Z_OFF = V_OFF + KV_COLS
G_OFF = Z_OFF + W_BRANCH
IN_COLS = G_OFF + N_BRANCH * D_MODEL

kernel_name = 'hybrid_conv_hyena_gqa_dit_block'


def _rms_norm(x, g):
    xf = x.astype(jnp.float32)
    y = xf * lax.rsqrt(jnp.mean(xf * xf, axis=-1, keepdims=True) + EPS)
    return (y * g.astype(jnp.float32)).astype(x.dtype)


def _short_conv(u, w):
    up = jnp.pad(u, ((0, 0), (1, 1), (0, 0)))
    return up[:, :-2] * w[0] + up[:, 1:-1] * w[1] + up[:, 2:] * w[2]


def _rope_tables(L):
    rows = L // GRID_W
    row = jnp.repeat(jnp.arange(rows, dtype=jnp.float32), GRID_W)
    col = jnp.tile(jnp.arange(GRID_W, dtype=jnp.float32), rows)
    inv = ROPE_BASE ** (-jnp.arange(ROPE_FREQS, dtype=jnp.float32) / ROPE_FREQS)
    ang = jnp.stack([row[:, None] * inv, col[:, None] * inv], axis=1)
    return jnp.cos(ang)[:, None], jnp.sin(ang)[:, None]


def _apply_rope(x, cos, sin):
    B_, L, H, _ = x.shape
    xf = x.astype(jnp.float32).reshape(B_, L, H, 2, 2, ROPE_FREQS)
    x1, x2 = xf[..., 0, :], xf[..., 1, :]
    out = jnp.stack([x1 * cos - x2 * sin, x2 * cos + x1 * sin], axis=-2)
    return out.reshape(x.shape).astype(x.dtype)


def _hyena_kernels_f(L, w1, b1, w2, b2, w3, b3, freq):
    f32 = jnp.float32
    t = jnp.linspace(0.0, 1.0, L, dtype=f32)[:, None]
    bands = jnp.linspace(1e-4, FILTER_BANDS - 1, FILTER_BANDS, dtype=f32)
    w = (2.0 * math.pi / L) * jnp.arange(L, dtype=f32)[:, None]
    feats = jnp.concatenate([t, jnp.cos(w * bands), jnp.sin(w * bands)], axis=-1)
    h = jnp.sin(freq[0].astype(f32) * (feats @ w1.astype(f32) + b1.astype(f32)))
    h = jnp.sin(freq[1].astype(f32) * (h @ w2.astype(f32) + b2.astype(f32)))
    h = (h @ w3.astype(f32) + b3.astype(f32)).reshape(L, HYENA_ORDER, 2, W_BRANCH)
    deltas = jnp.abs(jnp.linspace(MIN_DECAY, MAX_DECAY, W_BRANCH, dtype=f32))
    h = h * jnp.exp(-t[:, :, None, None] * deltas)
    kern = jnp.concatenate([h[:, :, 0], jnp.zeros((1, HYENA_ORDER, W_BRANCH), f32),
                            h[:0:-1, :, 1]], axis=0)
    kern = kern / jnp.sum(jnp.abs(kern), axis=0, keepdims=True)
    return jnp.fft.rfft(kern, axis=0)


def _long_conv(u, kf, bias):
    L = u.shape[1]
    uf = u.astype(jnp.float32)
    U = jnp.fft.rfft(uf, n=2 * L, axis=1)
    y = jnp.fft.irfft(U * kf, n=2 * L, axis=1)[:, :L]
    return (y + uf * bias.astype(jnp.float32)).astype(u.dtype)


def _q_heads(p, q_g):
    B_, L, _ = p.shape
    return _rms_norm(p[..., C_OFF:K_OFF].reshape(B_, L, N_HEADS, HEAD_DIM), q_g)


def _kv_heads(pkv, k_g):
    B_, L, _ = pkv.shape
    k = _rms_norm(pkv[..., :KV_COLS].reshape(B_, L, N_KV_HEADS, HEAD_DIM), k_g)
    v = pkv[..., KV_COLS:].reshape(B_, L, N_KV_HEADS, HEAD_DIM)
    return k, v


def _gqa(q, keys, vals):
    B_, Q, _, _ = q.shape
    qg = q.reshape(B_, Q, N_KV_HEADS, GROUP, HEAD_DIM)
    s = jnp.einsum('bqkgd,bskd->bkgqs', qg, keys).astype(jnp.float32) * ATTN_SCALE
    p = jax.nn.softmax(s, axis=-1).astype(vals.dtype)
    o = jnp.einsum('bkgqs,bskd->bqkgd', p, vals)
    return o.reshape(B_, Q, N_HEADS * HEAD_DIM)


def _latent_attention(q, k, v, k_c, v_c):
    B_, S, H, Dh = q.shape
    keys = jnp.concatenate([k_c, k], axis=1)
    vals = jnp.concatenate([v_c, v], axis=1)
    qb = q.reshape(B_, S // Q_BLOCK, Q_BLOCK, H, Dh).swapaxes(0, 1)
    o = lax.map(lambda qblk: _gqa(qblk, keys, vals), qb)
    return o.swapaxes(0, 1).reshape(B_, S, H * Dh)


def _mixer(p, attn_out, conv_a, conv_h, kern_f, hyena_bias, w_branch, w_out):
    B_, L, _ = p.shape
    xa, ba, ca, za = jnp.split(p[..., :A_COLS], 4, axis=-1)
    y_a = ba * _short_conv(ca * xa, conv_a) * jax.nn.silu(za)
    v, x1, x2 = jnp.split(_short_conv(p[..., H_OFF:H_OFF + 3 * W_BRANCH], conv_h), 3, axis=-1)
    z = x1 * _long_conv(v, kern_f[:, 0], hyena_bias[0])
    z = x2 * _long_conv(z, kern_f[:, 1], hyena_bias[1])
    y_b = z * jax.nn.silu(p[..., H_OFF + 3 * W_BRANCH:C_OFF])
    y_c = attn_out * jax.nn.silu(p[..., Z_OFF:G_OFF])
    gates = jax.nn.sigmoid(p[..., G_OFF:].astype(jnp.float32)).reshape(B_, L, N_BRANCH, D_MODEL)
    proj = jnp.einsum('blnw,nwd->blnd', jnp.stack([y_a, y_b, y_c], axis=2), w_branch)
    merged = jnp.sum(gates * proj.astype(jnp.float32), axis=2).astype(p.dtype)
    return merged @ w_out


def setup_inputs(seed: int = 0) -> dict:
    key = jax.random.key(seed)
    ks = jax.random.split(key, 22)
    f32 = jnp.float32
    nrm = lambda k, shape, s: jax.random.normal(k, shape, f32) * s
    return {
        'x': nrm(ks[0], (BATCH, SEQ, D_MODEL), 1.0),
        'c': nrm(ks[1], (BATCH, D_MODEL), 1.0),
        'ctx': nrm(ks[2], (BATCH, CTX_LEN, D_MODEL), 1.0),
        'c_ctx': nrm(ks[3], (D_MODEL,), 1.0),
        'norm_g': 1.0 + nrm(ks[4], (DEPTH, D_MODEL), 0.02),
        'w_mod': nrm(ks[5], (DEPTH, D_MODEL, 3 * D_MODEL), 0.5 * D_MODEL ** -0.5),
        'b_mod': nrm(ks[6], (DEPTH, 3 * D_MODEL), 0.02),
        'w_in': nrm(ks[7], (DEPTH, D_MODEL, IN_COLS), D_MODEL ** -0.5),
        'conv_a': nrm(ks[8], (DEPTH, 3, W_BRANCH), 3 ** -0.5),
        'conv_h': nrm(ks[9], (DEPTH, 3, 3 * W_BRANCH), 3 ** -0.5),
        'filt_w1': nrm(ks[10], (DEPTH, FILTER_EMB, FILTER_HIDDEN), FILTER_EMB ** -0.5),
        'filt_b1': nrm(ks[11], (DEPTH, FILTER_HIDDEN), 0.02),
        'filt_w2': nrm(ks[12], (DEPTH, FILTER_HIDDEN, FILTER_HIDDEN), FILTER_HIDDEN ** -0.5),
        'filt_b2': nrm(ks[13], (DEPTH, FILTER_HIDDEN), 0.02),
        'filt_w3': nrm(ks[14], (DEPTH, FILTER_HIDDEN, HYENA_ORDER * 2 * W_BRANCH), FILTER_HIDDEN ** -0.5),
        'filt_b3': nrm(ks[15], (DEPTH, HYENA_ORDER * 2 * W_BRANCH), 0.02),
        'filt_freq': 1.0 + nrm(ks[16], (DEPTH, 2, FILTER_HIDDEN), 0.1),
        'hyena_bias': nrm(ks[17], (DEPTH, HYENA_ORDER, W_BRANCH), 0.5),
        'q_norm_g': 1.0 + nrm(ks[18], (DEPTH, HEAD_DIM), 0.02),
        'k_norm_g': 1.0 + nrm(ks[19], (DEPTH, HEAD_DIM), 0.02),
        'w_branch': nrm(ks[20], (DEPTH, N_BRANCH, W_BRANCH, D_MODEL), W_BRANCH ** -0.5),
        'w_out': nrm(ks[21], (DEPTH, D_MODEL, D_MODEL), D_MODEL ** -0.5),
    }


def reference(x, c, ctx, c_ctx, norm_g, w_mod, b_mod, w_in, conv_a, conv_h, filt_w1, filt_b1,
              filt_w2, filt_b2, filt_w3, filt_b3, filt_freq, hyena_bias, q_norm_g, k_norm_g,
              w_branch, w_out):
    S = x.shape[1]
    C_L = ctx.shape[1]
    cos, sin = _rope_tables(S)
    for i in range(DEPTH):
        last = i == DEPTH - 1
        filt = (filt_w1[i], filt_b1[i], filt_w2[i], filt_b2[i], filt_w3[i], filt_b3[i], filt_freq[i])
        sh, sc, gt = jnp.split(jax.nn.silu(c) @ w_mod[i] + b_mod[i], 3, axis=-1)
        sh_c, sc_c, gt_c = jnp.split(jax.nn.silu(c_ctx) @ w_mod[i] + b_mod[i], 3, axis=-1)
        h = _rms_norm(x, norm_g[i]) * (1.0 + sc[:, None]) + sh[:, None]
        hc = _rms_norm(ctx, norm_g[i]) * (1.0 + sc_c) + sh_c
        if last:
            k_c, v_c = _kv_heads(hc @ w_in[i][:, K_OFF:Z_OFF], k_norm_g[i])
        else:
            pc = hc @ w_in[i]
            k_c, v_c = _kv_heads(pc[..., K_OFF:Z_OFF], k_norm_g[i])
            attn_c = _gqa(_q_heads(pc, q_norm_g[i]), k_c, v_c)
            out_c = _mixer(pc, attn_c, conv_a[i], conv_h[i], _hyena_kernels_f(C_L, *filt),
                           hyena_bias[i], w_branch[i], w_out[i])
            ctx_next = ctx + gt_c * out_c
        p = h @ w_in[i]
        q = _apply_rope(_q_heads(p, q_norm_g[i]), cos, sin)
        k, v = _kv_heads(p[..., K_OFF:Z_OFF], k_norm_g[i])
        k = _apply_rope(k, cos, sin)
        attn = _latent_attention(q, k, v, k_c, v_c)
        out = _mixer(p, attn, conv_a[i], conv_h[i], _hyena_kernels_f(S, *filt),
                     hyena_bias[i], w_branch[i], w_out[i])
        x = x + gt[:, None] * out
        if not last:
            ctx = ctx_next
    return x
```

```python
import functools
import math

import jax
import jax.numpy as jnp
from jax import lax
from jax.experimental import pallas as pl
from jax.experimental.pallas import tpu as pltpu

F32 = jnp.float32
BF16 = jnp.bfloat16

D_MODEL = 1024
DEPTH = 4
GRID_W = 64
W_BRANCH = 512
N_BRANCH = 3
N_HEADS = 8
N_KV_HEADS = 2
HEAD_DIM = 64
ROPE_FREQS = HEAD_DIM // 4
ROPE_BASE = 10000.0
ATTN_SCALE = HEAD_DIM ** -0.5
HYENA_ORDER = 2
FILTER_EMB = 33
FILTER_BANDS = (FILTER_EMB - 1) // 2
FILTER_HIDDEN = 64
HYENA_TARGET = 1e-2
MIN_DECAY = math.log(HYENA_TARGET) / 1.5
MAX_DECAY = math.log(HYENA_TARGET) / 0.3
EPS = 1e-6

A_COLS = 4 * W_BRANCH
H_OFF = A_COLS
C_OFF = H_OFF + 4 * W_BRANCH
Q_COLS = N_HEADS * HEAD_DIM
KV_COLS = N_KV_HEADS * HEAD_DIM
K_OFF = C_OFF + Q_COLS
V_OFF = K_OFF + KV_COLS
Z_OFF = V_OFF + KV_COLS
G_OFF = Z_OFF + W_BRANCH

LANES = 128
VMEM_LIMIT_BYTES = 58 * 1024 * 1024
KV_DUP_COLS = 2 * KV_COLS
MOD_ROWS = 24
FILT_PAD = 64
HALF = 256


def _params(n_axes):
    return pltpu.CompilerParams(dimension_semantics=("arbitrary",) * n_axes,
                                vmem_limit_bytes=VMEM_LIMIT_BYTES)


def _const_spec(shape):
    nd = len(shape)
    return pl.BlockSpec(shape, lambda *_: (0,) * nd, pipeline_mode=pl.Buffered(1))


def _silu(v):
    return v * jax.nn.sigmoid(v)


def _dot(a, b):
    return jnp.dot(a, b, preferred_element_type=F32)


def _split(a):
    hi = a.astype(BF16)
    lo = (a - hi.astype(F32)).astype(BF16)
    return hi, lo


def _dot3(a, b):
    ah, al = _split(a)
    bh, bl = _split(b)
    return _dot(ah, bh) + (_dot(ah, bl) + _dot(al, bh))


def _rms_mod(x, g, sc, sh):
    y = x * lax.rsqrt(jnp.mean(x * x, axis=-1, keepdims=True) + EPS)
    return (y * g) * (1.0 + sc) + sh


def _shift_rows(u, first_row, last_row):
    n = u.shape[0]
    row = lax.broadcasted_iota(jnp.int32, u.shape, 0)
    prev = jnp.where(row == 0, first_row, pltpu.roll(u, 1, 0))
    nxt = jnp.where(row == n - 1, last_row, pltpu.roll(u, n - 1, 0))
    return prev, nxt


def _conv3(u, w, first_row=0.0, last_row=0.0):
    prev, nxt = _shift_rows(u, first_row, last_row)
    return prev * w[0:1] + u * w[1:2] + nxt * w[2:3]


def _mod_kernel(cc_ref, w_ref, b_ref, o_ref):
    o_ref[0] = _dot3(_silu(cc_ref[...]), w_ref[0]) + b_ref[0]


def _modulation(cc, w_mod, b_mod):
    return pl.pallas_call(
        _mod_kernel,
        grid=(DEPTH, 3),
        in_specs=[pl.BlockSpec((MOD_ROWS, D_MODEL), lambda i, j: (0, 0)),
                  pl.BlockSpec((1, D_MODEL, D_MODEL), lambda i, j: (i, 0, j)),
                  pl.BlockSpec((1, 1, D_MODEL), lambda i, j: (i, 0, j))],
        out_specs=pl.BlockSpec((1, MOD_ROWS, D_MODEL), lambda i, j: (i, 0, j)),
        out_shape=jax.ShapeDtypeStruct((DEPTH, MOD_ROWS, 3 * D_MODEL), F32),
        compiler_params=_params(2),
        name="modulation",
    )(cc, w_mod, b_mod.reshape(DEPTH, 1, 3 * D_MODEL))


def _dft_tables(L):
    n = 2 * L
    idx = jnp.arange(L, dtype=jnp.int32)
    m = (idx[:, None] * idx[None, :]) % n
    ang = m.astype(F32) * (2.0 * math.pi / n)
    alt = (1 - 2 * (idx % 2)).astype(F32)
    cm = jnp.cos(ang)
    sm = jnp.where(idx[:, None] == 0, alt[None, :], -jnp.sin(ang))
    col0 = idx[None, :] == 0
    icm = jnp.where(col0, 1.0 / n, cm * (2.0 / n))
    ism = jnp.where(col0, alt[:, None] / n, -jnp.sin(ang) * (2.0 / n))
    cm_hi, cm_lo = _split(cm)
    sm_hi, sm_lo = _split(sm)
    return dict(cm=cm_hi, cm_lo=cm_lo, sm=sm_hi, sm_lo=sm_lo,
                icm=icm.astype(BF16), ism=ism.astype(BF16))


def _filter_kernel(feats_ref, w1_ref, b1_ref, w2_ref, b2_ref, w3_ref, b3_ref, fq_ref, dl_ref,
                   cmh_ref, cml_ref, smh_ref, sml_ref,
                   ka_ref, kb_ref, kn_ref,
                   xsh, xsl, xdh, xdl, *, L):
    ft = pl.program_id(2)

    @pl.when(ft == 0)
    def _():
        fq = fq_ref[0]
        h = jnp.sin(fq[0:1] * (_dot3(feats_ref[...], w1_ref[0]) + b1_ref[0]))
        h = jnp.sin(fq[1:2] * (_dot3(h, w2_ref[0]) + b2_ref[0]))
        h = _dot3(h, w3_ref[0]) + b3_ref[0]
        row = lax.broadcasted_iota(jnp.int32, (L, W_BRANCH), 0)
        t = row.astype(F32) * (1.0 / (L - 1))
        win = jnp.exp(-t * dl_ref[...])
        hf = h[:, :W_BRANCH] * win
        hb = jnp.where(row == 0, 0.0, h[:, W_BRANCH:] * win)
        nrm = (jnp.sum(jnp.abs(hf), axis=0, keepdims=True)
               + jnp.sum(jnp.abs(hb), axis=0, keepdims=True))
        xs = (hf + hb) / nrm
        xd = (hf - hb) / nrm
        alt = (1 - 2 * (row % 2)).astype(F32)
        kn_ref[0, 0] = jnp.sum(xs * alt, axis=0, keepdims=True)
        xsh[...], xsl[...] = _split(xs)
        xdh[...], xdl[...] = _split(xd)

    ka_ref[0, 0] = (_dot(cmh_ref[...], xsh[...])
                    + (_dot(cmh_ref[...], xsl[...]) + _dot(cml_ref[...], xsh[...])))
    kb = (_dot(smh_ref[...], xdh[...])
          + (_dot(smh_ref[...], xdl[...]) + _dot(sml_ref[...], xdh[...])))
    frow = lax.broadcasted_iota(jnp.int32, kb.shape, 0) + ft * kb.shape[0]
    kb_ref[0, 0] = jnp.where(frow == 0, 0.0, kb)


def _filter_spectra(L, tabs, filt_w1, filt_b1, filt_w2, filt_b2, filt_w3, filt_b3, filt_freq):
    t = jnp.linspace(0.0, 1.0, L, dtype=F32)[:, None]
    bands = jnp.linspace(1e-4, FILTER_BANDS - 1, FILTER_BANDS, dtype=F32)
    w = (2.0 * math.pi / L) * jnp.arange(L, dtype=F32)[:, None]
    feats = jnp.concatenate([t, jnp.cos(w * bands), jnp.sin(w * bands)], axis=-1)
    feats = jnp.pad(feats, ((0, 0), (0, FILT_PAD - FILTER_EMB)))
    w1 = jnp.pad(filt_w1, ((0, 0), (0, FILT_PAD - FILTER_EMB), (0, 0)))
    deltas = jnp.abs(jnp.linspace(MIN_DECAY, MAX_DECAY, W_BRANCH, dtype=F32))[None, :]
    ft = min(L, 512)
    n_ft = L // ft
    w2c = 2 * W_BRANCH
    lay = lambda i, o, f: (i, 0, 0)
    tab = pl.BlockSpec((ft, L), lambda i, o, f: (f, 0))
    out = pl.BlockSpec((1, 1, ft, W_BRANCH), lambda i, o, f: (i, o, f, 0))
    return pl.pallas_call(
        functools.partial(_filter_kernel, L=L),
        grid=(DEPTH, HYENA_ORDER, n_ft),
        in_specs=[pl.BlockSpec((L, FILT_PAD), lambda i, o, f: (0, 0)),
                  pl.BlockSpec((1, FILT_PAD, FILTER_HIDDEN), lay),
                  pl.BlockSpec((1, 1, FILTER_HIDDEN), lay),
                  pl.BlockSpec((1, FILTER_HIDDEN, FILTER_HIDDEN), lay),
                  pl.BlockSpec((1, 1, FILTER_HIDDEN), lay),
                  pl.BlockSpec((1, FILTER_HIDDEN, w2c), lambda i, o, f: (i, 0, o)),
                  pl.BlockSpec((1, 1, w2c), lambda i, o, f: (i, 0, o)),
                  pl.BlockSpec((1, 2, FILTER_HIDDEN), lay),
                  pl.BlockSpec((1, W_BRANCH), lambda i, o, f: (0, 0)),
                  tab, tab, tab, tab],
        out_specs=[out, out,
                   pl.BlockSpec((1, 1, 1, W_BRANCH), lambda i, o, f: (i, o, 0, 0))],
        out_shape=[jax.ShapeDtypeStruct((DEPTH, HYENA_ORDER, L, W_BRANCH), F32),
                   jax.ShapeDtypeStruct((DEPTH, HYENA_ORDER, L, W_BRANCH), F32),
                   jax.ShapeDtypeStruct((DEPTH, HYENA_ORDER, 1, W_BRANCH), F32)],
        scratch_shapes=[pltpu.VMEM((L, W_BRANCH), BF16)] * 4,
        compiler_params=_params(3),
        name=f"filter_spectra_{L}",
    )(feats, w1, filt_b1[:, None], filt_w2, filt_b2[:, None], filt_w3, filt_b3[:, None],
      filt_freq, deltas, tabs["cm"], tabs["cm_lo"], tabs["sm"], tabs["sm_lo"])


def _head_norm(v, bd_ref, g):
    hi, lo = _split(v * v)
    ss = _dot(hi, bd_ref[...]) + _dot(lo, bd_ref[...])
    return v * lax.rsqrt(ss * (1.0 / HEAD_DIM) + EPS) * g


def _rope(v, cos, sin):
    n = v.shape[1] // LANES
    lane = lax.broadcasted_iota(jnp.int32, (v.shape[0], LANES), 1)
    first = (lane % 32) < 16
    out = []
    for i in range(n):
        c = v[:, i * LANES:(i + 1) * LANES]
        partner = jnp.where(first, pltpu.roll(c, LANES - 16, 1), pltpu.roll(c, 16, 1))
        out.append(c * cos + partner * sin)
    return jnp.concatenate(out, axis=1)


def _inproj_kernel(*refs, rope, kv_only):
    it = iter(refs)
    x_ref, sc_ref, sh_ref, g_ref = next(it), next(it), next(it), next(it)
    if not kv_only:
        wa_ref, wh_ref = next(it), next(it)
    wq_ref = next(it)
    if not kv_only:
        gq_ref, bdq_ref = next(it), next(it)
    gk_ref, bdk_ref = next(it), next(it)
    if rope:
        cos_ref, sin_ref = next(it), next(it)
    outs = list(it)

    h = _rms_mod(x_ref[0], g_ref[...], sc_ref[0], sh_ref[0]).astype(BF16)
    w = W_BRANCH
    if kv_only:
        kk_ref, vv_ref = outs
        acc = _dot(h, wq_ref[...])
        k = _head_norm(acc[:, :KV_DUP_COLS], bdk_ref, gk_ref[...])
        kk_ref[0] = k.astype(BF16)
        vv_ref[0] = acc[:, KV_DUP_COLS:].astype(BF16)
        return

    ua_ref, ga_ref, hp_ref, sgh_ref, q_ref, kk_ref, vv_ref, sz_ref = outs
    acc = _dot(h, wa_ref[...])
    ua_ref[0] = (acc[:, 2 * w:3 * w] * acc[:, :w]).astype(BF16)
    ga_ref[0] = (acc[:, w:2 * w] * _silu(acc[:, 3 * w:])).astype(BF16)
    acc = _dot(h, wh_ref[...])
    hp_ref[0] = acc[:, :3 * w].astype(BF16)
    sgh_ref[0] = _silu(acc[:, 3 * w:]).astype(BF16)
    acc = _dot(h, wq_ref[...])
    q = _head_norm(acc[:, :Q_COLS], bdq_ref, gq_ref[...])
    k = _head_norm(acc[:, Q_COLS:Q_COLS + KV_DUP_COLS], bdk_ref, gk_ref[...])
    if rope:
        q = _rope(q, cos_ref[...], sin_ref[...])
        k = _rope(k, cos_ref[...], sin_ref[...])
    q_ref[0] = (q * ATTN_SCALE).astype(BF16)
    kk_ref[0] = k.astype(BF16)
    vv_ref[0] = acc[:, Q_COLS + KV_DUP_COLS:Q_COLS + 2 * KV_DUP_COLS].astype(BF16)
    sz_ref[0] = _silu(acc[:, Q_COLS + 2 * KV_DUP_COLS:]).astype(BF16)


def _inproj(x, sc, sh, g, wts, consts, *, rope, kv_only):
    B, T, D = x.shape
    tm = min(T, 512)
    row = lambda b, t: (b, t, 0)
    vec = pl.BlockSpec((1, 1, D), lambda b, t: (b, 0, 0))
    args = [x, sc, sh, g[None, :]]
    specs = [pl.BlockSpec((1, tm, D), row), vec, vec, _const_spec((1, D))]
    if not kv_only:
        args += [wts["wa"], wts["wh"], wts["wq"], consts["gq"], consts["bdq"]]
        specs += [_const_spec(wts["wa"].shape), _const_spec(wts["wh"].shape),
                  _const_spec(wts["wq"].shape), _const_spec(consts["gq"].shape),
                  _const_spec(consts["bdq"].shape)]
    else:
        args += [wts["wkv"]]
        specs += [_const_spec(wts["wkv"].shape)]
    args += [consts["gk"], consts["bdk"]]
    specs += [_const_spec(consts["gk"].shape), _const_spec(consts["bdk"].shape)]
    if rope:
        args += [consts["cos"], consts["sin"]]
        specs += [pl.BlockSpec((tm, LANES), lambda b, t: (t, 0))] * 2
    widths = ([KV_DUP_COLS, KV_DUP_COLS] if kv_only else
              [W_BRANCH, W_BRANCH, 3 * W_BRANCH, W_BRANCH, Q_COLS, KV_DUP_COLS, KV_DUP_COLS, W_BRANCH])
    return pl.pallas_call(
        functools.partial(_inproj_kernel, rope=rope, kv_only=kv_only),
        grid=(B, T // tm),
        in_specs=specs,
        out_specs=[pl.BlockSpec((1, tm, n), row) for n in widths],
        out_shape=[jax.ShapeDtypeStruct((B, T, n), BF16) for n in widths],
        compiler_params=_params(2),
        name=f"inproj_{T}" + ("_kv" if kv_only else ""),
    )(*args)


def _attn_kernel(*refs, n_src):
    q_ref, sz_ref = refs[0], refs[1]
    kv = refs[2:2 + 2 * n_src]
    o_ref = refs[-1]
    tq = q_ref.shape[1]
    low = lax.broadcasted_iota(jnp.int32, (tq, LANES), 1) < HEAD_DIM
    nt = (((1,), (1,)), ((), ()))
    for j in range(2):
        cols = slice(j * LANES, (j + 1) * LANES)
        qp = q_ref[0, :, cols].astype(F32)
        halves = []
        for sel in (low, jnp.logical_not(low)):
            qm = jnp.where(sel, qp, 0.0).astype(BF16)
            s = [lax.dot_general(qm, kv[2 * i][0], nt, preferred_element_type=F32) for i in range(n_src)]
            m = functools.reduce(jnp.maximum, [jnp.max(v, axis=-1, keepdims=True) for v in s])
            p = [jnp.exp(v - m) for v in s]
            l = functools.reduce(jnp.add, [jnp.sum(v, axis=-1, keepdims=True) for v in p])
            o = functools.reduce(jnp.add, [_dot(p[i].astype(BF16), kv[2 * i + 1][0]) for i in range(n_src)])
            halves.append(o / l)
        o = jnp.where(low, halves[0], halves[1])
        o_ref[0, :, cols] = (o * sz_ref[0, :, cols].astype(F32)).astype(BF16)


def _attention(q, sz, sources):
    B, T, _ = q.shape
    tq = min(T, 256)
    gw = Q_COLS // N_KV_HEADS
    qspec = pl.BlockSpec((1, tq, gw), lambda b, g, t: (b, t, g))
    args, specs = [q, sz], [qspec, qspec]
    for kk, vv in sources:
        spec = pl.BlockSpec((1, kk.shape[1], LANES), lambda b, g, t: (b, 0, g))
        args += [kk, vv]
        specs += [spec, spec]
    return pl.pallas_call(
        functools.partial(_attn_kernel, n_src=len(sources)),
        grid=(B, N_KV_HEADS, T // tq),
        in_specs=specs,
        out_specs=qspec,
        out_shape=jax.ShapeDtypeStruct((B, T, Q_COLS), BF16),
        compiler_params=_params(3),
        name=f"attention_{T}",
    )(*args)


def _hyena_fwd_kernel(*refs, conv_in):
    it = iter(refs)
    u_ref = next(it)
    cw_ref = next(it) if conv_in else None
    cm_ref, sm_ref, ka_ref, kb_ref, kn_ref, yr_ref, yi_ref = it
    L = u_ref.shape[1]
    row = lax.broadcasted_iota(jnp.int32, (L, HALF), 0)
    for c in range(W_BRANCH // HALF):
        cols = slice(c * HALF, (c + 1) * HALF)
        u = u_ref[0, :, cols].astype(F32)
        if conv_in:
            u = _conv3(u, cw_ref[:, cols])
        ub = u.astype(BF16)
        ur = _dot(cm_ref[...], ub)
        ui = _dot(sm_ref[...], ub)
        ka = ka_ref[0, 0, :, cols]
        kb = kb_ref[0, 0, :, cols]
        kd = jnp.where(row == 0, kn_ref[0, 0, :, cols], ka)
        yr_ref[0, :, cols] = (ur * ka - ui * kb).astype(BF16)
        yi_ref[0, :, cols] = (ur * kb + ui * kd).astype(BF16)


def _hyena_fwd(u, u_block, conv_w, tabs, ka, kb, kn, layer, order):
    B, L, _ = u.shape
    args = [u]
    specs = [pl.BlockSpec((1, L, W_BRANCH), lambda b: (b, 0, u_block))]
    if conv_w is not None:
        args.append(conv_w)
        specs.append(pl.BlockSpec((3, W_BRANCH), lambda b: (0, u_block), pipeline_mode=pl.Buffered(1)))
    kspec = pl.BlockSpec((1, 1, L, W_BRANCH), lambda b: (layer, order, 0, 0), pipeline_mode=pl.Buffered(1))
    nspec = pl.BlockSpec((1, 1, 1, W_BRANCH), lambda b: (layer, order, 0, 0), pipeline_mode=pl.Buffered(1))
    args += [tabs["cm"], tabs["sm"], ka, kb, kn]
    specs += [_const_spec((L, L)), _const_spec((L, L)), kspec, kspec, nspec]
    out = pl.BlockSpec((1, L, W_BRANCH), lambda b: (b, 0, 0))
    return pl.pallas_call(
        functools.partial(_hyena_fwd_kernel, conv_in=conv_w is not None),
        grid=(B,),
        in_specs=specs,
        out_specs=[out, out],
        out_shape=[jax.ShapeDtypeStruct((B, L, W_BRANCH), BF16)] * 2,
        compiler_params=_params(1),
        name=f"hyena_fwd_{L}_{order}",
    )(*args)


def _hyena_inv_kernel(*refs, last):
    it = iter(refs)
    yr_ref, yi_ref, icm_ref, ism_ref = next(it), next(it), next(it), next(it)
    gp_ref, gw_ref, rs_ref = next(it), next(it), next(it)
    rw_ref = None if last else next(it)
    bias_ref = next(it)
    sg_ref = next(it) if last else None
    o_ref = next(it)
    for c in range(W_BRANCH // HALF):
        cols = slice(c * HALF, (c + 1) * HALF)
        y = _dot(icm_ref[...], yr_ref[0, :, cols]) + _dot(ism_ref[...], yi_ref[0, :, cols])
        gate = _conv3(gp_ref[0, :, cols].astype(F32), gw_ref[:, cols])
        resid = rs_ref[0, :, cols].astype(F32)
        if not last:
            resid = _conv3(resid, rw_ref[:, cols])
        z = gate * (y + resid * bias_ref[0, :, cols])
        if last:
            z = z * sg_ref[0, :, cols].astype(F32)
        o_ref[0, :, cols] = z.astype(BF16)


def _hyena_inv(yr, yi, tabs, hpre, conv_h, z1, bias_l, sgh, order):
    B, L, _ = yr.shape
    last = order == 1
    blk = lambda j: pl.BlockSpec((1, L, W_BRANCH), lambda b: (b, 0, j))
    cw = lambda j: pl.BlockSpec((3, W_BRANCH), lambda b: (0, j), pipeline_mode=pl.Buffered(1))
    args = [yr, yi, tabs["icm"], tabs["ism"], hpre, conv_h]
    specs = [blk(0), blk(0), _const_spec((L, L)), _const_spec((L, L)), blk(1 + order), cw(1 + order)]
    if last:
        args += [z1]
        specs += [blk(0)]
    else:
        args += [hpre, conv_h]
        specs += [blk(0), cw(0)]
    args.append(bias_l)
    specs.append(pl.BlockSpec((1, 1, W_BRANCH), lambda b: (order, 0, 0), pipeline_mode=pl.Buffered(1)))
    if last:
        args.append(sgh)
        specs.append(blk(0))
    return pl.pallas_call(
        functools.partial(_hyena_inv_kernel, last=last),
        grid=(B,),
        in_specs=specs,
        out_specs=blk(0),
        out_shape=jax.ShapeDtypeStruct((B, L, W_BRANCH), BF16),
        compiler_params=_params(1),
        name=f"hyena_inv_{L}_{order}",
    )(*args)


def _hyena(hpre, sgh, conv_h, bias_l, tabs, spectra, layer):
    ka, kb, kn = spectra
    yr, yi = _hyena_fwd(hpre, 0, conv_h, tabs, ka, kb, kn, layer, 0)
    z1 = _hyena_inv(yr, yi, tabs, hpre, conv_h, None, bias_l, None, 0)
    yr, yi = _hyena_fwd(z1, 0, None, tabs, ka, kb, kn, layer, 1)
    return _hyena_inv(yr, yi, tabs, hpre, conv_h, z1, bias_l, sgh, 1)


def _merge_kernel(x_ref, sc_ref, sh_ref, gt_ref, g_ref, ua_ref, up_ref, un_ref, ga_ref, ca_ref,
                  yb_ref, yc_ref, wg_ref, wb_ref, wo_ref, o_ref):
    t, nt = pl.program_id(1), pl.num_programs(1)
    x = x_ref[0]
    h = _rms_mod(x, g_ref[...], sc_ref[0], sh_ref[0]).astype(BF16)
    halo = up_ref.shape[1]
    first = jnp.where(t == 0, 0.0, up_ref[0, halo - 1:halo, :].astype(F32))
    last = jnp.where(t == nt - 1, 0.0, un_ref[0, 0:1, :].astype(F32))
    ya = ga_ref[0].astype(F32) * _conv3(ua_ref[0].astype(F32), ca_ref[...], first, last)
    ys = (ya.astype(BF16), yb_ref[0], yc_ref[0])
    merged = None
    for n in range(N_BRANCH):
        gate = jax.nn.sigmoid(_dot(h, wg_ref[:, n * D_MODEL:(n + 1) * D_MODEL]))
        term = gate * _dot(ys[n], wb_ref[n])
        merged = term if merged is None else merged + term
    out = _dot(merged.astype(BF16), wo_ref[...])
    o_ref[0] = x + gt_ref[0] * out


def _merge(x, sc, sh, gt, g, ua, ga, conv_a, yb, yc, wts):
    B, T, D = x.shape
    tm = min(T, 512)
    halo = 16
    r = tm // halo
    row = lambda b, t: (b, t, 0)
    vec = pl.BlockSpec((1, 1, D), lambda b, t: (b, 0, 0))
    br = pl.BlockSpec((1, tm, W_BRANCH), row)
    prev = pl.BlockSpec((1, halo, W_BRANCH), lambda b, t: (b, jnp.maximum(t * r - 1, 0), 0))
    nxt = pl.BlockSpec((1, halo, W_BRANCH), lambda b, t: (b, jnp.minimum((t + 1) * r, T // halo - 1), 0))
    return pl.pallas_call(
        _merge_kernel,
        grid=(B, T // tm),
        in_specs=[pl.BlockSpec((1, tm, D), row), vec, vec, vec, _const_spec((1, D)),
                  br, prev, nxt, br, _const_spec((3, W_BRANCH)), br, br,
                  _const_spec(wts["wg"].shape), _const_spec(wts["wb"].shape), _const_spec(wts["wo"].shape)],
        out_specs=pl.BlockSpec((1, tm, D), row),
        out_shape=jax.ShapeDtypeStruct((B, T, D), F32),
        compiler_params=_params(2),
        name=f"merge_{T}",
    )(x, sc, sh, gt, g[None, :], ua, ua, ua, ga, conv_a, yb, yc, wts["wg"], wts["wb"], wts["wo"])


def _rope_tables(S):
    pos = jnp.arange(S, dtype=jnp.int32)
    coord = jnp.stack([(pos // GRID_W).astype(F32), (pos % GRID_W).astype(F32)], axis=1)
    inv = ROPE_BASE ** (-jnp.arange(ROPE_FREQS, dtype=F32) / ROPE_FREQS)
    ang = coord[:, :, None] * inv
    cos = jnp.repeat(jnp.cos(ang)[:, :, None, :], 2, axis=2).reshape(S, HEAD_DIM)
    sin = jnp.sin(ang)
    sin = jnp.stack([-sin, sin], axis=2).reshape(S, HEAD_DIM)
    return jnp.tile(cos, (1, 2)), jnp.tile(sin, (1, 2))


def _dup_heads(w):
    d = w.shape[0]
    w = w.reshape(d, N_KV_HEADS, 1, HEAD_DIM)
    return jnp.broadcast_to(w, (d, N_KV_HEADS, 2, HEAD_DIM)).reshape(d, KV_DUP_COLS)


def _layer_weights(w_in, w_branch, w_out):
    wk = _dup_heads(w_in[:, K_OFF:V_OFF])
    wv = _dup_heads(w_in[:, V_OFF:Z_OFF])
    return dict(
        wa=w_in[:, :A_COLS].astype(BF16),
        wh=w_in[:, H_OFF:C_OFF].astype(BF16),
        wq=jnp.concatenate([w_in[:, C_OFF:K_OFF], wk, wv, w_in[:, Z_OFF:G_OFF]], axis=1).astype(BF16),
        wkv=jnp.concatenate([wk, wv], axis=1).astype(BF16),
        wg=w_in[:, G_OFF:].astype(BF16),
        wb=w_branch.astype(BF16),
        wo=w_out.astype(BF16),
    )


def _mixers(x, sc, sh, gt, g, wts, consts, conv_a, conv_h, bias_l, tabs, spectra, layer, rope, ctx_kv):
    ua, ga, hpre, sgh, q, kk, vv, sz = _inproj(x, sc, sh, g, wts, consts, rope=rope, kv_only=False)
    sources = ([ctx_kv] if ctx_kv is not None else []) + [(kk, vv)]
    yc = _attention(q, sz, sources)
    yb = _hyena(hpre, sgh, conv_h, bias_l, tabs, spectra, layer)
    return _merge(x, sc, sh, gt, g, ua, ga, conv_a, yb, yc, wts), (kk, vv)


def kernel(x, c, ctx, c_ctx, norm_g, w_mod, b_mod, w_in, conv_a, conv_h, filt_w1, filt_b1, filt_w2,
           filt_b2, filt_w3, filt_b3, filt_freq, hyena_bias, q_norm_g, k_norm_g, w_branch, w_out):
    B, S, D = x.shape
    Lc = ctx.shape[1]
    cc = jnp.concatenate([c, c_ctx[None, :], jnp.zeros((MOD_ROWS - B - 1, D), F32)], axis=0)
    mod = _modulation(cc, w_mod, b_mod)

    tabs_s, tabs_c = _dft_tables(S), _dft_tables(Lc)
    filt = (filt_w1, filt_b1, filt_w2, filt_b2, filt_w3, filt_b3, filt_freq)
    spec_s = _filter_spectra(S, tabs_s, *filt)
    spec_c = _filter_spectra(Lc, tabs_c, *filt)

    cos, sin = _rope_tables(S)
    ones = jnp.ones((HEAD_DIM, HEAD_DIM), F32)
    bdq = jnp.kron(jnp.eye(N_HEADS, dtype=F32), ones).astype(BF16)
    bdk = jnp.kron(jnp.eye(2 * N_KV_HEADS, dtype=F32), ones).astype(BF16)

    for i in range(DEPTH):
        last = i == DEPTH - 1
        wts = _layer_weights(w_in[i], w_branch[i], w_out[i])
        consts = dict(gq=jnp.tile(q_norm_g[i], N_HEADS)[None, :],
                      gk=jnp.tile(k_norm_g[i], 2 * N_KV_HEADS)[None, :],
                      bdq=bdq, bdk=bdk, cos=cos, sin=sin)
        split = lambda rows: [rows[:, None, j * D:(j + 1) * D] for j in range(3)]
        sh, sc, gt = split(mod[i, :B])
        sh_c, sc_c, gt_c = [jnp.broadcast_to(v, (B, 1, D)) for v in split(mod[i, B:B + 1])]
        bias_l = hyena_bias[i][:, None, :]
        if last:
            ctx_kv = tuple(_inproj(ctx, sc_c, sh_c, norm_g[i], wts, consts, rope=False, kv_only=True))
        else:
            ctx_next, ctx_kv = _mixers(ctx, sc_c, sh_c, gt_c, norm_g[i], wts, consts, conv_a[i], conv_h[i],
                                       bias_l, tabs_c, spec_c, i, False, None)
        x, _ = _mixers(x, sc, sh, gt, norm_g[i], wts, consts, conv_a[i], conv_h[i],
                       bias_l, tabs_s, spec_s, i, True, ctx_kv)
        if not last:
            ctx = ctx_next
    return x
```

```python
import functools
import math

import jax
import jax.numpy as jnp
from jax import lax
from jax.experimental import pallas as pl
from jax.experimental.pallas import tpu as pltpu

F32 = jnp.float32
BF16 = jnp.bfloat16

D_MODEL = 1024
DEPTH = 4
GRID_W = 64
W_BRANCH = 512
N_BRANCH = 3
N_HEADS = 8
N_KV_HEADS = 2
HEAD_DIM = 64
ROPE_FREQS = HEAD_DIM // 4
ROPE_BASE = 10000.0
ATTN_SCALE = HEAD_DIM ** -0.5
HYENA_ORDER = 2
FILTER_EMB = 33
FILTER_BANDS = (FILTER_EMB - 1) // 2
FILTER_HIDDEN = 64
HYENA_TARGET = 1e-2
MIN_DECAY = math.log(HYENA_TARGET) / 1.5
MAX_DECAY = math.log(HYENA_TARGET) / 0.3
EPS = 1e-6

A_COLS = 4 * W_BRANCH
H_OFF = A_COLS
C_OFF = H_OFF + 4 * W_BRANCH
Q_COLS = N_HEADS * HEAD_DIM
KV_COLS = N_KV_HEADS * HEAD_DIM
K_OFF = C_OFF + Q_COLS
V_OFF = K_OFF + KV_COLS
Z_OFF = V_OFF + KV_COLS
G_OFF = Z_OFF + W_BRANCH

LANES = 128
VMEM_LIMIT_BYTES = 58 * 1024 * 1024
KV_DUP_COLS = 2 * KV_COLS
MOD_ROWS = 24
FILT_PAD = 64
HALF = 256


def _params(n_axes):
    return pltpu.CompilerParams(dimension_semantics=("arbitrary",) * n_axes,
                                vmem_limit_bytes=VMEM_LIMIT_BYTES)


def _const_spec(shape):
    nd = len(shape)
    return pl.BlockSpec(shape, lambda *_: (0,) * nd, pipeline_mode=pl.Buffered(1))


def _silu(v):
    return v * jax.nn.sigmoid(v)


def _dot(a, b):
    return jnp.dot(a, b, preferred_element_type=F32)


def _split(a):
    hi = a.astype(BF16)
    lo = (a - hi.astype(F32)).astype(BF16)
    return hi, lo


def _dot3(a, b):
    ah, al = _split(a)
    bh, bl = _split(b)
    return _dot(ah, bh) + (_dot(ah, bl) + _dot(al, bh))


def _rms_mod(x, g, sc, sh):
    y = x * lax.rsqrt(jnp.mean(x * x, axis=-1, keepdims=True) + EPS)
    return (y * g) * (1.0 + sc) + sh


def _shift_rows(u, first_row, last_row):
    n = u.shape[0]
    row = lax.broadcasted_iota(jnp.int32, u.shape, 0)
    prev = jnp.where(row == 0, first_row, pltpu.roll(u, 1, 0))
    nxt = jnp.where(row == n - 1, last_row, pltpu.roll(u, n - 1, 0))
    return prev, nxt


def _conv3(u, w, first_row=0.0, last_row=0.0):
    prev, nxt = _shift_rows(u, first_row, last_row)
    return prev * w[0:1] + u * w[1:2] + nxt * w[2:3]


def _mod_kernel(cc_ref, w_ref, b_ref, o_ref):
    o_ref[0] = _dot3(_silu(cc_ref[...]), w_ref[0]) + b_ref[0]


def _modulation(cc, w_mod, b_mod):
    return pl.pallas_call(
        _mod_kernel,
        grid=(DEPTH, 3),
        in_specs=[pl.BlockSpec((MOD_ROWS, D_MODEL), lambda i, j: (0, 0)),
                  pl.BlockSpec((1, D_MODEL, D_MODEL), lambda i, j: (i, 0, j)),
                  pl.BlockSpec((1, 1, D_MODEL), lambda i, j: (i, 0, j))],
        out_specs=pl.BlockSpec((1, MOD_ROWS, D_MODEL), lambda i, j: (i, 0, j)),
        out_shape=jax.ShapeDtypeStruct((DEPTH, MOD_ROWS, 3 * D_MODEL), F32),
        compiler_params=_params(2),
        name="modulation",
    )(cc, w_mod, b_mod.reshape(DEPTH, 1, 3 * D_MODEL))


def _dft_tables(L):
    n, m_half = 2 * L, L // 2
    idx = jnp.arange(m_half, dtype=jnp.int32)
    q = ((2 * idx[:, None] + 1) * idx[None, :]) % n
    ang = q.astype(F32) * (2.0 * math.pi / n)
    cm_hi, cm_lo = _split(jnp.cos(ang))
    sm_hi, sm_lo = _split(-jnp.sin(ang))
    tw = (2 * idx + 1).astype(F32) * (math.pi / n)
    lanes = lambda v: jnp.broadcast_to(v[:, None], (m_half, LANES))
    return dict(cm=cm_hi, cm_lo=cm_lo, sm=sm_hi, sm_lo=sm_lo,
                icm=cm_hi.T * (1.0 / L), ism=sm_hi.T * (1.0 / L),
                twr=lanes(jnp.cos(tw)), twi=lanes(-jnp.sin(tw)))


def _lane_tile(ref, width):
    return jnp.concatenate([ref[...]] * (width // LANES), axis=1)


def _half_dft(e, o, twr, twi, cm_ref, sm_ref, cml_ref=None, sml_ref=None):
    c = e.shape[1]
    cat = jnp.concatenate([e, o], axis=1)
    if cml_ref is None:
        b = cat.astype(BF16)
        re, im = _dot(cm_ref[...], b), _dot(sm_ref[...], b)
    else:
        hi, lo = _split(cat)
        re = _dot(cm_ref[...], hi) + (_dot(cm_ref[...], lo) + _dot(cml_ref[...], hi))
        im = _dot(sm_ref[...], hi) + (_dot(sm_ref[...], lo) + _dot(sml_ref[...], hi))
    er, orr, ei, oi = re[:, :c], re[:, c:], im[:, :c], im[:, c:]
    tr = twr * orr - twi * oi
    ti = twr * oi + twi * orr
    return er + tr, ei + ti, er - tr, ei - ti


def _filter_kernel(feats_ref, w1_ref, b1_ref, w2_ref, b2_ref, w3f_ref, w3b_ref, b3f_ref, b3b_ref,
                   fq_ref, dl_ref, cmh_ref, cml_ref, smh_ref, sml_ref, twr_ref, twi_ref, k_ref, *, L):
    m_half = L // 2
    c = dl_ref.shape[1]
    fq = fq_ref[0]
    h = jnp.sin(fq[0:1] * (_dot3(feats_ref[...], w1_ref[0]) + b1_ref[0]))
    h = jnp.sin(fq[1:2] * (_dot3(h, w2_ref[0]) + b2_ref[0]))
    row = lax.broadcasted_iota(jnp.int32, (L, c), 0)
    lag = jnp.where(row < m_half, 2 * row, 2 * (row - m_half) + 1)
    win = jnp.exp(-(lag.astype(F32) * (1.0 / (L - 1))) * dl_ref[...])
    hf = (_dot3(h, w3f_ref[0]) + b3f_ref[0]) * win
    hb = jnp.where(lag == 0, 0.0, (_dot3(h, w3b_ref[0]) + b3b_ref[0]) * win)
    nrm = jnp.sum(jnp.abs(hf), axis=0, keepdims=True) + jnp.sum(jnp.abs(hb), axis=0, keepdims=True)
    hf, hb = hf / nrm, hb / nrm
    twr, twi = _lane_tile(twr_ref, 2 * c), _lane_tile(twi_ref, 2 * c)
    p = _half_dft(jnp.concatenate([hf[:m_half], hb[:m_half]], axis=1),
                  jnp.concatenate([hf[m_half:], hb[m_half:]], axis=1),
                  twr, twi, cmh_ref, smh_ref, cml_ref, sml_ref)
    sign = (1.0, -1.0, 1.0, -1.0)
    for j in range(4):
        k_ref[0, 0, j] = p[j][:, :c] + sign[j] * p[j][:, c:]


def _filter_spectra(L, tabs, filt_w1, filt_b1, filt_w2, filt_b2, filt_w3, filt_b3, filt_freq):
    m_half = L // 2
    t = jnp.linspace(0.0, 1.0, L, dtype=F32)[:, None]
    bands = jnp.linspace(1e-4, FILTER_BANDS - 1, FILTER_BANDS, dtype=F32)
    w = (2.0 * math.pi / L) * jnp.arange(L, dtype=F32)[:, None]
    feats = jnp.concatenate([t, jnp.cos(w * bands), jnp.sin(w * bands)], axis=-1)
    feats = jnp.concatenate([feats[0::2], feats[1::2]], axis=0)
    feats = jnp.pad(feats, ((0, 0), (0, FILT_PAD - FILTER_EMB)))
    w1 = jnp.pad(filt_w1, ((0, 0), (0, FILT_PAD - FILTER_EMB), (0, 0)))
    deltas = jnp.abs(jnp.linspace(MIN_DECAY, MAX_DECAY, W_BRANCH, dtype=F32))[None, :]
    nc = W_BRANCH // HALF
    lay = lambda i, o, c: (i, 0, 0)
    fwd = lambda i, o, c: (i, 0, o * 2 * nc + c)
    bwd = lambda i, o, c: (i, 0, o * 2 * nc + nc + c)
    tab = _const_spec((m_half, m_half))
    tw = _const_spec((m_half, LANES))
    return pl.pallas_call(
        functools.partial(_filter_kernel, L=L),
        grid=(DEPTH, HYENA_ORDER, nc),
        in_specs=[_const_spec((L, FILT_PAD)),
                  pl.BlockSpec((1, FILT_PAD, FILTER_HIDDEN), lay),
                  pl.BlockSpec((1, 1, FILTER_HIDDEN), lay),
                  pl.BlockSpec((1, FILTER_HIDDEN, FILTER_HIDDEN), lay),
                  pl.BlockSpec((1, 1, FILTER_HIDDEN), lay),
                  pl.BlockSpec((1, FILTER_HIDDEN, HALF), fwd),
                  pl.BlockSpec((1, FILTER_HIDDEN, HALF), bwd),
                  pl.BlockSpec((1, 1, HALF), fwd),
                  pl.BlockSpec((1, 1, HALF), bwd),
                  pl.BlockSpec((1, 2, FILTER_HIDDEN), lay),
                  pl.BlockSpec((1, HALF), lambda i, o, c: (0, c)),
                  tab, tab, tab, tab, tw, tw],
        out_specs=pl.BlockSpec((1, 1, 4, m_half, HALF), lambda i, o, c: (i, o, 0, 0, c)),
        out_shape=jax.ShapeDtypeStruct((DEPTH, HYENA_ORDER, 4, m_half, W_BRANCH), F32),
        compiler_params=_params(3),
        name=f"filter_spectra_{L}",
    )(feats, w1, filt_b1[:, None], filt_w2, filt_b2[:, None], filt_w3, filt_w3,
      filt_b3[:, None], filt_b3[:, None], filt_freq, deltas,
      tabs["cm"], tabs["cm_lo"], tabs["sm"], tabs["sm_lo"], tabs["twr"], tabs["twi"])


def _head_norm(v, bd_ref, g):
    hi, lo = _split(v * v)
    ss = _dot(hi, bd_ref[...]) + _dot(lo, bd_ref[...])
    return v * lax.rsqrt(ss * (1.0 / HEAD_DIM) + EPS) * g


def _rope(v, cos, sin):
    n = v.shape[1] // LANES
    lane = lax.broadcasted_iota(jnp.int32, (v.shape[0], LANES), 1)
    first = (lane % 32) < 16
    out = []
    for i in range(n):
        c = v[:, i * LANES:(i + 1) * LANES]
        partner = jnp.where(first, pltpu.roll(c, LANES - 16, 1), pltpu.roll(c, 16, 1))
        out.append(c * cos + partner * sin)
    return jnp.concatenate(out, axis=1)


def _inproj_kernel(*refs, rope, kv_only):
    it = iter(refs)
    x_ref, sc_ref, sh_ref, g_ref = next(it), next(it), next(it), next(it)
    if not kv_only:
        wa_ref, wh_ref = next(it), next(it)
    wq_ref = next(it)
    if not kv_only:
        gq_ref, bdq_ref = next(it), next(it)
    gk_ref, bdk_ref = next(it), next(it)
    if rope:
        cos_ref, sin_ref = next(it), next(it)
    outs = list(it)

    h = _rms_mod(x_ref[0], g_ref[...], sc_ref[0], sh_ref[0]).astype(BF16)
    w = W_BRANCH
    if kv_only:
        kk_ref, vv_ref = outs
        acc = _dot(h, wq_ref[...])
        k = _head_norm(acc[:, :KV_DUP_COLS], bdk_ref, gk_ref[...])
        kk_ref[0] = k.astype(BF16)
        vv_ref[0] = acc[:, KV_DUP_COLS:].astype(BF16)
        return

    ua_ref, ga_ref, hp_ref, sgh_ref, q_ref, kk_ref, vv_ref, sz_ref = outs
    acc = _dot(h, wa_ref[...])
    ua_ref[0] = (acc[:, 2 * w:3 * w] * acc[:, :w]).astype(BF16)
    ga_ref[0] = (acc[:, w:2 * w] * _silu(acc[:, 3 * w:])).astype(BF16)
    acc = _dot(h, wh_ref[...])
    hp_ref[0] = acc[:, :3 * w].astype(BF16)
    sgh_ref[0] = _silu(acc[:, 3 * w:]).astype(BF16)
    acc = _dot(h, wq_ref[...])
    q = _head_norm(acc[:, :Q_COLS], bdq_ref, gq_ref[...])
    k = _head_norm(acc[:, Q_COLS:Q_COLS + KV_DUP_COLS], bdk_ref, gk_ref[...])
    if rope:
        q = _rope(q, cos_ref[...], sin_ref[...])
        k = _rope(k, cos_ref[...], sin_ref[...])
    q_ref[0] = (q * ATTN_SCALE).astype(BF16)
    kk_ref[0] = k.astype(BF16)
    vv_ref[0] = acc[:, Q_COLS + KV_DUP_COLS:Q_COLS + 2 * KV_DUP_COLS].astype(BF16)
    sz_ref[0] = _silu(acc[:, Q_COLS + 2 * KV_DUP_COLS:]).astype(BF16)


def _inproj(x, sc, sh, g, wts, consts, *, rope, kv_only):
    B, T, D = x.shape
    tm = min(T, 512)
    row = lambda b, t: (b, t, 0)
    vec = pl.BlockSpec((1, 1, D), lambda b, t: (b, 0, 0))
    args = [x, sc, sh, g[None, :]]
    specs = [pl.BlockSpec((1, tm, D), row), vec, vec, _const_spec((1, D))]
    if not kv_only:
        args += [wts["wa"], wts["wh"], wts["wq"], consts["gq"], consts["bdq"]]
        specs += [_const_spec(wts["wa"].shape), _const_spec(wts["wh"].shape),
                  _const_spec(wts["wq"].shape), _const_spec(consts["gq"].shape),
                  _const_spec(consts["bdq"].shape)]
    else:
        args += [wts["wkv"]]
        specs += [_const_spec(wts["wkv"].shape)]
    args += [consts["gk"], consts["bdk"]]
    specs += [_const_spec(consts["gk"].shape), _const_spec(consts["bdk"].shape)]
    if rope:
        args += [consts["cos"], consts["sin"]]
        specs += [pl.BlockSpec((tm, LANES), lambda b, t: (t, 0))] * 2
    widths = ([KV_DUP_COLS, KV_DUP_COLS] if kv_only else
              [W_BRANCH, W_BRANCH, 3 * W_BRANCH, W_BRANCH, Q_COLS, KV_DUP_COLS, KV_DUP_COLS, W_BRANCH])
    return pl.pallas_call(
        functools.partial(_inproj_kernel, rope=rope, kv_only=kv_only),
        grid=(B, T // tm),
        in_specs=specs,
        out_specs=[pl.BlockSpec((1, tm, n), row) for n in widths],
        out_shape=[jax.ShapeDtypeStruct((B, T, n), BF16) for n in widths],
        compiler_params=_params(2),
        name=f"inproj_{T}" + ("_kv" if kv_only else ""),
    )(*args)


def _attn_kernel(*refs, n_src):
    q_ref, sz_ref = refs[0], refs[1]
    kv = refs[2:2 + 2 * n_src]
    o_ref = refs[-1]
    tq = q_ref.shape[1]
    low = lax.broadcasted_iota(jnp.int32, (tq, LANES), 1) < HEAD_DIM
    nt = (((1,), (1,)), ((), ()))
    for j in range(2):
        cols = slice(j * LANES, (j + 1) * LANES)
        qp = q_ref[0, :, cols].astype(F32)
        halves = []
        for sel in (low, jnp.logical_not(low)):
            qm = jnp.where(sel, qp, 0.0).astype(BF16)
            s = [lax.dot_general(qm, kv[2 * i][0], nt, preferred_element_type=F32) for i in range(n_src)]
            m = functools.reduce(jnp.maximum, [jnp.max(v, axis=-1, keepdims=True) for v in s])
            p = [jnp.exp(v - m) for v in s]
            l = functools.reduce(jnp.add, [jnp.sum(v, axis=-1, keepdims=True) for v in p])
            o = functools.reduce(jnp.add, [_dot(p[i].astype(BF16), kv[2 * i + 1][0]) for i in range(n_src)])
            halves.append(o / l)
        o = jnp.where(low, halves[0], halves[1])
        o_ref[0, :, cols] = (o * sz_ref[0, :, cols].astype(F32)).astype(BF16)


def _attention(q, sz, sources):
    B, T, _ = q.shape
    tq = min(T, 256)
    gw = Q_COLS // N_KV_HEADS
    qspec = pl.BlockSpec((1, tq, gw), lambda b, g, t: (b, t, g))
    args, specs = [q, sz], [qspec, qspec]
    for kk, vv in sources:
        spec = pl.BlockSpec((1, kk.shape[1], LANES), lambda b, g, t: (b, 0, g))
        args += [kk, vv]
        specs += [spec, spec]
    return pl.pallas_call(
        functools.partial(_attn_kernel, n_src=len(sources)),
        grid=(B, N_KV_HEADS, T // tq),
        in_specs=specs,
        out_specs=qspec,
        out_shape=jax.ShapeDtypeStruct((B, T, Q_COLS), BF16),
        compiler_params=_params(3),
        name=f"attention_{T}",
    )(*args)


def _deinterleave(x, scr):
    n, c = x.shape
    slabs = range(c // LANES)
    for j in slabs:
        scr[j][...] = x[:, j * LANES:(j + 1) * LANES]
    pick = lambda s: jnp.concatenate([scr[j][pl.ds(s, n // 2, stride=2), :] for j in slabs], axis=1)
    return pick(0), pick(1)


def _interleave(e, o, scr):
    m, c = e.shape
    slabs = range(c // LANES)
    for j in slabs:
        scr[j][pl.ds(0, m, stride=2), :] = e[:, j * LANES:(j + 1) * LANES]
        scr[j][pl.ds(1, m, stride=2), :] = o[:, j * LANES:(j + 1) * LANES]
    return jnp.concatenate([scr[j][...] for j in slabs], axis=1)


def _conv3_split(e, o, w):
    m = e.shape[0]
    row = lax.broadcasted_iota(jnp.int32, e.shape, 0)
    o_prev = jnp.where(row == 0, 0.0, pltpu.roll(o, 1, 0))
    e_next = jnp.where(row == m - 1, 0.0, pltpu.roll(e, m - 1, 0))
    return (o_prev * w[0:1] + e * w[1:2] + o * w[2:3],
            e * w[0:1] + o * w[1:2] + e_next * w[2:3])


def _long_conv_split(e, o, k, twr, twi, cm_ref, sm_ref, icm_ref, ism_ref):
    c = e.shape[1]
    pr, pi, qr, qi = _half_dft(e, o, twr, twi, cm_ref, sm_ref)
    ypr, ypi = pr * k[0] - pi * k[1], pr * k[1] + pi * k[0]
    yqr, yqi = qr * k[2] - qi * k[3], qr * k[3] + qi * k[2]
    dr, di = ypr - yqr, ypi - yqi
    re = jnp.concatenate([ypr + yqr, twr * dr + twi * di], axis=1).astype(BF16)
    im = jnp.concatenate([ypi + yqi, twr * di - twi * dr], axis=1).astype(BF16)
    y = _dot(icm_ref[...], re) + _dot(ism_ref[...], im)
    return y[:, :c], y[:, c:]


def _hyena_kernel(hv_ref, hx1_ref, hx2_ref, sg_ref, wv_ref, wx1_ref, wx2_ref, bias_ref,
                  cm_ref, sm_ref, icm_ref, ism_ref, twr_ref, twi_ref, k_ref, o_ref, *scratch):
    c = o_ref.shape[2]
    ns = c // LANES
    s_v, s_x1, s_x2, s_g, s_o = [scratch[i * ns:(i + 1) * ns] for i in range(5)]
    twr, twi = _lane_tile(twr_ref, c), _lane_tile(twi_ref, c)
    tabs = (cm_ref, sm_ref, icm_ref, ism_ref)
    split = lambda ref, scr: _deinterleave(ref[0].astype(F32), scr)
    ve, vo = _conv3_split(*split(hv_ref, s_v), wv_ref[...])
    x1e, x1o = _conv3_split(*split(hx1_ref, s_x1), wx1_ref[...])
    x2e, x2o = _conv3_split(*split(hx2_ref, s_x2), wx2_ref[...])
    ge, go = split(sg_ref, s_g)
    b0, b1 = bias_ref[0:1], bias_ref[1:2]
    ye, yo = _long_conv_split(ve, vo, [k_ref[0, 0, j] for j in range(4)], twr, twi, *tabs)
    z1e, z1o = x1e * (ye + ve * b0), x1o * (yo + vo * b0)
    ye, yo = _long_conv_split(z1e, z1o, [k_ref[0, 1, j] for j in range(4)], twr, twi, *tabs)
    z2e, z2o = x2e * (ye + z1e * b1), x2o * (yo + z1o * b1)
    o_ref[0] = _interleave(z2e * ge, z2o * go, s_o).astype(BF16)


def _hyena(hpre, sgh, conv_h, bias, tabs, spectra, layer):
    B, L, _ = hpre.shape
    m_half = L // 2
    nc = W_BRANCH // HALF
    blk = lambda j: pl.BlockSpec((1, L, HALF), lambda c, b: (b, 0, j * nc + c))
    cw = lambda j: pl.BlockSpec((3, HALF), lambda c, b: (0, j * nc + c))
    tab = _const_spec((m_half, m_half))
    tw = _const_spec((m_half, LANES))
    kspec = pl.BlockSpec((1, HYENA_ORDER, 4, m_half, HALF), lambda c, b: (layer, 0, 0, 0, c),
                         pipeline_mode=pl.Buffered(1))
    return pl.pallas_call(
        _hyena_kernel,
        grid=(nc, B),
        in_specs=[blk(0), blk(1), blk(2), blk(0), cw(0), cw(1), cw(2),
                  pl.BlockSpec((HYENA_ORDER, HALF), lambda c, b: (0, c)),
                  tab, tab, tab, tab, tw, tw, kspec],
        out_specs=blk(0),
        out_shape=jax.ShapeDtypeStruct((B, L, W_BRANCH), BF16),
        scratch_shapes=[pltpu.VMEM((L, LANES), F32)] * (5 * HALF // LANES),
        compiler_params=_params(2),
        name=f"hyena_{L}",
    )(hpre, hpre, hpre, sgh, conv_h, conv_h, conv_h, bias,
      tabs["cm"], tabs["sm"], tabs["icm"], tabs["ism"], tabs["twr"], tabs["twi"], spectra)


def _merge_kernel(x_ref, sc_ref, sh_ref, gt_ref, g_ref, ua_ref, up_ref, un_ref, ga_ref, ca_ref,
                  yb_ref, yc_ref, wg_ref, wb_ref, wo_ref, o_ref):
    t, nt = pl.program_id(1), pl.num_programs(1)
    x = x_ref[0]
    h = _rms_mod(x, g_ref[...], sc_ref[0], sh_ref[0]).astype(BF16)
    halo = up_ref.shape[1]
    first = jnp.where(t == 0, 0.0, up_ref[0, halo - 1:halo, :].astype(F32))
    last = jnp.where(t == nt - 1, 0.0, un_ref[0, 0:1, :].astype(F32))
    ya = ga_ref[0].astype(F32) * _conv3(ua_ref[0].astype(F32), ca_ref[...], first, last)
    ys = (ya.astype(BF16), yb_ref[0], yc_ref[0])
    merged = None
    for n in range(N_BRANCH):
        gate = jax.nn.sigmoid(_dot(h, wg_ref[:, n * D_MODEL:(n + 1) * D_MODEL]))
        term = gate * _dot(ys[n], wb_ref[n])
        merged = term if merged is None else merged + term
    out = _dot(merged.astype(BF16), wo_ref[...])
    o_ref[0] = x + gt_ref[0] * out


def _merge(x, sc, sh, gt, g, ua, ga, conv_a, yb, yc, wts):
    B, T, D = x.shape
    tm = min(T, 512)
    halo = 16
    r = tm // halo
    row = lambda b, t: (b, t, 0)
    vec = pl.BlockSpec((1, 1, D), lambda b, t: (b, 0, 0))
    br = pl.BlockSpec((1, tm, W_BRANCH), row)
    prev = pl.BlockSpec((1, halo, W_BRANCH), lambda b, t: (b, jnp.maximum(t * r - 1, 0), 0))
    nxt = pl.BlockSpec((1, halo, W_BRANCH), lambda b, t: (b, jnp.minimum((t + 1) * r, T // halo - 1), 0))
    return pl.pallas_call(
        _merge_kernel,
        grid=(B, T // tm),
        in_specs=[pl.BlockSpec((1, tm, D), row), vec, vec, vec, _const_spec((1, D)),
                  br, prev, nxt, br, _const_spec((3, W_BRANCH)), br, br,
                  _const_spec(wts["wg"].shape), _const_spec(wts["wb"].shape), _const_spec(wts["wo"].shape)],
        out_specs=pl.BlockSpec((1, tm, D), row),
        out_shape=jax.ShapeDtypeStruct((B, T, D), F32),
        compiler_params=_params(2),
        name=f"merge_{T}",
    )(x, sc, sh, gt, g[None, :], ua, ua, ua, ga, conv_a, yb, yc, wts["wg"], wts["wb"], wts["wo"])


def _rope_tables(S):
    pos = jnp.arange(S, dtype=jnp.int32)
    coord = jnp.stack([(pos // GRID_W).astype(F32), (pos % GRID_W).astype(F32)], axis=1)
    inv = ROPE_BASE ** (-jnp.arange(ROPE_FREQS, dtype=F32) / ROPE_FREQS)
    ang = coord[:, :, None] * inv
    cos = jnp.repeat(jnp.cos(ang)[:, :, None, :], 2, axis=2).reshape(S, HEAD_DIM)
    sin = jnp.sin(ang)
    sin = jnp.stack([-sin, sin], axis=2).reshape(S, HEAD_DIM)
    return jnp.tile(cos, (1, 2)), jnp.tile(sin, (1, 2))


def _dup_heads(w):
    d = w.shape[0]
    w = w.reshape(d, N_KV_HEADS, 1, HEAD_DIM)
    return jnp.broadcast_to(w, (d, N_KV_HEADS, 2, HEAD_DIM)).reshape(d, KV_DUP_COLS)


def _layer_weights(w_in, w_branch, w_out):
    wk = _dup_heads(w_in[:, K_OFF:V_OFF])
    wv = _dup_heads(w_in[:, V_OFF:Z_OFF])
    return dict(
        wa=w_in[:, :A_COLS].astype(BF16),
        wh=w_in[:, H_OFF:C_OFF].astype(BF16),
        wq=jnp.concatenate([w_in[:, C_OFF:K_OFF], wk, wv, w_in[:, Z_OFF:G_OFF]], axis=1).astype(BF16),
        wkv=jnp.concatenate([wk, wv], axis=1).astype(BF16),
        wg=w_in[:, G_OFF:].astype(BF16),
        wb=w_branch.astype(BF16),
        wo=w_out.astype(BF16),
    )


def _mixers(x, sc, sh, gt, g, wts, consts, conv_a, conv_h, bias, tabs, spectra, layer, rope, ctx_kv):
    ua, ga, hpre, sgh, q, kk, vv, sz = _inproj(x, sc, sh, g, wts, consts, rope=rope, kv_only=False)
    sources = ([ctx_kv] if ctx_kv is not None else []) + [(kk, vv)]
    yc = _attention(q, sz, sources)
    yb = _hyena(hpre, sgh, conv_h, bias, tabs, spectra, layer)
    return _merge(x, sc, sh, gt, g, ua, ga, conv_a, yb, yc, wts), (kk, vv)


def kernel(x, c, ctx, c_ctx, norm_g, w_mod, b_mod, w_in, conv_a, conv_h, filt_w1, filt_b1, filt_w2,
           filt_b2, filt_w3, filt_b3, filt_freq, hyena_bias, q_norm_g, k_norm_g, w_branch, w_out):
    B, S, D = x.shape
    Lc = ctx.shape[1]
    cc = jnp.concatenate([c, c_ctx[None, :], jnp.zeros((MOD_ROWS - B - 1, D), F32)], axis=0)
    mod = _modulation(cc, w_mod, b_mod)

    tabs_s, tabs_c = _dft_tables(S), _dft_tables(Lc)
    filt = (filt_w1, filt_b1, filt_w2, filt_b2, filt_w3, filt_b3, filt_freq)
    spec_s = _filter_spectra(S, tabs_s, *filt)
    spec_c = _filter_spectra(Lc, tabs_c, *filt)

    cos, sin = _rope_tables(S)
    ones = jnp.ones((HEAD_DIM, HEAD_DIM), F32)
    bdq = jnp.kron(jnp.eye(N_HEADS, dtype=F32), ones).astype(BF16)
    bdk = jnp.kron(jnp.eye(2 * N_KV_HEADS, dtype=F32), ones).astype(BF16)

    for i in range(DEPTH):
        last = i == DEPTH - 1
        wts = _layer_weights(w_in[i], w_branch[i], w_out[i])
        consts = dict(gq=jnp.tile(q_norm_g[i], N_HEADS)[None, :],
                      gk=jnp.tile(k_norm_g[i], 2 * N_KV_HEADS)[None, :],
                      bdq=bdq, bdk=bdk, cos=cos, sin=sin)
        split = lambda rows: [rows[:, None, j * D:(j + 1) * D] for j in range(3)]
        sh, sc, gt = split(mod[i, :B])
        sh_c, sc_c, gt_c = [jnp.broadcast_to(v, (B, 1, D)) for v in split(mod[i, B:B + 1])]
        if last:
            ctx_kv = tuple(_inproj(ctx, sc_c, sh_c, norm_g[i], wts, consts, rope=False, kv_only=True))
        else:
            ctx_next, ctx_kv = _mixers(ctx, sc_c, sh_c, gt_c, norm_g[i], wts, consts, conv_a[i], conv_h[i],
                                       hyena_bias[i], tabs_c, spec_c, i, False, None)
        x, _ = _mixers(x, sc, sh, gt, norm_g[i], wts, consts, conv_a[i], conv_h[i],
                       hyena_bias[i], tabs_s, spec_s, i, True, ctx_kv)
        if not last:
            ctx = ctx_next
    return x
```

```python
import functools
import math

import jax
import jax.numpy as jnp
from jax import lax
from jax.experimental import pallas as pl
from jax.experimental.pallas import tpu as pltpu

F32 = jnp.float32
BF16 = jnp.bfloat16

D_MODEL = 1024
DEPTH = 4
GRID_W = 64
W_BRANCH = 512
N_BRANCH = 3
N_HEADS = 8
N_KV_HEADS = 2
HEAD_DIM = 64
ROPE_FREQS = HEAD_DIM // 4
ROPE_BASE = 10000.0
ATTN_SCALE = HEAD_DIM ** -0.5
HYENA_ORDER = 2
FILTER_EMB = 33
FILTER_BANDS = (FILTER_EMB - 1) // 2
FILTER_HIDDEN = 64
HYENA_TARGET = 1e-2
MIN_DECAY = math.log(HYENA_TARGET) / 1.5
MAX_DECAY = math.log(HYENA_TARGET) / 0.3
EPS = 1e-6

A_COLS = 4 * W_BRANCH
H_OFF = A_COLS
C_OFF = H_OFF + 4 * W_BRANCH
Q_COLS = N_HEADS * HEAD_DIM
KV_COLS = N_KV_HEADS * HEAD_DIM
K_OFF = C_OFF + Q_COLS
V_OFF = K_OFF + KV_COLS
Z_OFF = V_OFF + KV_COLS
G_OFF = Z_OFF + W_BRANCH

LANES = 128
VMEM_LIMIT_BYTES = 58 * 1024 * 1024
KV_DUP_COLS = 2 * KV_COLS
V_TILE_COLS = 2 * KV_DUP_COLS
Q_SCALE = ATTN_SCALE * math.log2(math.e)
SAFE_LOG2_RANGE = 64.0
KEY_CHUNK = 512
MOD_ROWS = 24
FILT_PAD = 64
HALF = 256


def _params(n_axes):
    return pltpu.CompilerParams(dimension_semantics=("arbitrary",) * n_axes,
                                vmem_limit_bytes=VMEM_LIMIT_BYTES)


def _const_spec(shape):
    nd = len(shape)
    return pl.BlockSpec(shape, lambda *_: (0,) * nd, pipeline_mode=pl.Buffered(1))


def _silu(v):
    return v * jax.nn.sigmoid(v)


def _dot(a, b):
    return jnp.dot(a, b, preferred_element_type=F32)


def _split(a):
    hi = a.astype(BF16)
    lo = (a - hi.astype(F32)).astype(BF16)
    return hi, lo


def _dot3(a, b):
    ah, al = _split(a)
    bh, bl = _split(b)
    return _dot(ah, bh) + (_dot(ah, bl) + _dot(al, bh))


def _rms_mod(x, g, sc, sh):
    y = x * lax.rsqrt(jnp.mean(x * x, axis=-1, keepdims=True) + EPS)
    return (y * g) * (1.0 + sc) + sh


def _shift_rows(u, first_row, last_row):
    n = u.shape[0]
    row = lax.broadcasted_iota(jnp.int32, u.shape, 0)
    prev = jnp.where(row == 0, first_row, pltpu.roll(u, 1, 0))
    nxt = jnp.where(row == n - 1, last_row, pltpu.roll(u, n - 1, 0))
    return prev, nxt


def _conv3(u, w, first_row=0.0, last_row=0.0):
    prev, nxt = _shift_rows(u, first_row, last_row)
    return prev * w[0:1] + u * w[1:2] + nxt * w[2:3]


def _mod_kernel(cc_ref, w_ref, b_ref, o_ref):
    o_ref[0] = _dot3(_silu(cc_ref[...]), w_ref[0]) + b_ref[0]


def _modulation(cc, w_mod, b_mod):
    return pl.pallas_call(
        _mod_kernel,
        grid=(DEPTH, 3),
        in_specs=[pl.BlockSpec((MOD_ROWS, D_MODEL), lambda i, j: (0, 0)),
                  pl.BlockSpec((1, D_MODEL, D_MODEL), lambda i, j: (i, 0, j)),
                  pl.BlockSpec((1, 1, D_MODEL), lambda i, j: (i, 0, j))],
        out_specs=pl.BlockSpec((1, MOD_ROWS, D_MODEL), lambda i, j: (i, 0, j)),
        out_shape=jax.ShapeDtypeStruct((DEPTH, MOD_ROWS, 3 * D_MODEL), F32),
        compiler_params=_params(2),
        name="modulation",
    )(cc, w_mod, b_mod.reshape(DEPTH, 1, 3 * D_MODEL))


def _dft_tables(L):
    n, m_half = 2 * L, L // 2
    idx = jnp.arange(m_half, dtype=jnp.int32)
    q = ((2 * idx[:, None] + 1) * idx[None, :]) % n
    ang = q.astype(F32) * (2.0 * math.pi / n)
    cm_hi, cm_lo = _split(jnp.cos(ang))
    sm_hi, sm_lo = _split(-jnp.sin(ang))
    tw = (2 * idx + 1).astype(F32) * (math.pi / n)
    lanes = lambda v: jnp.broadcast_to(v[:, None], (m_half, LANES))
    return dict(cm=cm_hi, cm_lo=cm_lo, sm=sm_hi, sm_lo=sm_lo,
                icm=cm_hi.T * (1.0 / L), ism=sm_hi.T * (1.0 / L),
                twr=lanes(jnp.cos(tw)), twi=lanes(-jnp.sin(tw)))


def _lane_tile(ref, width):
    return jnp.concatenate([ref[...]] * (width // LANES), axis=1)


def _half_dft(e, o, twr, twi, cm_ref, sm_ref, cml_ref=None, sml_ref=None):
    c = e.shape[1]
    cat = jnp.concatenate([e, o], axis=1)
    if cml_ref is None:
        b = cat.astype(BF16)
        re, im = _dot(cm_ref[...], b), _dot(sm_ref[...], b)
    else:
        hi, lo = _split(cat)
        re = _dot(cm_ref[...], hi) + (_dot(cm_ref[...], lo) + _dot(cml_ref[...], hi))
        im = _dot(sm_ref[...], hi) + (_dot(sm_ref[...], lo) + _dot(sml_ref[...], hi))
    er, orr, ei, oi = re[:, :c], re[:, c:], im[:, :c], im[:, c:]
    tr = twr * orr - twi * oi
    ti = twr * oi + twi * orr
    return er + tr, ei + ti, er - tr, ei - ti


def _filter_kernel(feats_ref, w1_ref, b1_ref, w2_ref, b2_ref, w3f_ref, w3b_ref, b3f_ref, b3b_ref,
                   fq_ref, dl_ref, cmh_ref, cml_ref, smh_ref, sml_ref, twr_ref, twi_ref, k_ref, *, L):
    m_half = L // 2
    c = dl_ref.shape[1]
    fq = fq_ref[0]
    h = jnp.sin(fq[0:1] * (_dot3(feats_ref[...], w1_ref[0]) + b1_ref[0]))
    h = jnp.sin(fq[1:2] * (_dot3(h, w2_ref[0]) + b2_ref[0]))
    row = lax.broadcasted_iota(jnp.int32, (L, c), 0)
    lag = jnp.where(row < m_half, 2 * row, 2 * (row - m_half) + 1)
    win = jnp.exp(-(lag.astype(F32) * (1.0 / (L - 1))) * dl_ref[...])
    hf = (_dot3(h, w3f_ref[0]) + b3f_ref[0]) * win
    hb = jnp.where(lag == 0, 0.0, (_dot3(h, w3b_ref[0]) + b3b_ref[0]) * win)
    nrm = jnp.sum(jnp.abs(hf), axis=0, keepdims=True) + jnp.sum(jnp.abs(hb), axis=0, keepdims=True)
    hf, hb = hf / nrm, hb / nrm
    twr, twi = _lane_tile(twr_ref, 2 * c), _lane_tile(twi_ref, 2 * c)
    p = _half_dft(jnp.concatenate([hf[:m_half], hb[:m_half]], axis=1),
                  jnp.concatenate([hf[m_half:], hb[m_half:]], axis=1),
                  twr, twi, cmh_ref, smh_ref, cml_ref, sml_ref)
    sign = (1.0, -1.0, 1.0, -1.0)
    for j in range(4):
        k_ref[0, 0, j] = p[j][:, :c] + sign[j] * p[j][:, c:]


def _filter_spectra(L, tabs, filt_w1, filt_b1, filt_w2, filt_b2, filt_w3, filt_b3, filt_freq):
    m_half = L // 2
    t = jnp.linspace(0.0, 1.0, L, dtype=F32)[:, None]
    bands = jnp.linspace(1e-4, FILTER_BANDS - 1, FILTER_BANDS, dtype=F32)
    w = (2.0 * math.pi / L) * jnp.arange(L, dtype=F32)[:, None]
    feats = jnp.concatenate([t, jnp.cos(w * bands), jnp.sin(w * bands)], axis=-1)
    feats = jnp.concatenate([feats[0::2], feats[1::2]], axis=0)
    feats = jnp.pad(feats, ((0, 0), (0, FILT_PAD - FILTER_EMB)))
    w1 = jnp.pad(filt_w1, ((0, 0), (0, FILT_PAD - FILTER_EMB), (0, 0)))
    deltas = jnp.abs(jnp.linspace(MIN_DECAY, MAX_DECAY, W_BRANCH, dtype=F32))[None, :]
    nc = W_BRANCH // HALF
    lay = lambda i, o, c: (i, 0, 0)
    fwd = lambda i, o, c: (i, 0, o * 2 * nc + c)
    bwd = lambda i, o, c: (i, 0, o * 2 * nc + nc + c)
    tab = _const_spec((m_half, m_half))
    tw = _const_spec((m_half, LANES))
    return pl.pallas_call(
        functools.partial(_filter_kernel, L=L),
        grid=(DEPTH, HYENA_ORDER, nc),
        in_specs=[_const_spec((L, FILT_PAD)),
                  pl.BlockSpec((1, FILT_PAD, FILTER_HIDDEN), lay),
                  pl.BlockSpec((1, 1, FILTER_HIDDEN), lay),
                  pl.BlockSpec((1, FILTER_HIDDEN, FILTER_HIDDEN), lay),
                  pl.BlockSpec((1, 1, FILTER_HIDDEN), lay),
                  pl.BlockSpec((1, FILTER_HIDDEN, HALF), fwd),
                  pl.BlockSpec((1, FILTER_HIDDEN, HALF), bwd),
                  pl.BlockSpec((1, 1, HALF), fwd),
                  pl.BlockSpec((1, 1, HALF), bwd),
                  pl.BlockSpec((1, 2, FILTER_HIDDEN), lay),
                  pl.BlockSpec((1, HALF), lambda i, o, c: (0, c)),
                  tab, tab, tab, tab, tw, tw],
        out_specs=pl.BlockSpec((1, 1, 4, m_half, HALF), lambda i, o, c: (i, o, 0, 0, c)),
        out_shape=jax.ShapeDtypeStruct((DEPTH, HYENA_ORDER, 4, m_half, W_BRANCH), F32),
        compiler_params=_params(3),
        name=f"filter_spectra_{L}",
    )(feats, w1, filt_b1[:, None], filt_w2, filt_b2[:, None], filt_w3, filt_w3,
      filt_b3[:, None], filt_b3[:, None], filt_freq, deltas,
      tabs["cm"], tabs["cm_lo"], tabs["sm"], tabs["sm_lo"], tabs["twr"], tabs["twi"])


def _head_norm(v, bd_ref, g):
    hi, lo = _split(v * v)
    ss = _dot(hi, bd_ref[...]) + _dot(lo, bd_ref[...])
    return v * lax.rsqrt(ss * (1.0 / HEAD_DIM) + EPS) * g


def _rope(v, cos, sin):
    n = v.shape[1] // LANES
    lane = lax.broadcasted_iota(jnp.int32, (v.shape[0], LANES), 1)
    first = (lane % 32) < 16
    out = []
    for i in range(n):
        c = v[:, i * LANES:(i + 1) * LANES]
        partner = jnp.where(first, pltpu.roll(c, LANES - 16, 1), pltpu.roll(c, 16, 1))
        out.append(c * cos + partner * sin)
    return jnp.concatenate(out, axis=1)


def _value_tiles(vd):
    low = (lax.broadcasted_iota(jnp.int32, vd.shape, 1) % LANES) < HEAD_DIM
    va, vb = jnp.where(low, vd, 1.0), jnp.where(low, 1.0, vd)
    tiles = []
    for g in range(N_KV_HEADS):
        cols = slice(g * LANES, (g + 1) * LANES)
        tiles += [va[:, cols], vb[:, cols]]
    return jnp.concatenate(tiles, axis=1).astype(BF16)


def _inproj_kernel(*refs, rope, kv_only):
    it = iter(refs)
    x_ref, sc_ref, sh_ref, g_ref = next(it), next(it), next(it), next(it)
    if not kv_only:
        wa_ref, wh_ref = next(it), next(it)
    wq_ref = next(it)
    if not kv_only:
        gq_ref, bdq_ref = next(it), next(it)
    gk_ref, bdk_ref = next(it), next(it)
    if rope:
        cos_ref, sin_ref = next(it), next(it)
    outs = list(it)

    h = _rms_mod(x_ref[0], g_ref[...], sc_ref[0], sh_ref[0]).astype(BF16)
    w = W_BRANCH
    if kv_only:
        kk_ref, vv_ref = outs
        acc = _dot(h, wq_ref[...])
        k = _head_norm(acc[:, :KV_DUP_COLS], bdk_ref, gk_ref[...])
        kk_ref[0] = k.astype(BF16)
        vv_ref[0] = _value_tiles(acc[:, KV_DUP_COLS:])
        return

    ua_ref, ga_ref, hp_ref, sgh_ref, q_ref, kk_ref, vv_ref, sz_ref = outs
    acc = _dot(h, wa_ref[...])
    ua_ref[0] = (acc[:, 2 * w:3 * w] * acc[:, :w]).astype(BF16)
    ga_ref[0] = (acc[:, w:2 * w] * _silu(acc[:, 3 * w:])).astype(BF16)
    acc = _dot(h, wh_ref[...])
    hp_ref[0] = acc[:, :3 * w].astype(BF16)
    sgh_ref[0] = _silu(acc[:, 3 * w:]).astype(BF16)
    acc = _dot(h, wq_ref[...])
    q = _head_norm(acc[:, :Q_COLS], bdq_ref, gq_ref[...])
    k = _head_norm(acc[:, Q_COLS:Q_COLS + KV_DUP_COLS], bdk_ref, gk_ref[...])
    if rope:
        q = _rope(q, cos_ref[...], sin_ref[...])
        k = _rope(k, cos_ref[...], sin_ref[...])
    q_ref[0] = (q * Q_SCALE).astype(BF16)
    kk_ref[0] = k.astype(BF16)
    vv_ref[0] = _value_tiles(acc[:, Q_COLS + KV_DUP_COLS:Q_COLS + 2 * KV_DUP_COLS])
    sz_ref[0] = _silu(acc[:, Q_COLS + 2 * KV_DUP_COLS:]).astype(BF16)


def _inproj(x, sc, sh, g, wts, consts, *, rope, kv_only):
    B, T, D = x.shape
    tm = min(T, 512)
    row = lambda b, t: (b, t, 0)
    vec = pl.BlockSpec((1, 1, D), lambda b, t: (b, 0, 0))
    args = [x, sc, sh, g[None, :]]
    specs = [pl.BlockSpec((1, tm, D), row), vec, vec, _const_spec((1, D))]
    if not kv_only:
        args += [wts["wa"], wts["wh"], wts["wq"], consts["gq"], consts["bdq"]]
        specs += [_const_spec(wts["wa"].shape), _const_spec(wts["wh"].shape),
                  _const_spec(wts["wq"].shape), _const_spec(consts["gq"].shape),
                  _const_spec(consts["bdq"].shape)]
    else:
        args += [wts["wkv"]]
        specs += [_const_spec(wts["wkv"].shape)]
    args += [consts["gk"], consts["bdk"]]
    specs += [_const_spec(consts["gk"].shape), _const_spec(consts["bdk"].shape)]
    if rope:
        args += [consts["cos"], consts["sin"]]
        specs += [pl.BlockSpec((tm, LANES), lambda b, t: (t, 0))] * 2
    widths = ([KV_DUP_COLS, V_TILE_COLS] if kv_only else
              [W_BRANCH, W_BRANCH, 3 * W_BRANCH, W_BRANCH, Q_COLS, KV_DUP_COLS, V_TILE_COLS, W_BRANCH])
    return pl.pallas_call(
        functools.partial(_inproj_kernel, rope=rope, kv_only=kv_only),
        grid=(B, T // tm),
        in_specs=specs,
        out_specs=[pl.BlockSpec((1, tm, n), row) for n in widths],
        out_shape=[jax.ShapeDtypeStruct((B, T, n), BF16) for n in widths],
        compiler_params=_params(2),
        name=f"inproj_{T}" + ("_kv" if kv_only else ""),
    )(*args)


def _attn_kernel(*refs, n_src, shifted):
    q_ref, sz_ref = refs[0], refs[1]
    kv = refs[2:2 + 2 * n_src]
    o_ref = refs[-1]
    tq = q_ref.shape[1]
    low = lax.broadcasted_iota(jnp.int32, (tq, LANES), 1) < HEAD_DIM
    nt = (((1,), (1,)), ((), ()))
    for j in range(2):
        cols = slice(j * LANES, (j + 1) * LANES)
        qp = q_ref[0, :, cols].astype(F32)
        halves = []
        for half, sel in enumerate((low, jnp.logical_not(low))):
            qm = jnp.where(sel, qp, 0.0).astype(BF16)
            vt = slice(half * LANES, (half + 1) * LANES)
            if shifted:
                s = [lax.dot_general(qm, kv[2 * i][0], nt, preferred_element_type=F32) for i in range(n_src)]
                m = functools.reduce(jnp.maximum, [jnp.max(v, axis=-1, keepdims=True) for v in s])
                terms = [_dot(jnp.exp2(s[i] - m).astype(BF16), kv[2 * i + 1][0, :, vt]) for i in range(n_src)]
            else:
                terms = []
                for i in range(n_src):
                    n_keys = kv[2 * i].shape[1]
                    for c0 in range(0, n_keys, KEY_CHUNK):
                        rows = slice(c0, min(c0 + KEY_CHUNK, n_keys))
                        s = lax.dot_general(qm, kv[2 * i][0, rows, :], nt, preferred_element_type=F32)
                        terms.append(_dot(jnp.exp2(s).astype(BF16), kv[2 * i + 1][0, rows, vt]))
            halves.append(functools.reduce(jnp.add, terms))
        num = jnp.where(low, halves[0], halves[1])
        den = pltpu.roll(jnp.where(low, halves[1], halves[0]), HEAD_DIM, 1)
        o_ref[0, :, cols] = (num / den * sz_ref[0, :, cols].astype(F32)).astype(BF16)


def _attention(q, sz, sources, shift_free):
    B, T, _ = q.shape
    gw = Q_COLS // N_KV_HEADS
    args = [q, sz] + [a for src in sources for a in src]

    def call(shifted):
        tq = min(T, 256 if shifted else 1024)
        qspec = pl.BlockSpec((1, tq, gw), lambda b, g, t: (b, t, g))
        specs = [qspec, qspec]
        for kk, vv in sources:
            specs += [pl.BlockSpec((1, kk.shape[1], LANES), lambda b, g, t: (b, 0, g)),
                      pl.BlockSpec((1, vv.shape[1], 2 * LANES), lambda b, g, t: (b, 0, g))]
        return pl.pallas_call(
            functools.partial(_attn_kernel, n_src=len(sources), shifted=shifted),
            grid=(B, N_KV_HEADS, T // tq),
            in_specs=specs,
            out_specs=qspec,
            out_shape=jax.ShapeDtypeStruct((B, T, Q_COLS), BF16),
            compiler_params=_params(3),
            name=f"attention_{T}" + ("_shifted" if shifted else ""),
        )
    return lax.cond(shift_free, call(False), call(True), *args)


def _deinterleave(x, scr):
    n, c = x.shape
    slabs = range(c // LANES)
    for j in slabs:
        scr[j][...] = x[:, j * LANES:(j + 1) * LANES]
    pick = lambda s: jnp.concatenate([scr[j][pl.ds(s, n // 2, stride=2), :] for j in slabs], axis=1)
    return pick(0), pick(1)


def _interleave(e, o, scr):
    m, c = e.shape
    slabs = range(c // LANES)
    for j in slabs:
        scr[j][pl.ds(0, m, stride=2), :] = e[:, j * LANES:(j + 1) * LANES]
        scr[j][pl.ds(1, m, stride=2), :] = o[:, j * LANES:(j + 1) * LANES]
    return jnp.concatenate([scr[j][...] for j in slabs], axis=1)


def _conv3_split(e, o, w):
    m = e.shape[0]
    row = lax.broadcasted_iota(jnp.int32, e.shape, 0)
    o_prev = jnp.where(row == 0, 0.0, pltpu.roll(o, 1, 0))
    e_next = jnp.where(row == m - 1, 0.0, pltpu.roll(e, m - 1, 0))
    return (o_prev * w[0:1] + e * w[1:2] + o * w[2:3],
            e * w[0:1] + o * w[1:2] + e_next * w[2:3])


def _long_conv_split(e, o, k, twr, twi, cm_ref, sm_ref, icm_ref, ism_ref):
    c = e.shape[1]
    pr, pi, qr, qi = _half_dft(e, o, twr, twi, cm_ref, sm_ref)
    ypr, ypi = pr * k[0] - pi * k[1], pr * k[1] + pi * k[0]
    yqr, yqi = qr * k[2] - qi * k[3], qr * k[3] + qi * k[2]
    dr, di = ypr - yqr, ypi - yqi
    re = jnp.concatenate([ypr + yqr, twr * dr + twi * di], axis=1).astype(BF16)
    im = jnp.concatenate([ypi + yqi, twr * di - twi * dr], axis=1).astype(BF16)
    y = _dot(icm_ref[...], re) + _dot(ism_ref[...], im)
    return y[:, :c], y[:, c:]


def _hyena_kernel(hv_ref, hx1_ref, hx2_ref, sg_ref, wv_ref, wx1_ref, wx2_ref, bias_ref,
                  cm_ref, sm_ref, icm_ref, ism_ref, twr_ref, twi_ref, k_ref, o_ref, *scratch):
    c = o_ref.shape[2]
    ns = c // LANES
    s_v, s_x1, s_x2, s_g, s_o = [scratch[i * ns:(i + 1) * ns] for i in range(5)]
    twr, twi = _lane_tile(twr_ref, c), _lane_tile(twi_ref, c)
    tabs = (cm_ref, sm_ref, icm_ref, ism_ref)
    split = lambda ref, scr: _deinterleave(ref[0].astype(F32), scr)
    ve, vo = _conv3_split(*split(hv_ref, s_v), wv_ref[...])
    x1e, x1o = _conv3_split(*split(hx1_ref, s_x1), wx1_ref[...])
    x2e, x2o = _conv3_split(*split(hx2_ref, s_x2), wx2_ref[...])
    ge, go = split(sg_ref, s_g)
    b0, b1 = bias_ref[0:1], bias_ref[1:2]
    ye, yo = _long_conv_split(ve, vo, [k_ref[0, 0, j] for j in range(4)], twr, twi, *tabs)
    z1e, z1o = x1e * (ye + ve * b0), x1o * (yo + vo * b0)
    ye, yo = _long_conv_split(z1e, z1o, [k_ref[0, 1, j] for j in range(4)], twr, twi, *tabs)
    z2e, z2o = x2e * (ye + z1e * b1), x2o * (yo + z1o * b1)
    o_ref[0] = _interleave(z2e * ge, z2o * go, s_o).astype(BF16)


def _hyena(hpre, sgh, conv_h, bias, tabs, spectra, layer):
    B, L, _ = hpre.shape
    m_half = L // 2
    nc = W_BRANCH // HALF
    blk = lambda j: pl.BlockSpec((1, L, HALF), lambda c, b: (b, 0, j * nc + c))
    cw = lambda j: pl.BlockSpec((3, HALF), lambda c, b: (0, j * nc + c))
    tab = _const_spec((m_half, m_half))
    tw = _const_spec((m_half, LANES))
    kspec = pl.BlockSpec((1, HYENA_ORDER, 4, m_half, HALF), lambda c, b: (layer, 0, 0, 0, c),
                         pipeline_mode=pl.Buffered(1))
    return pl.pallas_call(
        _hyena_kernel,
        grid=(nc, B),
        in_specs=[blk(0), blk(1), blk(2), blk(0), cw(0), cw(1), cw(2),
                  pl.BlockSpec((HYENA_ORDER, HALF), lambda c, b: (0, c)),
                  tab, tab, tab, tab, tw, tw, kspec],
        out_specs=blk(0),
        out_shape=jax.ShapeDtypeStruct((B, L, W_BRANCH), BF16),
        scratch_shapes=[pltpu.VMEM((L, LANES), F32)] * (5 * HALF // LANES),
        compiler_params=_params(2),
        name=f"hyena_{L}",
    )(hpre, hpre, hpre, sgh, conv_h, conv_h, conv_h, bias,
      tabs["cm"], tabs["sm"], tabs["icm"], tabs["ism"], tabs["twr"], tabs["twi"], spectra)


def _merge_kernel(x_ref, sc_ref, sh_ref, gt_ref, g_ref, ua_ref, up_ref, un_ref, ga_ref, ca_ref,
                  yb_ref, yc_ref, wg_ref, wb_ref, wo_ref, o_ref):
    t, nt = pl.program_id(1), pl.num_programs(1)
    x = x_ref[0]
    h = _rms_mod(x, g_ref[...], sc_ref[0], sh_ref[0]).astype(BF16)
    halo = up_ref.shape[1]
    first = jnp.where(t == 0, 0.0, up_ref[0, halo - 1:halo, :].astype(F32))
    last = jnp.where(t == nt - 1, 0.0, un_ref[0, 0:1, :].astype(F32))
    ya = ga_ref[0].astype(F32) * _conv3(ua_ref[0].astype(F32), ca_ref[...], first, last)
    ys = (ya.astype(BF16), yb_ref[0], yc_ref[0])
    merged = None
    for n in range(N_BRANCH):
        gate = jax.nn.sigmoid(_dot(h, wg_ref[:, n * D_MODEL:(n + 1) * D_MODEL]))
        term = gate * _dot(ys[n], wb_ref[n])
        merged = term if merged is None else merged + term
    out = _dot(merged.astype(BF16), wo_ref[...])
    o_ref[0] = x + gt_ref[0] * out


def _merge(x, sc, sh, gt, g, ua, ga, conv_a, yb, yc, wts):
    B, T, D = x.shape
    tm = min(T, 512)
    halo = 16
    r = tm // halo
    row = lambda b, t: (b, t, 0)
    vec = pl.BlockSpec((1, 1, D), lambda b, t: (b, 0, 0))
    br = pl.BlockSpec((1, tm, W_BRANCH), row)
    prev = pl.BlockSpec((1, halo, W_BRANCH), lambda b, t: (b, jnp.maximum(t * r - 1, 0), 0))
    nxt = pl.BlockSpec((1, halo, W_BRANCH), lambda b, t: (b, jnp.minimum((t + 1) * r, T // halo - 1), 0))
    return pl.pallas_call(
        _merge_kernel,
        grid=(B, T // tm),
        in_specs=[pl.BlockSpec((1, tm, D), row), vec, vec, vec, _const_spec((1, D)),
                  br, prev, nxt, br, _const_spec((3, W_BRANCH)), br, br,
                  _const_spec(wts["wg"].shape), _const_spec(wts["wb"].shape), _const_spec(wts["wo"].shape)],
        out_specs=pl.BlockSpec((1, tm, D), row),
        out_shape=jax.ShapeDtypeStruct((B, T, D), F32),
        compiler_params=_params(2),
        name=f"merge_{T}",
    )(x, sc, sh, gt, g[None, :], ua, ua, ua, ga, conv_a, yb, yc, wts["wg"], wts["wb"], wts["wo"])


def _rope_tables(S):
    pos = jnp.arange(S, dtype=jnp.int32)
    coord = jnp.stack([(pos // GRID_W).astype(F32), (pos % GRID_W).astype(F32)], axis=1)
    inv = ROPE_BASE ** (-jnp.arange(ROPE_FREQS, dtype=F32) / ROPE_FREQS)
    ang = coord[:, :, None] * inv
    cos = jnp.repeat(jnp.cos(ang)[:, :, None, :], 2, axis=2).reshape(S, HEAD_DIM)
    sin = jnp.sin(ang)
    sin = jnp.stack([-sin, sin], axis=2).reshape(S, HEAD_DIM)
    return jnp.tile(cos, (1, 2)), jnp.tile(sin, (1, 2))


def _dup_heads(w):
    d = w.shape[0]
    w = w.reshape(d, N_KV_HEADS, 1, HEAD_DIM)
    return jnp.broadcast_to(w, (d, N_KV_HEADS, 2, HEAD_DIM)).reshape(d, KV_DUP_COLS)


def _layer_weights(w_in, w_branch, w_out):
    wk = _dup_heads(w_in[:, K_OFF:V_OFF])
    wv = _dup_heads(w_in[:, V_OFF:Z_OFF])
    return dict(
        wa=w_in[:, :A_COLS].astype(BF16),
        wh=w_in[:, H_OFF:C_OFF].astype(BF16),
        wq=jnp.concatenate([w_in[:, C_OFF:K_OFF], wk, wv, w_in[:, Z_OFF:G_OFF]], axis=1).astype(BF16),
        wkv=jnp.concatenate([wk, wv], axis=1).astype(BF16),
        wg=w_in[:, G_OFF:].astype(BF16),
        wb=w_branch.astype(BF16),
        wo=w_out.astype(BF16),
    )


def _mixers(x, sc, sh, gt, g, wts, consts, conv_a, conv_h, bias, tabs, spectra, layer, rope, ctx_kv):
    ua, ga, hpre, sgh, q, kk, vv, sz = _inproj(x, sc, sh, g, wts, consts, rope=rope, kv_only=False)
    sources = ([ctx_kv] if ctx_kv is not None else []) + [(kk, vv)]
    yc = _attention(q, sz, sources, consts["shift_free"])
    yb = _hyena(hpre, sgh, conv_h, bias, tabs, spectra, layer)
    return _merge(x, sc, sh, gt, g, ua, ga, conv_a, yb, yc, wts), (kk, vv)


def kernel(x, c, ctx, c_ctx, norm_g, w_mod, b_mod, w_in, conv_a, conv_h, filt_w1, filt_b1, filt_w2,
           filt_b2, filt_w3, filt_b3, filt_freq, hyena_bias, q_norm_g, k_norm_g, w_branch, w_out):
    B, S, D = x.shape
    Lc = ctx.shape[1]
    cc = jnp.concatenate([c, c_ctx[None, :], jnp.zeros((MOD_ROWS - B - 1, D), F32)], axis=0)
    mod = _modulation(cc, w_mod, b_mod)

    tabs_s, tabs_c = _dft_tables(S), _dft_tables(Lc)
    filt = (filt_w1, filt_b1, filt_w2, filt_b2, filt_w3, filt_b3, filt_freq)
    spec_s = _filter_spectra(S, tabs_s, *filt)
    spec_c = _filter_spectra(Lc, tabs_c, *filt)

    cos, sin = _rope_tables(S)
    ones = jnp.ones((HEAD_DIM, HEAD_DIM), F32)
    bdq = jnp.kron(jnp.eye(N_HEADS, dtype=F32), ones).astype(BF16)
    bdk = jnp.kron(jnp.eye(2 * N_KV_HEADS, dtype=F32), ones).astype(BF16)

    for i in range(DEPTH):
        last = i == DEPTH - 1
        wts = _layer_weights(w_in[i], w_branch[i], w_out[i])
        consts = dict(gq=jnp.tile(q_norm_g[i], N_HEADS)[None, :],
                      gk=jnp.tile(k_norm_g[i], 2 * N_KV_HEADS)[None, :],
                      bdq=bdq, bdk=bdk, cos=cos, sin=sin)
        score_bound = (HEAD_DIM * Q_SCALE) * jnp.max(jnp.abs(q_norm_g[i])) * jnp.max(jnp.abs(k_norm_g[i]))
        consts["shift_free"] = score_bound <= SAFE_LOG2_RANGE
        split = lambda rows: [rows[:, None, j * D:(j + 1) * D] for j in range(3)]
        sh, sc, gt = split(mod[i, :B])
        sh_c, sc_c, gt_c = [jnp.broadcast_to(v, (B, 1, D)) for v in split(mod[i, B:B + 1])]
        if last:
            ctx_kv = tuple(_inproj(ctx, sc_c, sh_c, norm_g[i], wts, consts, rope=False, kv_only=True))
        else:
            ctx_next, ctx_kv = _mixers(ctx, sc_c, sh_c, gt_c, norm_g[i], wts, consts, conv_a[i], conv_h[i],
                                       hyena_bias[i], tabs_c, spec_c, i, False, None)
        x, _ = _mixers(x, sc, sh, gt, norm_g[i], wts, consts, conv_a[i], conv_h[i],
                       hyena_bias[i], tabs_s, spec_s, i, True, ctx_kv)
        if not last:
            ctx = ctx_next
    return x
```

```python
import functools
import math

import jax
import jax.numpy as jnp
from jax import lax
from jax.experimental import pallas as pl
from jax.experimental.pallas import tpu as pltpu

F32 = jnp.float32
BF16 = jnp.bfloat16

D_MODEL = 1024
DEPTH = 4
GRID_W = 64
W_BRANCH = 512
N_BRANCH = 3
N_HEADS = 8
N_KV_HEADS = 2
HEAD_DIM = 64
ROPE_FREQS = HEAD_DIM // 4
ROPE_BASE = 10000.0
ATTN_SCALE = HEAD_DIM ** -0.5
HYENA_ORDER = 2
FILTER_EMB = 33
FILTER_BANDS = (FILTER_EMB - 1) // 2
FILTER_HIDDEN = 64
HYENA_TARGET = 1e-2
MIN_DECAY = math.log(HYENA_TARGET) / 1.5
MAX_DECAY = math.log(HYENA_TARGET) / 0.3
EPS = 1e-6

A_COLS = 4 * W_BRANCH
H_OFF = A_COLS
C_OFF = H_OFF + 4 * W_BRANCH
Q_COLS = N_HEADS * HEAD_DIM
KV_COLS = N_KV_HEADS * HEAD_DIM
K_OFF = C_OFF + Q_COLS
V_OFF = K_OFF + KV_COLS
Z_OFF = V_OFF + KV_COLS
G_OFF = Z_OFF + W_BRANCH

LANES = 128
VMEM_LIMIT_BYTES = 58 * 1024 * 1024
KV_DUP_COLS = 2 * KV_COLS
V_TILE_COLS = 2 * KV_DUP_COLS
Q_SCALE = ATTN_SCALE * math.log2(math.e)
SAFE_LOG2_RANGE = 64.0
KEY_CHUNK = 512
MOD_ROWS = 24
FILT_PAD = 64
HALF = 256


def _params(n_axes):
    return pltpu.CompilerParams(dimension_semantics=("arbitrary",) * n_axes,
                                vmem_limit_bytes=VMEM_LIMIT_BYTES)


def _const_spec(shape):
    nd = len(shape)
    return pl.BlockSpec(shape, lambda *_: (0,) * nd, pipeline_mode=pl.Buffered(1))


def _silu(v):
    return v * jax.nn.sigmoid(v)


def _dot(a, b):
    return jnp.dot(a, b, preferred_element_type=F32)


def _split(a):
    hi = a.astype(BF16)
    lo = (a - hi.astype(F32)).astype(BF16)
    return hi, lo


def _dot3(a, b):
    ah, al = _split(a)
    bh, bl = _split(b)
    return _dot(ah, bh) + (_dot(ah, bl) + _dot(al, bh))


def _rms_mod(x, g, sc, sh):
    y = x * lax.rsqrt(jnp.mean(x * x, axis=-1, keepdims=True) + EPS)
    return (y * g) * (1.0 + sc) + sh


def _shift_rows(u, first_row, last_row):
    n = u.shape[0]
    row = lax.broadcasted_iota(jnp.int32, u.shape, 0)
    prev = jnp.where(row == 0, first_row, pltpu.roll(u, 1, 0))
    nxt = jnp.where(row == n - 1, last_row, pltpu.roll(u, n - 1, 0))
    return prev, nxt


def _conv3(u, w, first_row=0.0, last_row=0.0):
    prev, nxt = _shift_rows(u, first_row, last_row)
    return prev * w[0:1] + u * w[1:2] + nxt * w[2:3]


def _mod_kernel(cc_ref, w_ref, b_ref, o_ref):
    o_ref[0] = _dot3(_silu(cc_ref[...]), w_ref[0]) + b_ref[0]


def _modulation(cc, w_mod, b_mod):
    return pl.pallas_call(
        _mod_kernel,
        grid=(DEPTH, 3),
        in_specs=[pl.BlockSpec((MOD_ROWS, D_MODEL), lambda i, j: (0, 0)),
                  pl.BlockSpec((1, D_MODEL, D_MODEL), lambda i, j: (i, 0, j)),
                  pl.BlockSpec((1, 1, D_MODEL), lambda i, j: (i, 0, j))],
        out_specs=pl.BlockSpec((1, MOD_ROWS, D_MODEL), lambda i, j: (i, 0, j)),
        out_shape=jax.ShapeDtypeStruct((DEPTH, MOD_ROWS, 3 * D_MODEL), F32),
        compiler_params=_params(2),
        name="modulation",
    )(cc, w_mod, b_mod.reshape(DEPTH, 1, 3 * D_MODEL))


def _dft_tables(L):
    n, m_half = 2 * L, L // 2
    idx = jnp.arange(m_half, dtype=jnp.int32)
    q = ((2 * idx[:, None] + 1) * idx[None, :]) % n
    ang = q.astype(F32) * (2.0 * math.pi / n)
    cm, sm = jnp.cos(ang).astype(BF16), (-jnp.sin(ang)).astype(BF16)
    tw = (2 * idx + 1).astype(F32) * (math.pi / n)
    lanes = lambda v: jnp.broadcast_to(v[:, None], (m_half, LANES))
    return dict(cm=cm, sm=sm, icm=cm.T * (1.0 / L), ism=sm.T * (1.0 / L),
                twr=lanes(jnp.cos(tw)), twi=lanes(-jnp.sin(tw)))


def _lane_tile(ref, width):
    return jnp.concatenate([ref[...]] * (width // LANES), axis=1)


def _half_dft(e, o, twr, twi, cm_ref, sm_ref):
    c = e.shape[1]
    cat = jnp.concatenate([e, o], axis=1).astype(BF16)
    re, im = _dot(cm_ref[...], cat), _dot(sm_ref[...], cat)
    er, orr, ei, oi = re[:, :c], re[:, c:], im[:, :c], im[:, c:]
    tr = twr * orr - twi * oi
    ti = twr * oi + twi * orr
    return er + tr, ei + ti, er - tr, ei - ti


def _filter_hidden_kernel(feats_ref, w1_ref, b1_ref, w2_ref, b2_ref, fq_ref, h_ref):
    fq = fq_ref[0]
    h = jnp.sin(fq[0:1] * (_dot3(feats_ref[...], w1_ref[0]) + b1_ref[0]))
    h_ref[0] = jnp.sin(fq[1:2] * (_dot3(h, w2_ref[0]) + b2_ref[0]))


def _filter_kernel(h_ref, w3f_ref, w3b_ref, b3f_ref, b3b_ref, dl_ref, cm_ref, sm_ref, twr_ref, twi_ref,
                   k_ref, *, L):
    m_half = L // 2
    c = dl_ref.shape[1]
    h = h_ref[0]
    row = lax.broadcasted_iota(jnp.int32, (L, c), 0)
    lag = jnp.where(row < m_half, 2 * row, 2 * (row - m_half) + 1)
    win = jnp.exp(-(lag.astype(F32) * (1.0 / (L - 1))) * dl_ref[...])
    hf = (_dot3(h, w3f_ref[0]) + b3f_ref[0]) * win
    hb = jnp.where(lag == 0, 0.0, (_dot3(h, w3b_ref[0]) + b3b_ref[0]) * win)
    nrm = jnp.sum(jnp.abs(hf), axis=0, keepdims=True) + jnp.sum(jnp.abs(hb), axis=0, keepdims=True)
    hf, hb = hf / nrm, hb / nrm
    twr, twi = _lane_tile(twr_ref, 2 * c), _lane_tile(twi_ref, 2 * c)
    p = _half_dft(jnp.concatenate([hf[:m_half], hb[:m_half]], axis=1),
                  jnp.concatenate([hf[m_half:], hb[m_half:]], axis=1), twr, twi, cm_ref, sm_ref)
    sign = (1.0, -1.0, 1.0, -1.0)
    for j in range(4):
        k_ref[0, 0, j] = p[j][:, :c] + sign[j] * p[j][:, c:]


def _filter_spectra(L, tabs, filt_w1, filt_b1, filt_w2, filt_b2, filt_w3, filt_b3, filt_freq):
    m_half = L // 2
    t = jnp.linspace(0.0, 1.0, L, dtype=F32)[:, None]
    bands = jnp.linspace(1e-4, FILTER_BANDS - 1, FILTER_BANDS, dtype=F32)
    w = (2.0 * math.pi / L) * jnp.arange(L, dtype=F32)[:, None]
    feats = jnp.concatenate([t, jnp.cos(w * bands), jnp.sin(w * bands)], axis=-1)
    feats = jnp.concatenate([feats[0::2], feats[1::2]], axis=0)
    feats = jnp.pad(feats, ((0, 0), (0, FILT_PAD - FILTER_EMB)))
    w1 = jnp.pad(filt_w1, ((0, 0), (0, FILT_PAD - FILTER_EMB), (0, 0)))
    deltas = jnp.abs(jnp.linspace(MIN_DECAY, MAX_DECAY, W_BRANCH, dtype=F32))[None, :]
    lay1 = lambda i: (i, 0, 0)
    hidden = pl.pallas_call(
        _filter_hidden_kernel,
        grid=(DEPTH,),
        in_specs=[_const_spec((L, FILT_PAD)),
                  pl.BlockSpec((1, FILT_PAD, FILTER_HIDDEN), lay1),
                  pl.BlockSpec((1, 1, FILTER_HIDDEN), lay1),
                  pl.BlockSpec((1, FILTER_HIDDEN, FILTER_HIDDEN), lay1),
                  pl.BlockSpec((1, 1, FILTER_HIDDEN), lay1),
                  pl.BlockSpec((1, 2, FILTER_HIDDEN), lay1)],
        out_specs=pl.BlockSpec((1, L, FILTER_HIDDEN), lay1),
        out_shape=jax.ShapeDtypeStruct((DEPTH, L, FILTER_HIDDEN), F32),
        compiler_params=_params(1),
        name=f"filter_hidden_{L}",
    )(feats, w1, filt_b1[:, None], filt_w2, filt_b2[:, None], filt_freq)
    nc = W_BRANCH // HALF
    fwd = lambda i, o, c: (i, 0, o * 2 * nc + c)
    bwd = lambda i, o, c: (i, 0, o * 2 * nc + nc + c)
    tab = _const_spec((m_half, m_half))
    tw = _const_spec((m_half, LANES))
    return pl.pallas_call(
        functools.partial(_filter_kernel, L=L),
        grid=(DEPTH, HYENA_ORDER, nc),
        in_specs=[pl.BlockSpec((1, L, FILTER_HIDDEN), lambda i, o, c: (i, 0, 0)),
                  pl.BlockSpec((1, FILTER_HIDDEN, HALF), fwd),
                  pl.BlockSpec((1, FILTER_HIDDEN, HALF), bwd),
                  pl.BlockSpec((1, 1, HALF), fwd),
                  pl.BlockSpec((1, 1, HALF), bwd),
                  pl.BlockSpec((1, HALF), lambda i, o, c: (0, c)),
                  tab, tab, tw, tw],
        out_specs=pl.BlockSpec((1, 1, 4, m_half, HALF), lambda i, o, c: (i, o, 0, 0, c)),
        out_shape=jax.ShapeDtypeStruct((DEPTH, HYENA_ORDER, 4, m_half, W_BRANCH), F32),
        compiler_params=_params(3),
        name=f"filter_spectra_{L}",
    )(hidden, filt_w3, filt_w3, filt_b3[:, None], filt_b3[:, None], deltas,
      tabs["cm"], tabs["sm"], tabs["twr"], tabs["twi"])


def _head_norm(v, bd_ref, g):
    ss = _dot((v * v).astype(BF16), bd_ref[...])
    return v * lax.rsqrt(ss * (1.0 / HEAD_DIM) + EPS) * g


def _rope(v, cos, sin):
    n = v.shape[1] // LANES
    lane = lax.broadcasted_iota(jnp.int32, (v.shape[0], LANES), 1)
    first = (lane % 32) < 16
    out = []
    for i in range(n):
        c = v[:, i * LANES:(i + 1) * LANES]
        partner = jnp.where(first, pltpu.roll(c, LANES - 16, 1), pltpu.roll(c, 16, 1))
        out.append(c * cos + partner * sin)
    return jnp.concatenate(out, axis=1)


def _kv_tiles(k, v):
    low = lax.broadcasted_iota(jnp.int32, k.shape, 1) < HEAD_DIM
    ks, vs = pltpu.roll(k, HEAD_DIM, 1), pltpu.roll(v, HEAD_DIM, 1)
    kk = jnp.concatenate([jnp.where(low, k, ks), jnp.where(low, ks, k)], axis=1)
    vv = jnp.concatenate([jnp.where(low, v, 1.0), jnp.where(low, 1.0, vs),
                          jnp.where(low, vs, 1.0), jnp.where(low, 1.0, v)], axis=1)
    return kk.astype(BF16), vv.astype(BF16)


def _inproj_kernel(*refs, rope, kv_only):
    it = iter(refs)
    x_ref, sc_ref, sh_ref, g_ref = next(it), next(it), next(it), next(it)
    if not kv_only:
        wa_ref, wh_ref = next(it), next(it)
    wq_ref = next(it)
    if not kv_only:
        gq_ref, bdq_ref = next(it), next(it)
    gk_ref, bdk_ref = next(it), next(it)
    if rope:
        cos_ref, sin_ref = next(it), next(it)
    outs = list(it)

    h = _rms_mod(x_ref[0], g_ref[...], sc_ref[0], sh_ref[0]).astype(BF16)
    w = W_BRANCH
    if kv_only:
        kk_ref, vv_ref = outs
        acc = _dot(h, wq_ref[...])
        k = _head_norm(acc[:, :KV_COLS], bdk_ref, gk_ref[...])
        kk_ref[0], vv_ref[0] = _kv_tiles(k, acc[:, KV_COLS:])
        return

    ua_ref, ga_ref, hp_ref, sgh_ref, q_ref, kk_ref, vv_ref, sz_ref = outs
    acc = _dot(h, wq_ref[...])
    q = _head_norm(acc[:, :Q_COLS], bdq_ref, gq_ref[...])
    k = _head_norm(acc[:, Q_COLS:Q_COLS + KV_COLS], bdk_ref, gk_ref[...])
    if rope:
        q = _rope(q, cos_ref[...], sin_ref[...])
        k = _rope(k, cos_ref[...], sin_ref[...])
    q_ref[0] = (q * Q_SCALE).astype(BF16)
    kk_ref[0], vv_ref[0] = _kv_tiles(k, acc[:, Q_COLS + KV_COLS:Q_COLS + 2 * KV_COLS])
    sz_ref[0] = _silu(acc[:, Q_COLS + 2 * KV_COLS:]).astype(BF16)
    acc = _dot(h, wa_ref[...])
    ua_ref[0] = (acc[:, 2 * w:3 * w] * acc[:, :w]).astype(BF16)
    ga_ref[0] = (acc[:, w:2 * w] * _silu(acc[:, 3 * w:])).astype(BF16)
    acc = _dot(h, wh_ref[...])
    hp_ref[0] = acc[:, :3 * w].astype(BF16)
    sgh_ref[0] = _silu(acc[:, 3 * w:]).astype(BF16)


def _inproj(x, sc, sh, g, wts, consts, *, rope, kv_only):
    B, T, D = x.shape
    tm = min(T, 512)
    row = lambda b, t: (b, t, 0)
    vec = pl.BlockSpec((1, 1, D), lambda b, t: (b, 0, 0))
    args = [x, sc, sh, g[None, :]]
    specs = [pl.BlockSpec((1, tm, D), row), vec, vec, _const_spec((1, D))]
    if not kv_only:
        args += [wts["wa"], wts["wh"], wts["wq"], consts["gq"], consts["bdq"]]
        specs += [_const_spec(wts["wa"].shape), _const_spec(wts["wh"].shape),
                  _const_spec(wts["wq"].shape), _const_spec(consts["gq"].shape),
                  _const_spec(consts["bdq"].shape)]
    else:
        args += [wts["wkv"]]
        specs += [_const_spec(wts["wkv"].shape)]
    args += [consts["gk"], consts["bdk"]]
    specs += [_const_spec(consts["gk"].shape), _const_spec(consts["bdk"].shape)]
    if rope:
        args += [consts["cos"], consts["sin"]]
        specs += [pl.BlockSpec((tm, LANES), lambda b, t: (t, 0))] * 2
    widths = ([KV_DUP_COLS, V_TILE_COLS] if kv_only else
              [W_BRANCH, W_BRANCH, 3 * W_BRANCH, W_BRANCH, Q_COLS, KV_DUP_COLS, V_TILE_COLS, W_BRANCH])
    return pl.pallas_call(
        functools.partial(_inproj_kernel, rope=rope, kv_only=kv_only),
        grid=(B, T // tm),
        in_specs=specs,
        out_specs=[pl.BlockSpec((1, tm, n), row) for n in widths],
        out_shape=[jax.ShapeDtypeStruct((B, T, n), BF16) for n in widths],
        compiler_params=_params(2),
        name=f"inproj_{T}" + ("_kv" if kv_only else ""),
    )(*args)


def _attn_kernel(*refs, n_src, shifted):
    q_ref, sz_ref = refs[0], refs[1]
    kv = refs[2:2 + 2 * n_src]
    o_ref = refs[-1]
    tq = q_ref.shape[1]
    low = lax.broadcasted_iota(jnp.int32, (tq, LANES), 1) < HEAD_DIM
    nt = (((1,), (1,)), ((), ()))
    for j in range(2):
        cols = slice(j * LANES, (j + 1) * LANES)
        qp = q_ref[0, :, cols].astype(F32)
        halves = []
        for half, sel in enumerate((low, jnp.logical_not(low))):
            qm = jnp.where(sel, qp, 0.0).astype(BF16)
            vt = slice(half * LANES, (half + 1) * LANES)
            if shifted:
                s = [lax.dot_general(qm, kv[2 * i][0], nt, preferred_element_type=F32) for i in range(n_src)]
                m = functools.reduce(jnp.maximum, [jnp.max(v, axis=-1, keepdims=True) for v in s])
                terms = [_dot(jnp.exp2(s[i] - m).astype(BF16), kv[2 * i + 1][0, :, vt]) for i in range(n_src)]
            else:
                terms = []
                for i in range(n_src):
                    n_keys = kv[2 * i].shape[1]
                    for c0 in range(0, n_keys, KEY_CHUNK):
                        rows = slice(c0, min(c0 + KEY_CHUNK, n_keys))
                        s = lax.dot_general(qm, kv[2 * i][0, rows, :], nt, preferred_element_type=F32)
                        terms.append(_dot(jnp.exp2(s).astype(BF16), kv[2 * i + 1][0, rows, vt]))
            halves.append(functools.reduce(jnp.add, terms))
        num = jnp.where(low, halves[0], halves[1])
        den = pltpu.roll(jnp.where(low, halves[1], halves[0]), HEAD_DIM, 1)
        o_ref[0, :, cols] = (num / den * sz_ref[0, :, cols].astype(F32)).astype(BF16)


def _attention(q, sz, sources, shift_free):
    B, T, _ = q.shape
    gw = Q_COLS // N_KV_HEADS
    args = [q, sz] + [a for src in sources for a in src]

    def call(shifted):
        tq = min(T, 256 if shifted else 1024)
        qspec = pl.BlockSpec((1, tq, gw), lambda b, g, t: (b, t, g))
        specs = [qspec, qspec]
        for kk, vv in sources:
            specs += [pl.BlockSpec((1, kk.shape[1], LANES), lambda b, g, t: (b, 0, g)),
                      pl.BlockSpec((1, vv.shape[1], 2 * LANES), lambda b, g, t: (b, 0, g))]
        return pl.pallas_call(
            functools.partial(_attn_kernel, n_src=len(sources), shifted=shifted),
            grid=(B, N_KV_HEADS, T // tq),
            in_specs=specs,
            out_specs=qspec,
            out_shape=jax.ShapeDtypeStruct((B, T, Q_COLS), BF16),
            compiler_params=_params(3),
            name=f"attention_{T}" + ("_shifted" if shifted else ""),
        )
    return lax.cond(shift_free, call(False), call(True), *args)


def _deinterleave(x, scr):
    n, c = x.shape
    slabs = range(c // LANES)
    for j in slabs:
        scr[j][...] = x[:, j * LANES:(j + 1) * LANES]
    pick = lambda s: jnp.concatenate([scr[j][pl.ds(s, n // 2, stride=2), :] for j in slabs], axis=1)
    return pick(0), pick(1)


def _interleave(e, o, scr):
    m, c = e.shape
    slabs = range(c // LANES)
    for j in slabs:
        scr[j][pl.ds(0, m, stride=2), :] = e[:, j * LANES:(j + 1) * LANES]
        scr[j][pl.ds(1, m, stride=2), :] = o[:, j * LANES:(j + 1) * LANES]
    return jnp.concatenate([scr[j][...] for j in slabs], axis=1)


def _conv3_split(e, o, w):
    m = e.shape[0]
    row = lax.broadcasted_iota(jnp.int32, e.shape, 0)
    o_prev = jnp.where(row == 0, 0.0, pltpu.roll(o, 1, 0))
    e_next = jnp.where(row == m - 1, 0.0, pltpu.roll(e, m - 1, 0))
    return (o_prev * w[0:1] + e * w[1:2] + o * w[2:3],
            e * w[0:1] + o * w[1:2] + e_next * w[2:3])


def _long_conv_split(e, o, k, twr, twi, cm_ref, sm_ref, icm_ref, ism_ref):
    c = e.shape[1]
    pr, pi, qr, qi = _half_dft(e, o, twr, twi, cm_ref, sm_ref)
    ypr, ypi = pr * k[0] - pi * k[1], pr * k[1] + pi * k[0]
    yqr, yqi = qr * k[2] - qi * k[3], qr * k[3] + qi * k[2]
    dr, di = ypr - yqr, ypi - yqi
    re = jnp.concatenate([ypr + yqr, twr * dr + twi * di], axis=1).astype(BF16)
    im = jnp.concatenate([ypi + yqi, twr * di - twi * dr], axis=1).astype(BF16)
    y = _dot(icm_ref[...], re) + _dot(ism_ref[...], im)
    return y[:, :c], y[:, c:]


def _hyena_kernel(hv_ref, hx1_ref, hx2_ref, sg_ref, wv_ref, wx1_ref, wx2_ref, bias_ref,
                  cm_ref, sm_ref, icm_ref, ism_ref, twr_ref, twi_ref, k_ref, o_ref, *scratch):
    c = o_ref.shape[2]
    ns = c // LANES
    s_v, s_x1, s_x2, s_g, s_o = [scratch[i * ns:(i + 1) * ns] for i in range(5)]
    twr, twi = _lane_tile(twr_ref, c), _lane_tile(twi_ref, c)
    tabs = (cm_ref, sm_ref, icm_ref, ism_ref)
    split = lambda ref, scr: _deinterleave(ref[0].astype(F32), scr)
    ve, vo = _conv3_split(*split(hv_ref, s_v), wv_ref[...])
    x1e, x1o = _conv3_split(*split(hx1_ref, s_x1), wx1_ref[...])
    x2e, x2o = _conv3_split(*split(hx2_ref, s_x2), wx2_ref[...])
    ge, go = split(sg_ref, s_g)
    b0, b1 = bias_ref[0:1], bias_ref[1:2]
    ye, yo = _long_conv_split(ve, vo, [k_ref[0, 0, j] for j in range(4)], twr, twi, *tabs)
    z1e, z1o = x1e * (ye + ve * b0), x1o * (yo + vo * b0)
    ye, yo = _long_conv_split(z1e, z1o, [k_ref[0, 1, j] for j in range(4)], twr, twi, *tabs)
    z2e, z2o = x2e * (ye + z1e * b1), x2o * (yo + z1o * b1)
    o_ref[0] = _interleave(z2e * ge, z2o * go, s_o).astype(BF16)


def _hyena(hpre, sgh, conv_h, bias, tabs, spectra, layer):
    B, L, _ = hpre.shape
    m_half = L // 2
    nc = W_BRANCH // HALF
    blk = lambda j: pl.BlockSpec((1, L, HALF), lambda c, b: (b, 0, j * nc + c))
    cw = lambda j: pl.BlockSpec((3, HALF), lambda c, b: (0, j * nc + c))
    tab = _const_spec((m_half, m_half))
    tw = _const_spec((m_half, LANES))
    kspec = pl.BlockSpec((1, HYENA_ORDER, 4, m_half, HALF), lambda c, b: (layer, 0, 0, 0, c),
                         pipeline_mode=pl.Buffered(1))
    return pl.pallas_call(
        _hyena_kernel,
        grid=(nc, B),
        in_specs=[blk(0), blk(1), blk(2), blk(0), cw(0), cw(1), cw(2),
                  pl.BlockSpec((HYENA_ORDER, HALF), lambda c, b: (0, c)),
                  tab, tab, tab, tab, tw, tw, kspec],
        out_specs=blk(0),
        out_shape=jax.ShapeDtypeStruct((B, L, W_BRANCH), BF16),
        scratch_shapes=[pltpu.VMEM((L, LANES), F32)] * (5 * HALF // LANES),
        compiler_params=_params(2),
        name=f"hyena_{L}",
    )(hpre, hpre, hpre, sgh, conv_h, conv_h, conv_h, bias,
      tabs["cm"], tabs["sm"], tabs["icm"], tabs["ism"], tabs["twr"], tabs["twi"], spectra)


def _merge_kernel(x_ref, sc_ref, sh_ref, gt_ref, g_ref, ua_ref, up_ref, un_ref, ga_ref, ca_ref,
                  yb_ref, yc_ref, wg_ref, wb_ref, wo_ref, o_ref):
    t, nt = pl.program_id(1), pl.num_programs(1)
    x = x_ref[0]
    h = _rms_mod(x, g_ref[...], sc_ref[0], sh_ref[0]).astype(BF16)
    halo = up_ref.shape[1]
    first = jnp.where(t == 0, 0.0, up_ref[0, halo - 1:halo, :].astype(F32))
    last = jnp.where(t == nt - 1, 0.0, un_ref[0, 0:1, :].astype(F32))
    ya = ga_ref[0].astype(F32) * _conv3(ua_ref[0].astype(F32), ca_ref[...], first, last)
    ys = (ya.astype(BF16), yb_ref[0], yc_ref[0])
    merged = None
    for n in range(N_BRANCH):
        gate = jax.nn.sigmoid(_dot(h, wg_ref[:, n * D_MODEL:(n + 1) * D_MODEL]))
        term = gate * _dot(ys[n], wb_ref[n])
        merged = term if merged is None else merged + term
    out = _dot(merged.astype(BF16), wo_ref[...])
    o_ref[0] = x + gt_ref[0] * out


def _merge(x, sc, sh, gt, g, ua, ga, conv_a, yb, yc, wts):
    B, T, D = x.shape
    tm = min(T, 512)
    halo = 16
    r = tm // halo
    row = lambda b, t: (b, t, 0)
    vec = pl.BlockSpec((1, 1, D), lambda b, t: (b, 0, 0))
    br = pl.BlockSpec((1, tm, W_BRANCH), row)
    prev = pl.BlockSpec((1, halo, W_BRANCH), lambda b, t: (b, jnp.maximum(t * r - 1, 0), 0))
    nxt = pl.BlockSpec((1, halo, W_BRANCH), lambda b, t: (b, jnp.minimum((t + 1) * r, T // halo - 1), 0))
    return pl.pallas_call(
        _merge_kernel,
        grid=(B, T // tm),
        in_specs=[pl.BlockSpec((1, tm, D), row), vec, vec, vec, _const_spec((1, D)),
                  br, prev, nxt, br, _const_spec((3, W_BRANCH)), br, br,
                  _const_spec(wts["wg"].shape), _const_spec(wts["wb"].shape), _const_spec(wts["wo"].shape)],
        out_specs=pl.BlockSpec((1, tm, D), row),
        out_shape=jax.ShapeDtypeStruct((B, T, D), F32),
        compiler_params=_params(2),
        name=f"merge_{T}",
    )(x, sc, sh, gt, g[None, :], ua, ua, ua, ga, conv_a, yb, yc, wts["wg"], wts["wb"], wts["wo"])


def _rope_tables(S):
    pos = jnp.arange(S, dtype=jnp.int32)
    coord = jnp.stack([(pos // GRID_W).astype(F32), (pos % GRID_W).astype(F32)], axis=1)
    inv = ROPE_BASE ** (-jnp.arange(ROPE_FREQS, dtype=F32) / ROPE_FREQS)
    ang = coord[:, :, None] * inv
    cos = jnp.repeat(jnp.cos(ang)[:, :, None, :], 2, axis=2).reshape(S, HEAD_DIM)
    sin = jnp.sin(ang)
    sin = jnp.stack([-sin, sin], axis=2).reshape(S, HEAD_DIM)
    return jnp.tile(cos, (1, 2)), jnp.tile(sin, (1, 2))


def _layer_weights(w_in, w_branch, w_out):
    return dict(
        wa=w_in[:, :A_COLS].astype(BF16),
        wh=w_in[:, H_OFF:C_OFF].astype(BF16),
        wq=w_in[:, C_OFF:G_OFF].astype(BF16),
        wkv=w_in[:, K_OFF:Z_OFF].astype(BF16),
        wg=w_in[:, G_OFF:].astype(BF16),
        wb=w_branch.astype(BF16),
        wo=w_out.astype(BF16),
    )


def _mixers(x, sc, sh, gt, g, wts, consts, conv_a, conv_h, bias, tabs, spectra, layer, rope, ctx_kv):
    ua, ga, hpre, sgh, q, kk, vv, sz = _inproj(x, sc, sh, g, wts, consts, rope=rope, kv_only=False)
    sources = ([ctx_kv] if ctx_kv is not None else []) + [(kk, vv)]
    yc = _attention(q, sz, sources, consts["shift_free"])
    yb = _hyena(hpre, sgh, conv_h, bias, tabs, spectra, layer)
    return _merge(x, sc, sh, gt, g, ua, ga, conv_a, yb, yc, wts), (kk, vv)


def kernel(x, c, ctx, c_ctx, norm_g, w_mod, b_mod, w_in, conv_a, conv_h, filt_w1, filt_b1, filt_w2,
           filt_b2, filt_w3, filt_b3, filt_freq, hyena_bias, q_norm_g, k_norm_g, w_branch, w_out):
    B, S, D = x.shape
    Lc = ctx.shape[1]
    cc = jnp.concatenate([c, c_ctx[None, :], jnp.zeros((MOD_ROWS - B - 1, D), F32)], axis=0)
    mod = _modulation(cc, w_mod, b_mod)

    tabs_s, tabs_c = _dft_tables(S), _dft_tables(Lc)
    filt = (filt_w1, filt_b1, filt_w2, filt_b2, filt_w3, filt_b3, filt_freq)
    spec_s = _filter_spectra(S, tabs_s, *filt)
    spec_c = _filter_spectra(Lc, tabs_c, *filt)

    cos, sin = _rope_tables(S)
    ones = jnp.ones((HEAD_DIM, HEAD_DIM), F32)
    bdq = jnp.kron(jnp.eye(N_HEADS, dtype=F32), ones).astype(BF16)
    bdk = jnp.kron(jnp.eye(N_KV_HEADS, dtype=F32), ones).astype(BF16)

    for i in range(DEPTH):
        last = i == DEPTH - 1
        wts = _layer_weights(w_in[i], w_branch[i], w_out[i])
        consts = dict(gq=jnp.tile(q_norm_g[i], N_HEADS)[None, :],
                      gk=jnp.tile(k_norm_g[i], N_KV_HEADS)[None, :],
                      bdq=bdq, bdk=bdk, cos=cos, sin=sin)
        score_bound = (HEAD_DIM * Q_SCALE) * jnp.max(jnp.abs(q_norm_g[i])) * jnp.max(jnp.abs(k_norm_g[i]))
        consts["shift_free"] = score_bound <= SAFE_LOG2_RANGE
        split = lambda rows: [rows[:, None, j * D:(j + 1) * D] for j in range(3)]
        sh, sc, gt = split(mod[i, :B])
        sh_c, sc_c, gt_c = [jnp.broadcast_to(v, (B, 1, D)) for v in split(mod[i, B:B + 1])]
        if last:
            ctx_kv = tuple(_inproj(ctx, sc_c, sh_c, norm_g[i], wts, consts, rope=False, kv_only=True))
        else:
            ctx_next, ctx_kv = _mixers(ctx, sc_c, sh_c, gt_c, norm_g[i], wts, consts, conv_a[i], conv_h[i],
                                       hyena_bias[i], tabs_c, spec_c, i, False, None)
        x, _ = _mixers(x, sc, sh, gt, norm_g[i], wts, consts, conv_a[i], conv_h[i],
                       hyena_bias[i], tabs_s, spec_s, i, True, ctx_kv)
        if not last:
            ctx = ctx_next
    return x
```

```python
import functools
import math

import jax
import jax.numpy as jnp
from jax import lax
from jax.experimental import pallas as pl
from jax.experimental.pallas import tpu as pltpu

F32 = jnp.float32
BF16 = jnp.bfloat16

D_MODEL = 1024
DEPTH = 4
GRID_W = 64
W_BRANCH = 512
N_BRANCH = 3
N_HEADS = 8
N_KV_HEADS = 2
HEAD_DIM = 64
ROPE_FREQS = HEAD_DIM // 4
ROPE_BASE = 10000.0
ATTN_SCALE = HEAD_DIM ** -0.5
HYENA_ORDER = 2
FILTER_EMB = 33
FILTER_BANDS = (FILTER_EMB - 1) // 2
FILTER_HIDDEN = 64
HYENA_TARGET = 1e-2
MIN_DECAY = math.log(HYENA_TARGET) / 1.5
MAX_DECAY = math.log(HYENA_TARGET) / 0.3
EPS = 1e-6

A_COLS = 4 * W_BRANCH
H_OFF = A_COLS
C_OFF = H_OFF + 4 * W_BRANCH
Q_COLS = N_HEADS * HEAD_DIM
KV_COLS = N_KV_HEADS * HEAD_DIM
K_OFF = C_OFF + Q_COLS
V_OFF = K_OFF + KV_COLS
Z_OFF = V_OFF + KV_COLS
G_OFF = Z_OFF + W_BRANCH

LANES = 128
VMEM_LIMIT_BYTES = 58 * 1024 * 1024
KV_DUP_COLS = 2 * KV_COLS
V_TILE_COLS = 2 * KV_DUP_COLS
Q_SCALE = ATTN_SCALE * math.log2(math.e)
SAFE_LOG2_RANGE = 64.0
KEY_CHUNK = 512
MOD_ROWS = 24
FILT_PAD = 64
HALF = 256
RADIX = 4


def _params(n_axes):
    return pltpu.CompilerParams(dimension_semantics=("arbitrary",) * n_axes,
                                vmem_limit_bytes=VMEM_LIMIT_BYTES)


def _const_spec(shape):
    nd = len(shape)
    return pl.BlockSpec(shape, lambda *_: (0,) * nd, pipeline_mode=pl.Buffered(1))


def _silu(v):
    return v * jax.nn.sigmoid(v)


def _dot(a, b):
    return jnp.dot(a, b, preferred_element_type=F32)


def _split(a):
    hi = a.astype(BF16)
    lo = (a - hi.astype(F32)).astype(BF16)
    return hi, lo


def _dot3(a, b):
    ah, al = _split(a)
    bh, bl = _split(b)
    return _dot(ah, bh) + (_dot(ah, bl) + _dot(al, bh))


def _rms_mod(x, g, sc, sh):
    y = x * lax.rsqrt(jnp.mean(x * x, axis=-1, keepdims=True) + EPS)
    return (y * g) * (1.0 + sc) + sh


def _shift_rows(u, first_row, last_row):
    n = u.shape[0]
    row = lax.broadcasted_iota(jnp.int32, u.shape, 0)
    prev = jnp.where(row == 0, first_row, pltpu.roll(u, 1, 0))
    nxt = jnp.where(row == n - 1, last_row, pltpu.roll(u, n - 1, 0))
    return prev, nxt


def _conv3(u, w, first_row=0.0, last_row=0.0):
    prev, nxt = _shift_rows(u, first_row, last_row)
    return prev * w[0:1] + u * w[1:2] + nxt * w[2:3]


def _mod_kernel(cc_ref, w_ref, b_ref, o_ref):
    o_ref[0] = _dot3(_silu(cc_ref[...]), w_ref[0]) + b_ref[0]


def _modulation(cc, w_mod, b_mod):
    return pl.pallas_call(
        _mod_kernel,
        grid=(DEPTH, 3),
        in_specs=[pl.BlockSpec((MOD_ROWS, D_MODEL), lambda i, j: (0, 0)),
                  pl.BlockSpec((1, D_MODEL, D_MODEL), lambda i, j: (i, 0, j)),
                  pl.BlockSpec((1, 1, D_MODEL), lambda i, j: (i, 0, j))],
        out_specs=pl.BlockSpec((1, MOD_ROWS, D_MODEL), lambda i, j: (i, 0, j)),
        out_shape=jax.ShapeDtypeStruct((DEPTH, MOD_ROWS, 3 * D_MODEL), F32),
        compiler_params=_params(2),
        name="modulation",
    )(cc, w_mod, b_mod.reshape(DEPTH, 1, 3 * D_MODEL))


def _dft_tables(L):
    n, m_sub = 2 * L, L // RADIX
    period2 = 2 * n // RADIX
    idx = jnp.arange(m_sub, dtype=jnp.int32)
    q = ((2 * idx[:, None] + 1) * idx[None, :]) % period2
    ang = q.astype(F32) * (2.0 * math.pi / period2)
    cm, sm = jnp.cos(ang).astype(BF16), (-jnp.sin(ang)).astype(BF16)
    tw = ((2 * idx + 1).astype(F32) * (math.pi / n))[None, :] * jnp.arange(1, RADIX, dtype=F32)[:, None]
    lanes = lambda v: jnp.broadcast_to(v[:, :, None], (RADIX - 1, m_sub, LANES))
    return dict(cm=cm, sm=sm, icm=cm.T * (1.0 / L), ism=sm.T * (1.0 / L),
                twr=lanes(jnp.cos(tw)), twi=lanes(-jnp.sin(tw)))


def _twiddles(twr_ref, twi_ref, width):
    tile = lambda v: jnp.concatenate([v] * (width // LANES), axis=1)
    return [(tile(twr_ref[s]), tile(twi_ref[s])) for s in range(RADIX - 1)]


def _butterfly4(t0, t1, t2, t3):
    pr, pi, mr, mi = t0[0] + t2[0], t0[1] + t2[1], t0[0] - t2[0], t0[1] - t2[1]
    qr, qi, nr, ni = t1[0] + t3[0], t1[1] + t3[1], t1[0] - t3[0], t1[1] - t3[1]
    return (pr + qr, pi + qi), (mr - ni, mi + nr), (mr + ni, mi - nr), (pr - qr, pi - qi)


def _split_dft(parts, tw, cm_ref, sm_ref):
    c = parts[0].shape[1]
    cat = jnp.concatenate(parts, axis=1).astype(BF16)
    re, im = _dot(cm_ref[...], cat), _dot(sm_ref[...], cat)
    t = [(re[:, :c], im[:, :c])]
    for s in range(1, RADIX):
        er, ei = re[:, s * c:(s + 1) * c], im[:, s * c:(s + 1) * c]
        wr, wi = tw[s - 1]
        t.append((er * wr - ei * wi, er * wi + ei * wr))
    return _butterfly4(*t)


def _filter_hidden_kernel(feats_ref, w1_ref, b1_ref, w2_ref, b2_ref, fq_ref, h_ref):
    fq = fq_ref[0]
    h = jnp.sin(fq[0:1] * (_dot3(feats_ref[...], w1_ref[0]) + b1_ref[0]))
    h_ref[0] = jnp.sin(fq[1:2] * (_dot3(h, w2_ref[0]) + b2_ref[0]))


def _filter_kernel(h_ref, w3f_ref, w3b_ref, b3f_ref, b3b_ref, dl_ref, cm_ref, sm_ref, twr_ref, twi_ref,
                   k_ref, *, L):
    m_sub = L // RADIX
    c = dl_ref.shape[1]
    h = h_ref[0]
    row = lax.broadcasted_iota(jnp.int32, (L, c), 0)
    lag = RADIX * (row % m_sub) + row // m_sub
    win = jnp.exp(-(lag.astype(F32) * (1.0 / (L - 1))) * dl_ref[...])
    hf = (_dot3(h, w3f_ref[0]) + b3f_ref[0]) * win
    hb = jnp.where(lag == 0, 0.0, (_dot3(h, w3b_ref[0]) + b3b_ref[0]) * win)
    nrm = jnp.sum(jnp.abs(hf), axis=0, keepdims=True) + jnp.sum(jnp.abs(hb), axis=0, keepdims=True)
    hf, hb = hf / nrm, hb / nrm
    parts = [jnp.concatenate([hf[s * m_sub:(s + 1) * m_sub], hb[s * m_sub:(s + 1) * m_sub]], axis=1)
             for s in range(RADIX)]
    spec = _split_dft(parts, _twiddles(twr_ref, twi_ref, 2 * c), cm_ref, sm_ref)
    for j, (re, im) in enumerate(spec):
        k_ref[0, 0, 2 * j] = re[:, :c] + re[:, c:]
        k_ref[0, 0, 2 * j + 1] = im[:, :c] - im[:, c:]


def _filter_spectra(L, tabs, filt_w1, filt_b1, filt_w2, filt_b2, filt_w3, filt_b3, filt_freq):
    m_sub = L // RADIX
    t = jnp.linspace(0.0, 1.0, L, dtype=F32)[:, None]
    bands = jnp.linspace(1e-4, FILTER_BANDS - 1, FILTER_BANDS, dtype=F32)
    w = (2.0 * math.pi / L) * jnp.arange(L, dtype=F32)[:, None]
    feats = jnp.concatenate([t, jnp.cos(w * bands), jnp.sin(w * bands)], axis=-1)
    feats = jnp.concatenate([feats[s::RADIX] for s in range(RADIX)], axis=0)
    feats = jnp.pad(feats, ((0, 0), (0, FILT_PAD - FILTER_EMB)))
    w1 = jnp.pad(filt_w1, ((0, 0), (0, FILT_PAD - FILTER_EMB), (0, 0)))
    deltas = jnp.abs(jnp.linspace(MIN_DECAY, MAX_DECAY, W_BRANCH, dtype=F32))[None, :]
    lay1 = lambda i: (i, 0, 0)
    hidden = pl.pallas_call(
        _filter_hidden_kernel,
        grid=(DEPTH,),
        in_specs=[_const_spec((L, FILT_PAD)),
                  pl.BlockSpec((1, FILT_PAD, FILTER_HIDDEN), lay1),
                  pl.BlockSpec((1, 1, FILTER_HIDDEN), lay1),
                  pl.BlockSpec((1, FILTER_HIDDEN, FILTER_HIDDEN), lay1),
                  pl.BlockSpec((1, 1, FILTER_HIDDEN), lay1),
                  pl.BlockSpec((1, 2, FILTER_HIDDEN), lay1)],
        out_specs=pl.BlockSpec((1, L, FILTER_HIDDEN), lay1),
        out_shape=jax.ShapeDtypeStruct((DEPTH, L, FILTER_HIDDEN), F32),
        compiler_params=_params(1),
        name=f"filter_hidden_{L}",
    )(feats, w1, filt_b1[:, None], filt_w2, filt_b2[:, None], filt_freq)
    nc = W_BRANCH // HALF
    fwd = lambda i, o, c: (i, 0, o * 2 * nc + c)
    bwd = lambda i, o, c: (i, 0, o * 2 * nc + nc + c)
    tab = _const_spec((m_sub, m_sub))
    tw = _const_spec((RADIX - 1, m_sub, LANES))
    return pl.pallas_call(
        functools.partial(_filter_kernel, L=L),
        grid=(DEPTH, HYENA_ORDER, nc),
        in_specs=[pl.BlockSpec((1, L, FILTER_HIDDEN), lambda i, o, c: (i, 0, 0)),
                  pl.BlockSpec((1, FILTER_HIDDEN, HALF), fwd),
                  pl.BlockSpec((1, FILTER_HIDDEN, HALF), bwd),
                  pl.BlockSpec((1, 1, HALF), fwd),
                  pl.BlockSpec((1, 1, HALF), bwd),
                  pl.BlockSpec((1, HALF), lambda i, o, c: (0, c)),
                  tab, tab, tw, tw],
        out_specs=pl.BlockSpec((1, 1, 2 * RADIX, m_sub, HALF), lambda i, o, c: (i, o, 0, 0, c)),
        out_shape=jax.ShapeDtypeStruct((DEPTH, HYENA_ORDER, 2 * RADIX, m_sub, W_BRANCH), F32),
        compiler_params=_params(3),
        name=f"filter_spectra_{L}",
    )(hidden, filt_w3, filt_w3, filt_b3[:, None], filt_b3[:, None], deltas,
      tabs["cm"], tabs["sm"], tabs["twr"], tabs["twi"])


def _head_norm(v, bd_ref, g):
    ss = _dot((v * v).astype(BF16), bd_ref[...])
    return v * lax.rsqrt(ss * (1.0 / HEAD_DIM) + EPS) * g


def _rope(v, cos, sin):
    n = v.shape[1] // LANES
    lane = lax.broadcasted_iota(jnp.int32, (v.shape[0], LANES), 1)
    first = (lane % 32) < 16
    out = []
    for i in range(n):
        c = v[:, i * LANES:(i + 1) * LANES]
        partner = jnp.where(first, pltpu.roll(c, LANES - 16, 1), pltpu.roll(c, 16, 1))
        out.append(c * cos + partner * sin)
    return jnp.concatenate(out, axis=1)


def _kv_tiles(k, v):
    low = lax.broadcasted_iota(jnp.int32, k.shape, 1) < HEAD_DIM
    ks, vs = pltpu.roll(k, HEAD_DIM, 1), pltpu.roll(v, HEAD_DIM, 1)
    kk = jnp.concatenate([jnp.where(low, k, ks), jnp.where(low, ks, k)], axis=1)
    vv = jnp.concatenate([jnp.where(low, v, 1.0), jnp.where(low, 1.0, vs),
                          jnp.where(low, vs, 1.0), jnp.where(low, 1.0, v)], axis=1)
    return kk.astype(BF16), vv.astype(BF16)


def _inproj_kernel(*refs, rope, kv_only):
    it = iter(refs)
    x_ref, sc_ref, sh_ref, g_ref = next(it), next(it), next(it), next(it)
    if not kv_only:
        wa_ref, wh_ref = next(it), next(it)
    wq_ref = next(it)
    if not kv_only:
        gq_ref, bdq_ref = next(it), next(it)
    gk_ref, bdk_ref = next(it), next(it)
    if rope:
        cos_ref, sin_ref = next(it), next(it)
    outs = list(it)

    h = _rms_mod(x_ref[0], g_ref[...], sc_ref[0], sh_ref[0]).astype(BF16)
    w = W_BRANCH
    if kv_only:
        kk_ref, vv_ref = outs
        acc = _dot(h, wq_ref[...])
        k = _head_norm(acc[:, :KV_COLS], bdk_ref, gk_ref[...])
        kk_ref[0], vv_ref[0] = _kv_tiles(k, acc[:, KV_COLS:])
        return

    ua_ref, ga_ref, hp_ref, sgh_ref, q_ref, kk_ref, vv_ref, sz_ref = outs
    acc = _dot(h, wq_ref[...])
    q = _head_norm(acc[:, :Q_COLS], bdq_ref, gq_ref[...])
    k = _head_norm(acc[:, Q_COLS:Q_COLS + KV_COLS], bdk_ref, gk_ref[...])
    if rope:
        q = _rope(q, cos_ref[...], sin_ref[...])
        k = _rope(k, cos_ref[...], sin_ref[...])
    q_ref[0] = (q * Q_SCALE).astype(BF16)
    kk_ref[0], vv_ref[0] = _kv_tiles(k, acc[:, Q_COLS + KV_COLS:Q_COLS + 2 * KV_COLS])
    sz_ref[0] = _silu(acc[:, Q_COLS + 2 * KV_COLS:]).astype(BF16)
    acc = _dot(h, wa_ref[...])
    ua_ref[0] = (acc[:, 2 * w:3 * w] * acc[:, :w]).astype(BF16)
    ga_ref[0] = (acc[:, w:2 * w] * _silu(acc[:, 3 * w:])).astype(BF16)
    acc = _dot(h, wh_ref[...])
    hp_ref[0] = acc[:, :3 * w].astype(BF16)
    sgh_ref[0] = _silu(acc[:, 3 * w:]).astype(BF16)


def _inproj(x, sc, sh, g, wts, consts, *, rope, kv_only):
    B, T, D = x.shape
    tm = min(T, 512)
    row = lambda b, t: (b, t, 0)
    vec = pl.BlockSpec((1, 1, D), lambda b, t: (b, 0, 0))
    args = [x, sc, sh, g[None, :]]
    specs = [pl.BlockSpec((1, tm, D), row), vec, vec, _const_spec((1, D))]
    if not kv_only:
        args += [wts["wa"], wts["wh"], wts["wq"], consts["gq"], consts["bdq"]]
        specs += [_const_spec(wts["wa"].shape), _const_spec(wts["wh"].shape),
                  _const_spec(wts["wq"].shape), _const_spec(consts["gq"].shape),
                  _const_spec(consts["bdq"].shape)]
    else:
        args += [wts["wkv"]]
        specs += [_const_spec(wts["wkv"].shape)]
    args += [consts["gk"], consts["bdk"]]
    specs += [_const_spec(consts["gk"].shape), _const_spec(consts["bdk"].shape)]
    if rope:
        args += [consts["cos"], consts["sin"]]
        specs += [pl.BlockSpec((tm, LANES), lambda b, t: (t, 0))] * 2
    widths = ([KV_DUP_COLS, V_TILE_COLS] if kv_only else
              [W_BRANCH, W_BRANCH, 3 * W_BRANCH, W_BRANCH, Q_COLS, KV_DUP_COLS, V_TILE_COLS, W_BRANCH])
    return pl.pallas_call(
        functools.partial(_inproj_kernel, rope=rope, kv_only=kv_only),
        grid=(B, T // tm),
        in_specs=specs,
        out_specs=[pl.BlockSpec((1, tm, n), row) for n in widths],
        out_shape=[jax.ShapeDtypeStruct((B, T, n), BF16) for n in widths],
        compiler_params=_params(2),
        name=f"inproj_{T}" + ("_kv" if kv_only else ""),
    )(*args)


def _attn_kernel(*refs, n_src, shifted):
    q_ref, sz_ref = refs[0], refs[1]
    kv = refs[2:2 + 2 * n_src]
    o_ref = refs[-1]
    tq = q_ref.shape[1]
    low = lax.broadcasted_iota(jnp.int32, (tq, LANES), 1) < HEAD_DIM
    nt = (((1,), (1,)), ((), ()))
    for j in range(2):
        cols = slice(j * LANES, (j + 1) * LANES)
        qp = q_ref[0, :, cols].astype(F32)
        halves = []
        for half, sel in enumerate((low, jnp.logical_not(low))):
            qm = jnp.where(sel, qp, 0.0).astype(BF16)
            vt = slice(half * LANES, (half + 1) * LANES)
            if shifted:
                s = [lax.dot_general(qm, kv[2 * i][0], nt, preferred_element_type=F32) for i in range(n_src)]
                m = functools.reduce(jnp.maximum, [jnp.max(v, axis=-1, keepdims=True) for v in s])
                terms = [_dot(jnp.exp2(s[i] - m).astype(BF16), kv[2 * i + 1][0, :, vt]) for i in range(n_src)]
            else:
                terms = []
                for i in range(n_src):
                    n_keys = kv[2 * i].shape[1]
                    for c0 in range(0, n_keys, KEY_CHUNK):
                        rows = slice(c0, min(c0 + KEY_CHUNK, n_keys))
                        s = lax.dot_general(qm, kv[2 * i][0, rows, :], nt, preferred_element_type=F32)
                        terms.append(_dot(jnp.exp2(s).astype(BF16), kv[2 * i + 1][0, rows, vt]))
            halves.append(functools.reduce(jnp.add, terms))
        num = jnp.where(low, halves[0], halves[1])
        den = pltpu.roll(jnp.where(low, halves[1], halves[0]), HEAD_DIM, 1)
        o_ref[0, :, cols] = (num / den * sz_ref[0, :, cols].astype(F32)).astype(BF16)


def _attention(q, sz, sources, shift_free):
    B, T, _ = q.shape
    gw = Q_COLS // N_KV_HEADS
    args = [q, sz] + [a for src in sources for a in src]

    def call(shifted):
        tq = min(T, 256 if shifted else 1024)
        qspec = pl.BlockSpec((1, tq, gw), lambda b, g, t: (b, t, g))
        specs = [qspec, qspec]
        for kk, vv in sources:
            specs += [pl.BlockSpec((1, kk.shape[1], LANES), lambda b, g, t: (b, 0, g)),
                      pl.BlockSpec((1, vv.shape[1], 2 * LANES), lambda b, g, t: (b, 0, g))]
        return pl.pallas_call(
            functools.partial(_attn_kernel, n_src=len(sources), shifted=shifted),
            grid=(B, N_KV_HEADS, T // tq),
            in_specs=specs,
            out_specs=qspec,
            out_shape=jax.ShapeDtypeStruct((B, T, Q_COLS), BF16),
            compiler_params=_params(3),
            name=f"attention_{T}" + ("_shifted" if shifted else ""),
        )
    return lax.cond(shift_free, call(False), call(True), *args)


def _deinterleave(x, scr):
    n, c = x.shape
    slabs = range(c // LANES)
    for j in slabs:
        scr[j][...] = x[:, j * LANES:(j + 1) * LANES]
    return [jnp.concatenate([scr[j][pl.ds(s, n // RADIX, stride=RADIX), :] for j in slabs], axis=1)
            for s in range(RADIX)]


def _interleave(parts, scr):
    m, c = parts[0].shape
    slabs = range(c // LANES)
    for j in slabs:
        for s in range(RADIX):
            scr[j][pl.ds(s, m, stride=RADIX), :] = parts[s][:, j * LANES:(j + 1) * LANES]
    return jnp.concatenate([scr[j][...] for j in slabs], axis=1)


def _conv3_split(parts, w):
    m = parts[0].shape[0]
    row = lax.broadcasted_iota(jnp.int32, parts[0].shape, 0)
    before = jnp.where(row == 0, 0.0, pltpu.roll(parts[-1], 1, 0))
    after = jnp.where(row == m - 1, 0.0, pltpu.roll(parts[0], m - 1, 0))
    ext = [before] + list(parts) + [after]
    return [ext[s] * w[0:1] + ext[s + 1] * w[1:2] + ext[s + 2] * w[2:3] for s in range(RADIX)]


def _long_conv_split(parts, k, tw, cm_ref, sm_ref, icm_ref, ism_ref):
    c = parts[0].shape[1]
    spec = _split_dft(parts, tw, cm_ref, sm_ref)
    ya, yb, yc, yd = [(xr * k[2 * j] - xi * k[2 * j + 1], xr * k[2 * j + 1] + xi * k[2 * j])
                      for j, (xr, xi) in enumerate(spec)]
    g0, g1, g3, g2 = _butterfly4(ya, yc, yd, yb)
    g = [g0, g1, g2, g3]
    for s in range(1, RADIX):
        (gr, gi), (wr, wi) = g[s], tw[s - 1]
        g[s] = (gr * wr + gi * wi, gi * wr - gr * wi)
    re = jnp.concatenate([v[0] for v in g], axis=1).astype(BF16)
    im = jnp.concatenate([v[1] for v in g], axis=1).astype(BF16)
    y = _dot(icm_ref[...], re) + _dot(ism_ref[...], im)
    return [y[:, s * c:(s + 1) * c] for s in range(RADIX)]


def _hyena_kernel(hv_ref, hx1_ref, hx2_ref, sg_ref, wv_ref, wx1_ref, wx2_ref, bias_ref,
                  cm_ref, sm_ref, icm_ref, ism_ref, twr_ref, twi_ref, k_ref, o_ref, *scratch):
    c = o_ref.shape[2]
    ns = c // LANES
    s_v, s_x1, s_x2, s_g, s_o = [scratch[i * ns:(i + 1) * ns] for i in range(5)]
    tw = _twiddles(twr_ref, twi_ref, c)
    tabs = (cm_ref, sm_ref, icm_ref, ism_ref)
    split = lambda ref, scr: _deinterleave(ref[0].astype(F32), scr)
    v = _conv3_split(split(hv_ref, s_v), wv_ref[...])
    x1 = _conv3_split(split(hx1_ref, s_x1), wx1_ref[...])
    x2 = _conv3_split(split(hx2_ref, s_x2), wx2_ref[...])
    gate = split(sg_ref, s_g)
    b0, b1 = bias_ref[0:1], bias_ref[1:2]
    n_k = 2 * RADIX
    y = _long_conv_split(v, [k_ref[0, 0, j] for j in range(n_k)], tw, *tabs)
    z1 = [x1[s] * (y[s] + v[s] * b0) for s in range(RADIX)]
    y = _long_conv_split(z1, [k_ref[0, 1, j] for j in range(n_k)], tw, *tabs)
    out = [x2[s] * (y[s] + z1[s] * b1) * gate[s] for s in range(RADIX)]
    o_ref[0] = _interleave(out, s_o).astype(BF16)


def _hyena(hpre, sgh, conv_h, bias, tabs, spectra, layer):
    B, L, _ = hpre.shape
    m_sub = L // RADIX
    nc = W_BRANCH // HALF
    blk = lambda j: pl.BlockSpec((1, L, HALF), lambda c, b: (b, 0, j * nc + c))
    cw = lambda j: pl.BlockSpec((3, HALF), lambda c, b: (0, j * nc + c))
    tab = _const_spec((m_sub, m_sub))
    tw = _const_spec((RADIX - 1, m_sub, LANES))
    kspec = pl.BlockSpec((1, HYENA_ORDER, 2 * RADIX, m_sub, HALF), lambda c, b: (layer, 0, 0, 0, c),
                         pipeline_mode=pl.Buffered(1))
    return pl.pallas_call(
        _hyena_kernel,
        grid=(nc, B),
        in_specs=[blk(0), blk(1), blk(2), blk(0), cw(0), cw(1), cw(2),
                  pl.BlockSpec((HYENA_ORDER, HALF), lambda c, b: (0, c)),
                  tab, tab, tab, tab, tw, tw, kspec],
        out_specs=blk(0),
        out_shape=jax.ShapeDtypeStruct((B, L, W_BRANCH), BF16),
        scratch_shapes=[pltpu.VMEM((L, LANES), F32)] * (5 * HALF // LANES),
        compiler_params=_params(2),
        name=f"hyena_{L}",
    )(hpre, hpre, hpre, sgh, conv_h, conv_h, conv_h, bias,
      tabs["cm"], tabs["sm"], tabs["icm"], tabs["ism"], tabs["twr"], tabs["twi"], spectra)


def _merge_kernel(x_ref, sc_ref, sh_ref, gt_ref, g_ref, ua_ref, up_ref, un_ref, ga_ref, ca_ref,
                  yb_ref, yc_ref, wg_ref, wb_ref, wo_ref, o_ref):
    t, nt = pl.program_id(1), pl.num_programs(1)
    x = x_ref[0]
    h = _rms_mod(x, g_ref[...], sc_ref[0], sh_ref[0]).astype(BF16)
    halo = up_ref.shape[1]
    first = jnp.where(t == 0, 0.0, up_ref[0, halo - 1:halo, :].astype(F32))
    last = jnp.where(t == nt - 1, 0.0, un_ref[0, 0:1, :].astype(F32))
    ya = ga_ref[0].astype(F32) * _conv3(ua_ref[0].astype(F32), ca_ref[...], first, last)
    ys = (ya.astype(BF16), yb_ref[0], yc_ref[0])
    merged = None
    for n in range(N_BRANCH):
        gate = jax.nn.sigmoid(_dot(h, wg_ref[:, n * D_MODEL:(n + 1) * D_MODEL]))
        term = gate * _dot(ys[n], wb_ref[n])
        merged = term if merged is None else merged + term
    out = _dot(merged.astype(BF16), wo_ref[...])
    o_ref[0] = x + gt_ref[0] * out


def _merge(x, sc, sh, gt, g, ua, ga, conv_a, yb, yc, wts):
    B, T, D = x.shape
    tm = min(T, 512)
    halo = 16
    r = tm // halo
    row = lambda b, t: (b, t, 0)
    vec = pl.BlockSpec((1, 1, D), lambda b, t: (b, 0, 0))
    br = pl.BlockSpec((1, tm, W_BRANCH), row)
    prev = pl.BlockSpec((1, halo, W_BRANCH), lambda b, t: (b, jnp.maximum(t * r - 1, 0), 0))
    nxt = pl.BlockSpec((1, halo, W_BRANCH), lambda b, t: (b, jnp.minimum((t + 1) * r, T // halo - 1), 0))
    return pl.pallas_call(
        _merge_kernel,
        grid=(B, T // tm),
        in_specs=[pl.BlockSpec((1, tm, D), row), vec, vec, vec, _const_spec((1, D)),
                  br, prev, nxt, br, _const_spec((3, W_BRANCH)), br, br,
                  _const_spec(wts["wg"].shape), _const_spec(wts["wb"].shape), _const_spec(wts["wo"].shape)],
        out_specs=pl.BlockSpec((1, tm, D), row),
        out_shape=jax.ShapeDtypeStruct((B, T, D), F32),
        compiler_params=_params(2),
        name=f"merge_{T}",
    )(x, sc, sh, gt, g[None, :], ua, ua, ua, ga, conv_a, yb, yc, wts["wg"], wts["wb"], wts["wo"])


def _rope_tables(S):
    pos = jnp.arange(S, dtype=jnp.int32)
    coord = jnp.stack([(pos // GRID_W).astype(F32), (pos % GRID_W).astype(F32)], axis=1)
    inv = ROPE_BASE ** (-jnp.arange(ROPE_FREQS, dtype=F32) / ROPE_FREQS)
    ang = coord[:, :, None] * inv
    cos = jnp.repeat(jnp.cos(ang)[:, :, None, :], 2, axis=2).reshape(S, HEAD_DIM)
    sin = jnp.sin(ang)
    sin = jnp.stack([-sin, sin], axis=2).reshape(S, HEAD_DIM)
    return jnp.tile(cos, (1, 2)), jnp.tile(sin, (1, 2))


def _layer_weights(w_in, w_branch, w_out):
    return dict(
        wa=w_in[:, :A_COLS].astype(BF16),
        wh=w_in[:, H_OFF:C_OFF].astype(BF16),
        wq=w_in[:, C_OFF:G_OFF].astype(BF16),
        wkv=w_in[:, K_OFF:Z_OFF].astype(BF16),
        wg=w_in[:, G_OFF:].astype(BF16),
        wb=w_branch.astype(BF16),
        wo=w_out.astype(BF16),
    )


def _mixers(x, sc, sh, gt, g, wts, consts, conv_a, conv_h, bias, tabs, spectra, layer, rope, ctx_kv):
    ua, ga, hpre, sgh, q, kk, vv, sz = _inproj(x, sc, sh, g, wts, consts, rope=rope, kv_only=False)
    sources = ([ctx_kv] if ctx_kv is not None else []) + [(kk, vv)]
    yc = _attention(q, sz, sources, consts["shift_free"])
    yb = _hyena(hpre, sgh, conv_h, bias, tabs, spectra, layer)
    return _merge(x, sc, sh, gt, g, ua, ga, conv_a, yb, yc, wts), (kk, vv)


def kernel(x, c, ctx, c_ctx, norm_g, w_mod, b_mod, w_in, conv_a, conv_h, filt_w1, filt_b1, filt_w2,
           filt_b2, filt_w3, filt_b3, filt_freq, hyena_bias, q_norm_g, k_norm_g, w_branch, w_out):
    B, S, D = x.shape
    Lc = ctx.shape[1]
    cc = jnp.concatenate([c, c_ctx[None, :], jnp.zeros((MOD_ROWS - B - 1, D), F32)], axis=0)
    mod = _modulation(cc, w_mod, b_mod)

    tabs_s, tabs_c = _dft_tables(S), _dft_tables(Lc)
    filt = (filt_w1, filt_b1, filt_w2, filt_b2, filt_w3, filt_b3, filt_freq)
    spec_s = _filter_spectra(S, tabs_s, *filt)
    spec_c = _filter_spectra(Lc, tabs_c, *filt)

    cos, sin = _rope_tables(S)
    ones = jnp.ones((HEAD_DIM, HEAD_DIM), F32)
    bdq = jnp.kron(jnp.eye(N_HEADS, dtype=F32), ones).astype(BF16)
    bdk = jnp.kron(jnp.eye(N_KV_HEADS, dtype=F32), ones).astype(BF16)

    for i in range(DEPTH):
        last = i == DEPTH - 1
        wts = _layer_weights(w_in[i], w_branch[i], w_out[i])
        consts = dict(gq=jnp.tile(q_norm_g[i], N_HEADS)[None, :],
                      gk=jnp.tile(k_norm_g[i], N_KV_HEADS)[None, :],
                      bdq=bdq, bdk=bdk, cos=cos, sin=sin)
        score_bound = (HEAD_DIM * Q_SCALE) * jnp.max(jnp.abs(q_norm_g[i])) * jnp.max(jnp.abs(k_norm_g[i]))
        consts["shift_free"] = score_bound <= SAFE_LOG2_RANGE
        split = lambda rows: [rows[:, None, j * D:(j + 1) * D] for j in range(3)]
        sh, sc, gt = split(mod[i, :B])
        sh_c, sc_c, gt_c = [jnp.broadcast_to(v, (B, 1, D)) for v in split(mod[i, B:B + 1])]
        if last:
            ctx_kv = tuple(_inproj(ctx, sc_c, sh_c, norm_g[i], wts, consts, rope=False, kv_only=True))
        else:
            ctx_next, ctx_kv = _mixers(ctx, sc_c, sh_c, gt_c, norm_g[i], wts, consts, conv_a[i], conv_h[i],
                                       hyena_bias[i], tabs_c, spec_c, i, False, None)
        x, _ = _mixers(x, sc, sh, gt, norm_g[i], wts, consts, conv_a[i], conv_h[i],
                       hyena_bias[i], tabs_s, spec_s, i, True, ctx_kv)
        if not last:
            ctx = ctx_next
    return x
```

```python
import functools
import math

import jax
import jax.numpy as jnp
from jax import lax
from jax.experimental import pallas as pl
from jax.experimental.pallas import tpu as pltpu

F32 = jnp.float32
BF16 = jnp.bfloat16

D_MODEL = 1024
DEPTH = 4
GRID_W = 64
W_BRANCH = 512
N_BRANCH = 3
N_HEADS = 8
N_KV_HEADS = 2
HEAD_DIM = 64
ROPE_FREQS = HEAD_DIM // 4
ROPE_BASE = 10000.0
ATTN_SCALE = HEAD_DIM ** -0.5
HYENA_ORDER = 2
FILTER_EMB = 33
FILTER_BANDS = (FILTER_EMB - 1) // 2
FILTER_HIDDEN = 64
HYENA_TARGET = 1e-2
MIN_DECAY = math.log(HYENA_TARGET) / 1.5
MAX_DECAY = math.log(HYENA_TARGET) / 0.3
EPS = 1e-6

A_COLS = 4 * W_BRANCH
H_OFF = A_COLS
C_OFF = H_OFF + 4 * W_BRANCH
Q_COLS = N_HEADS * HEAD_DIM
KV_COLS = N_KV_HEADS * HEAD_DIM
K_OFF = C_OFF + Q_COLS
V_OFF = K_OFF + KV_COLS
Z_OFF = V_OFF + KV_COLS
G_OFF = Z_OFF + W_BRANCH

LANES = 128
VMEM_LIMIT_BYTES = 58 * 1024 * 1024
KV_DUP_COLS = 2 * KV_COLS
V_TILE_COLS = 2 * KV_DUP_COLS
Q_SCALE = ATTN_SCALE * math.log2(math.e)
SAFE_LOG2_RANGE = 64.0
KEY_CHUNK = 512
MOD_ROWS = 24
FILT_PAD = 64
HALF = 256
RADIX = 4


def _params(n_axes):
    return pltpu.CompilerParams(dimension_semantics=("arbitrary",) * n_axes,
                                vmem_limit_bytes=VMEM_LIMIT_BYTES)


def _const_spec(shape):
    nd = len(shape)
    return pl.BlockSpec(shape, lambda *_: (0,) * nd, pipeline_mode=pl.Buffered(1))


def _silu(v):
    return v * jax.nn.sigmoid(v)


def _dot(a, b):
    return jnp.dot(a, b, preferred_element_type=F32)


def _split(a):
    hi = a.astype(BF16)
    lo = (a - hi.astype(F32)).astype(BF16)
    return hi, lo


def _dot3(a, b):
    ah, al = _split(a)
    bh, bl = _split(b)
    return _dot(ah, bh) + (_dot(ah, bl) + _dot(al, bh))


def _rms_mod(x, g, sc, sh):
    y = x * lax.rsqrt(jnp.mean(x * x, axis=-1, keepdims=True) + EPS)
    return (y * g) * (1.0 + sc) + sh


def _shift_rows(u, first_row, last_row):
    n = u.shape[0]
    row = lax.broadcasted_iota(jnp.int32, u.shape, 0)
    prev = jnp.where(row == 0, first_row, pltpu.roll(u, 1, 0))
    nxt = jnp.where(row == n - 1, last_row, pltpu.roll(u, n - 1, 0))
    return prev, nxt


def _conv3(u, w, first_row=0.0, last_row=0.0):
    prev, nxt = _shift_rows(u, first_row, last_row)
    return prev * w[0:1] + u * w[1:2] + nxt * w[2:3]


def _mod_kernel(cc_ref, w_ref, b_ref, o_ref):
    o_ref[0] = _dot3(_silu(cc_ref[...]), w_ref[0]) + b_ref[0]


def _modulation(cc, w_mod, b_mod):
    return pl.pallas_call(
        _mod_kernel,
        grid=(DEPTH, 3),
        in_specs=[pl.BlockSpec((MOD_ROWS, D_MODEL), lambda i, j: (0, 0)),
                  pl.BlockSpec((1, D_MODEL, D_MODEL), lambda i, j: (i, 0, j)),
                  pl.BlockSpec((1, 1, D_MODEL), lambda i, j: (i, 0, j))],
        out_specs=pl.BlockSpec((1, MOD_ROWS, D_MODEL), lambda i, j: (i, 0, j)),
        out_shape=jax.ShapeDtypeStruct((DEPTH, MOD_ROWS, 3 * D_MODEL), F32),
        compiler_params=_params(2),
        name="modulation",
    )(cc, w_mod, b_mod.reshape(DEPTH, 1, 3 * D_MODEL))


def _dft_tables(L):
    n2, m_sub = 4 * L, L // RADIX
    idx = jnp.arange(m_sub, dtype=jnp.int32)
    sample = RADIX * idx[None, None, :] + jnp.arange(RADIX, dtype=jnp.int32)[:, None, None]
    q = ((2 * idx[None, :, None] + 1) * sample) % n2
    ang = q.astype(F32) * (2.0 * math.pi / n2)
    cm, sm = jnp.cos(ang).astype(BF16), (-jnp.sin(ang)).astype(BF16)
    tr = lambda v: jnp.swapaxes(v, 1, 2) * (1.0 / L)
    return dict(cm=cm, sm=sm, icm=tr(cm), ism=tr(sm))


def _butterfly4(t0, t1, t2, t3):
    pr, pi, mr, mi = t0[0] + t2[0], t0[1] + t2[1], t0[0] - t2[0], t0[1] - t2[1]
    qr, qi, nr, ni = t1[0] + t3[0], t1[1] + t3[1], t1[0] - t3[0], t1[1] - t3[1]
    return (pr + qr, pi + qi), (mr - ni, mi + nr), (mr + ni, mi - nr), (pr - qr, pi - qi)


def _split_dft(parts, cm_ref, sm_ref):
    t = []
    for s in range(RADIX):
        u = parts[s].astype(BF16)
        t.append((_dot(cm_ref[s], u), _dot(sm_ref[s], u)))
    return _butterfly4(*t)


def _filter_hidden_kernel(feats_ref, w1_ref, b1_ref, w2_ref, b2_ref, fq_ref, h_ref):
    fq = fq_ref[0]
    h = jnp.sin(fq[0:1] * (_dot3(feats_ref[...], w1_ref[0]) + b1_ref[0]))
    h_ref[0] = jnp.sin(fq[1:2] * (_dot3(h, w2_ref[0]) + b2_ref[0]))


def _filter_kernel(h_ref, w3f_ref, w3b_ref, b3f_ref, b3b_ref, dl_ref, cm_ref, sm_ref, k_ref, *, L):
    m_sub = L // RADIX
    c = dl_ref.shape[1]
    h = h_ref[0]
    row = lax.broadcasted_iota(jnp.int32, (L, c), 0)
    lag = RADIX * (row % m_sub) + row // m_sub
    win = jnp.exp(-(lag.astype(F32) * (1.0 / (L - 1))) * dl_ref[...])
    hf = (_dot3(h, w3f_ref[0]) + b3f_ref[0]) * win
    hb = jnp.where(lag == 0, 0.0, (_dot3(h, w3b_ref[0]) + b3b_ref[0]) * win)
    nrm = jnp.sum(jnp.abs(hf), axis=0, keepdims=True) + jnp.sum(jnp.abs(hb), axis=0, keepdims=True)
    hf, hb = hf / nrm, hb / nrm
    parts = [jnp.concatenate([hf[s * m_sub:(s + 1) * m_sub], hb[s * m_sub:(s + 1) * m_sub]], axis=1)
             for s in range(RADIX)]
    spec = _split_dft(parts, cm_ref, sm_ref)
    for j, (re, im) in enumerate(spec):
        k_ref[0, 0, 2 * j] = re[:, :c] + re[:, c:]
        k_ref[0, 0, 2 * j + 1] = im[:, :c] - im[:, c:]


def _filter_spectra(L, tabs, filt_w1, filt_b1, filt_w2, filt_b2, filt_w3, filt_b3, filt_freq):
    m_sub = L // RADIX
    t = jnp.linspace(0.0, 1.0, L, dtype=F32)[:, None]
    bands = jnp.linspace(1e-4, FILTER_BANDS - 1, FILTER_BANDS, dtype=F32)
    w = (2.0 * math.pi / L) * jnp.arange(L, dtype=F32)[:, None]
    feats = jnp.concatenate([t, jnp.cos(w * bands), jnp.sin(w * bands)], axis=-1)
    feats = jnp.concatenate([feats[s::RADIX] for s in range(RADIX)], axis=0)
    feats = jnp.pad(feats, ((0, 0), (0, FILT_PAD - FILTER_EMB)))
    w1 = jnp.pad(filt_w1, ((0, 0), (0, FILT_PAD - FILTER_EMB), (0, 0)))
    deltas = jnp.abs(jnp.linspace(MIN_DECAY, MAX_DECAY, W_BRANCH, dtype=F32))[None, :]
    lay1 = lambda i: (i, 0, 0)
    hidden = pl.pallas_call(
        _filter_hidden_kernel,
        grid=(DEPTH,),
        in_specs=[_const_spec((L, FILT_PAD)),
                  pl.BlockSpec((1, FILT_PAD, FILTER_HIDDEN), lay1),
                  pl.BlockSpec((1, 1, FILTER_HIDDEN), lay1),
                  pl.BlockSpec((1, FILTER_HIDDEN, FILTER_HIDDEN), lay1),
                  pl.BlockSpec((1, 1, FILTER_HIDDEN), lay1),
                  pl.BlockSpec((1, 2, FILTER_HIDDEN), lay1)],
        out_specs=pl.BlockSpec((1, L, FILTER_HIDDEN), lay1),
        out_shape=jax.ShapeDtypeStruct((DEPTH, L, FILTER_HIDDEN), F32),
        compiler_params=_params(1),
        name=f"filter_hidden_{L}",
    )(feats, w1, filt_b1[:, None], filt_w2, filt_b2[:, None], filt_freq)
    nc = W_BRANCH // HALF
    fwd = lambda i, o, c: (i, 0, o * 2 * nc + c)
    bwd = lambda i, o, c: (i, 0, o * 2 * nc + nc + c)
    tab = _const_spec((RADIX, m_sub, m_sub))
    return pl.pallas_call(
        functools.partial(_filter_kernel, L=L),
        grid=(DEPTH, HYENA_ORDER, nc),
        in_specs=[pl.BlockSpec((1, L, FILTER_HIDDEN), lambda i, o, c: (i, 0, 0)),
                  pl.BlockSpec((1, FILTER_HIDDEN, HALF), fwd),
                  pl.BlockSpec((1, FILTER_HIDDEN, HALF), bwd),
                  pl.BlockSpec((1, 1, HALF), fwd),
                  pl.BlockSpec((1, 1, HALF), bwd),
                  pl.BlockSpec((1, HALF), lambda i, o, c: (0, c)),
                  tab, tab],
        out_specs=pl.BlockSpec((1, 1, 2 * RADIX, m_sub, HALF), lambda i, o, c: (i, o, 0, 0, c)),
        out_shape=jax.ShapeDtypeStruct((DEPTH, HYENA_ORDER, 2 * RADIX, m_sub, W_BRANCH), F32),
        compiler_params=_params(3),
        name=f"filter_spectra_{L}",
    )(hidden, filt_w3, filt_w3, filt_b3[:, None], filt_b3[:, None], deltas,
      tabs["cm"], tabs["sm"])


def _head_norm(v, bd_ref, g):
    ss = _dot((v * v).astype(BF16), bd_ref[...])
    return v * lax.rsqrt(ss * (1.0 / HEAD_DIM) + EPS) * g


def _rope(v, cos, sin):
    n = v.shape[1] // LANES
    lane = lax.broadcasted_iota(jnp.int32, (v.shape[0], LANES), 1)
    first = (lane % 32) < 16
    out = []
    for i in range(n):
        c = v[:, i * LANES:(i + 1) * LANES]
        partner = jnp.where(first, pltpu.roll(c, LANES - 16, 1), pltpu.roll(c, 16, 1))
        out.append(c * cos + partner * sin)
    return jnp.concatenate(out, axis=1)


def _kv_tiles(k, v):
    low = lax.broadcasted_iota(jnp.int32, k.shape, 1) < HEAD_DIM
    ks, vs = pltpu.roll(k, HEAD_DIM, 1), pltpu.roll(v, HEAD_DIM, 1)
    kk = jnp.concatenate([jnp.where(low, k, ks), jnp.where(low, ks, k)], axis=1)
    vv = jnp.concatenate([jnp.where(low, v, 1.0), jnp.where(low, 1.0, vs),
                          jnp.where(low, vs, 1.0), jnp.where(low, 1.0, v)], axis=1)
    return kk.astype(BF16), vv.astype(BF16)


def _inproj_kernel(*refs, rope, kv_only):
    it = iter(refs)
    x_ref, sc_ref, sh_ref, g_ref = next(it), next(it), next(it), next(it)
    if not kv_only:
        wa_ref, wh_ref = next(it), next(it)
    wq_ref = next(it)
    if not kv_only:
        gq_ref, bdq_ref = next(it), next(it)
    gk_ref, bdk_ref = next(it), next(it)
    if rope:
        cos_ref, sin_ref = next(it), next(it)
    outs = list(it)

    h = _rms_mod(x_ref[0], g_ref[...], sc_ref[0], sh_ref[0]).astype(BF16)
    w = W_BRANCH
    if kv_only:
        kk_ref, vv_ref = outs
        acc = _dot(h, wq_ref[...])
        k = _head_norm(acc[:, :KV_COLS], bdk_ref, gk_ref[...])
        kk_ref[0], vv_ref[0] = _kv_tiles(k, acc[:, KV_COLS:])
        return

    ua_ref, ga_ref, hp_ref, sgh_ref, q_ref, kk_ref, vv_ref, sz_ref = outs
    acc = _dot(h, wq_ref[...])
    q = _head_norm(acc[:, :Q_COLS], bdq_ref, gq_ref[...])
    k = _head_norm(acc[:, Q_COLS:Q_COLS + KV_COLS], bdk_ref, gk_ref[...])
    if rope:
        q = _rope(q, cos_ref[...], sin_ref[...])
        k = _rope(k, cos_ref[...], sin_ref[...])
    q_ref[0] = (q * Q_SCALE).astype(BF16)
    kk_ref[0], vv_ref[0] = _kv_tiles(k, acc[:, Q_COLS + KV_COLS:Q_COLS + 2 * KV_COLS])
    sz_ref[0] = _silu(acc[:, Q_COLS + 2 * KV_COLS:]).astype(BF16)
    acc = _dot(h, wa_ref[...])
    ua_ref[0] = (acc[:, 2 * w:3 * w] * acc[:, :w]).astype(BF16)
    ga_ref[0] = (acc[:, w:2 * w] * _silu(acc[:, 3 * w:])).astype(BF16)
    acc = _dot(h, wh_ref[...])
    hp_ref[0] = acc[:, :3 * w].astype(BF16)
    sgh_ref[0] = _silu(acc[:, 3 * w:]).astype(BF16)


def _inproj(x, sc, sh, g, wts, consts, *, rope, kv_only):
    B, T, D = x.shape
    tm = min(T, 512)
    row = lambda b, t: (b, t, 0)
    vec = pl.BlockSpec((1, 1, D), lambda b, t: (b, 0, 0))
    args = [x, sc, sh, g[None, :]]
    specs = [pl.BlockSpec((1, tm, D), row), vec, vec, _const_spec((1, D))]
    if not kv_only:
        args += [wts["wa"], wts["wh"], wts["wq"], consts["gq"], consts["bdq"]]
        specs += [_const_spec(wts["wa"].shape), _const_spec(wts["wh"].shape),
                  _const_spec(wts["wq"].shape), _const_spec(consts["gq"].shape),
                  _const_spec(consts["bdq"].shape)]
    else:
        args += [wts["wkv"]]
        specs += [_const_spec(wts["wkv"].shape)]
    args += [consts["gk"], consts["bdk"]]
    specs += [_const_spec(consts["gk"].shape), _const_spec(consts["bdk"].shape)]
    if rope:
        args += [consts["cos"], consts["sin"]]
        specs += [pl.BlockSpec((tm, LANES), lambda b, t: (t, 0))] * 2
    widths = ([KV_DUP_COLS, V_TILE_COLS] if kv_only else
              [W_BRANCH, W_BRANCH, 3 * W_BRANCH, W_BRANCH, Q_COLS, KV_DUP_COLS, V_TILE_COLS, W_BRANCH])
    return pl.pallas_call(
        functools.partial(_inproj_kernel, rope=rope, kv_only=kv_only),
        grid=(B, T // tm),
        in_specs=specs,
        out_specs=[pl.BlockSpec((1, tm, n), row) for n in widths],
        out_shape=[jax.ShapeDtypeStruct((B, T, n), BF16) for n in widths],
        compiler_params=_params(2),
        name=f"inproj_{T}" + ("_kv" if kv_only else ""),
    )(*args)


def _attn_kernel(*refs, n_src, shifted):
    q_ref, sz_ref = refs[0], refs[1]
    kv = refs[2:2 + 2 * n_src]
    o_ref = refs[-1]
    tq = q_ref.shape[1]
    low = lax.broadcasted_iota(jnp.int32, (tq, LANES), 1) < HEAD_DIM
    nt = (((1,), (1,)), ((), ()))
    for j in range(2):
        cols = slice(j * LANES, (j + 1) * LANES)
        qp = q_ref[0, :, cols].astype(F32)
        halves = []
        for half, sel in enumerate((low, jnp.logical_not(low))):
            qm = jnp.where(sel, qp, 0.0).astype(BF16)
            vt = slice(half * LANES, (half + 1) * LANES)
            if shifted:
                s = [lax.dot_general(qm, kv[2 * i][0], nt, preferred_element_type=F32) for i in range(n_src)]
                m = functools.reduce(jnp.maximum, [jnp.max(v, axis=-1, keepdims=True) for v in s])
                terms = [_dot(jnp.exp2(s[i] - m).astype(BF16), kv[2 * i + 1][0, :, vt]) for i in range(n_src)]
            else:
                terms = []
                for i in range(n_src):
                    n_keys = kv[2 * i].shape[1]
                    for c0 in range(0, n_keys, KEY_CHUNK):
                        rows = slice(c0, min(c0 + KEY_CHUNK, n_keys))
                        s = lax.dot_general(qm, kv[2 * i][0, rows, :], nt, preferred_element_type=F32)
                        terms.append(_dot(jnp.exp2(s).astype(BF16), kv[2 * i + 1][0, rows, vt]))
            halves.append(functools.reduce(jnp.add, terms))
        num = jnp.where(low, halves[0], halves[1])
        den = pltpu.roll(jnp.where(low, halves[1], halves[0]), HEAD_DIM, 1)
        o_ref[0, :, cols] = (num / den * sz_ref[0, :, cols].astype(F32)).astype(BF16)


def _attention(q, sz, sources, shift_free):
    B, T, _ = q.shape
    gw = Q_COLS // N_KV_HEADS
    args = [q, sz] + [a for src in sources for a in src]

    def call(shifted):
        tq = min(T, 256 if shifted else 1024)
        qspec = pl.BlockSpec((1, tq, gw), lambda b, g, t: (b, t, g))
        specs = [qspec, qspec]
        for kk, vv in sources:
            specs += [pl.BlockSpec((1, kk.shape[1], LANES), lambda b, g, t: (b, 0, g)),
                      pl.BlockSpec((1, vv.shape[1], 2 * LANES), lambda b, g, t: (b, 0, g))]
        return pl.pallas_call(
            functools.partial(_attn_kernel, n_src=len(sources), shifted=shifted),
            grid=(B, N_KV_HEADS, T // tq),
            in_specs=specs,
            out_specs=qspec,
            out_shape=jax.ShapeDtypeStruct((B, T, Q_COLS), BF16),
            compiler_params=_params(3),
            name=f"attention_{T}" + ("_shifted" if shifted else ""),
        )
    return lax.cond(shift_free, call(False), call(True), *args)


def _deinterleave(x, scr):
    n, c = x.shape
    slabs = range(c // LANES)
    for j in slabs:
        scr[j][...] = x[:, j * LANES:(j + 1) * LANES]
    return [jnp.concatenate([scr[j][pl.ds(s, n // RADIX, stride=RADIX), :] for j in slabs], axis=1)
            for s in range(RADIX)]


def _interleave(parts, scr):
    m, c = parts[0].shape
    slabs = range(c // LANES)
    for j in slabs:
        for s in range(RADIX):
            scr[j][pl.ds(s, m, stride=RADIX), :] = parts[s][:, j * LANES:(j + 1) * LANES]
    return jnp.concatenate([scr[j][...] for j in slabs], axis=1)


def _conv3_split(parts, w):
    m = parts[0].shape[0]
    row = lax.broadcasted_iota(jnp.int32, parts[0].shape, 0)
    before = jnp.where(row == 0, 0.0, pltpu.roll(parts[-1], 1, 0))
    after = jnp.where(row == m - 1, 0.0, pltpu.roll(parts[0], m - 1, 0))
    ext = [before] + list(parts) + [after]
    return [ext[s] * w[0:1] + ext[s + 1] * w[1:2] + ext[s + 2] * w[2:3] for s in range(RADIX)]


def _long_conv_split(parts, k, cm_ref, sm_ref, icm_ref, ism_ref):
    spec = _split_dft(parts, cm_ref, sm_ref)
    ya, yb, yc, yd = [(xr * k[2 * j] - xi * k[2 * j + 1], xr * k[2 * j + 1] + xi * k[2 * j])
                      for j, (xr, xi) in enumerate(spec)]
    g0, g1, g3, g2 = _butterfly4(ya, yc, yd, yb)
    return [_dot(icm_ref[s], gr.astype(BF16)) + _dot(ism_ref[s], gi.astype(BF16))
            for s, (gr, gi) in enumerate((g0, g1, g2, g3))]


def _hyena_kernel(hv_ref, hx1_ref, hx2_ref, sg_ref, wv_ref, wx1_ref, wx2_ref, bias_ref,
                  cm_ref, sm_ref, icm_ref, ism_ref, k_ref, o_ref, *scratch):
    c = o_ref.shape[2]
    ns = c // LANES
    s_v, s_x1, s_x2, s_g, s_o = [scratch[i * ns:(i + 1) * ns] for i in range(5)]
    tabs = (cm_ref, sm_ref, icm_ref, ism_ref)
    split = lambda ref, scr: _deinterleave(ref[0].astype(F32), scr)
    v = _conv3_split(split(hv_ref, s_v), wv_ref[...])
    x1 = _conv3_split(split(hx1_ref, s_x1), wx1_ref[...])
    x2 = _conv3_split(split(hx2_ref, s_x2), wx2_ref[...])
    gate = split(sg_ref, s_g)
    b0, b1 = bias_ref[0:1], bias_ref[1:2]
    n_k = 2 * RADIX
    y = _long_conv_split(v, [k_ref[0, 0, j] for j in range(n_k)], *tabs)
    z1 = [x1[s] * (y[s] + v[s] * b0) for s in range(RADIX)]
    y = _long_conv_split(z1, [k_ref[0, 1, j] for j in range(n_k)], *tabs)
    out = [x2[s] * (y[s] + z1[s] * b1) * gate[s] for s in range(RADIX)]
    o_ref[0] = _interleave(out, s_o).astype(BF16)


def _hyena(hpre, sgh, conv_h, bias, tabs, spectra, layer):
    B, L, _ = hpre.shape
    m_sub = L // RADIX
    lanes = HALF if L * HALF * 4 > (1 << 20) else W_BRANCH
    nc = W_BRANCH // lanes
    blk = lambda j: pl.BlockSpec((1, L, lanes), lambda c, b: (b, 0, j * nc + c))
    cw = lambda j: pl.BlockSpec((3, lanes), lambda c, b: (0, j * nc + c))
    tab = _const_spec((RADIX, m_sub, m_sub))
    kspec = pl.BlockSpec((1, HYENA_ORDER, 2 * RADIX, m_sub, lanes), lambda c, b: (layer, 0, 0, 0, c),
                         pipeline_mode=pl.Buffered(1))
    return pl.pallas_call(
        _hyena_kernel,
        grid=(nc, B),
        in_specs=[blk(0), blk(1), blk(2), blk(0), cw(0), cw(1), cw(2),
                  pl.BlockSpec((HYENA_ORDER, lanes), lambda c, b: (0, c)),
                  tab, tab, tab, tab, kspec],
        out_specs=blk(0),
        out_shape=jax.ShapeDtypeStruct((B, L, W_BRANCH), BF16),
        scratch_shapes=[pltpu.VMEM((L, LANES), F32)] * (5 * lanes // LANES),
        compiler_params=_params(2),
        name=f"hyena_{L}",
    )(hpre, hpre, hpre, sgh, conv_h, conv_h, conv_h, bias,
      tabs["cm"], tabs["sm"], tabs["icm"], tabs["ism"], spectra)


def _merge_kernel(x_ref, sc_ref, sh_ref, gt_ref, g_ref, ua_ref, up_ref, un_ref, ga_ref, ca_ref,
                  yb_ref, yc_ref, wg_ref, wb_ref, wo_ref, o_ref):
    t, nt = pl.program_id(1), pl.num_programs(1)
    x = x_ref[0]
    h = _rms_mod(x, g_ref[...], sc_ref[0], sh_ref[0]).astype(BF16)
    halo = up_ref.shape[1]
    first = jnp.where(t == 0, 0.0, up_ref[0, halo - 1:halo, :].astype(F32))
    last = jnp.where(t == nt - 1, 0.0, un_ref[0, 0:1, :].astype(F32))
    ya = ga_ref[0].astype(F32) * _conv3(ua_ref[0].astype(F32), ca_ref[...], first, last)
    ys = (ya.astype(BF16), yb_ref[0], yc_ref[0])
    merged = None
    for n in range(N_BRANCH):
        gate = jax.nn.sigmoid(_dot(h, wg_ref[:, n * D_MODEL:(n + 1) * D_MODEL]))
        term = gate * _dot(ys[n], wb_ref[n])
        merged = term if merged is None else merged + term
    out = _dot(merged.astype(BF16), wo_ref[...])
    o_ref[0] = x + gt_ref[0] * out


def _merge(x, sc, sh, gt, g, ua, ga, conv_a, yb, yc, wts):
    B, T, D = x.shape
    tm = min(T, 512)
    halo = 16
    r = tm // halo
    row = lambda b, t: (b, t, 0)
    vec = pl.BlockSpec((1, 1, D), lambda b, t: (b, 0, 0))
    br = pl.BlockSpec((1, tm, W_BRANCH), row)
    prev = pl.BlockSpec((1, halo, W_BRANCH), lambda b, t: (b, jnp.maximum(t * r - 1, 0), 0))
    nxt = pl.BlockSpec((1, halo, W_BRANCH), lambda b, t: (b, jnp.minimum((t + 1) * r, T // halo - 1), 0))
    return pl.pallas_call(
        _merge_kernel,
        grid=(B, T // tm),
        in_specs=[pl.BlockSpec((1, tm, D), row), vec, vec, vec, _const_spec((1, D)),
                  br, prev, nxt, br, _const_spec((3, W_BRANCH)), br, br,
                  _const_spec(wts["wg"].shape), _const_spec(wts["wb"].shape), _const_spec(wts["wo"].shape)],
        out_specs=pl.BlockSpec((1, tm, D), row),
        out_shape=jax.ShapeDtypeStruct((B, T, D), F32),
        compiler_params=_params(2),
        name=f"merge_{T}",
    )(x, sc, sh, gt, g[None, :], ua, ua, ua, ga, conv_a, yb, yc, wts["wg"], wts["wb"], wts["wo"])


def _rope_tables(S):
    pos = jnp.arange(S, dtype=jnp.int32)
    coord = jnp.stack([(pos // GRID_W).astype(F32), (pos % GRID_W).astype(F32)], axis=1)
    inv = ROPE_BASE ** (-jnp.arange(ROPE_FREQS, dtype=F32) / ROPE_FREQS)
    ang = coord[:, :, None] * inv
    cos = jnp.repeat(jnp.cos(ang)[:, :, None, :], 2, axis=2).reshape(S, HEAD_DIM)
    sin = jnp.sin(ang)
    sin = jnp.stack([-sin, sin], axis=2).reshape(S, HEAD_DIM)
    return jnp.tile(cos, (1, 2)), jnp.tile(sin, (1, 2))


def _layer_weights(w_in, w_branch, w_out):
    return dict(
        wa=w_in[:, :A_COLS].astype(BF16),
        wh=w_in[:, H_OFF:C_OFF].astype(BF16),
        wq=w_in[:, C_OFF:G_OFF].astype(BF16),
        wkv=w_in[:, K_OFF:Z_OFF].astype(BF16),
        wg=w_in[:, G_OFF:].astype(BF16),
        wb=w_branch.astype(BF16),
        wo=w_out.astype(BF16),
    )


def _mixers(x, sc, sh, gt, g, wts, consts, conv_a, conv_h, bias, tabs, spectra, layer, rope, ctx_kv):
    ua, ga, hpre, sgh, q, kk, vv, sz = _inproj(x, sc, sh, g, wts, consts, rope=rope, kv_only=False)
    sources = ([ctx_kv] if ctx_kv is not None else []) + [(kk, vv)]
    yc = _attention(q, sz, sources, consts["shift_free"])
    yb = _hyena(hpre, sgh, conv_h, bias, tabs, spectra, layer)
    return _merge(x, sc, sh, gt, g, ua, ga, conv_a, yb, yc, wts), (kk, vv)


def kernel(x, c, ctx, c_ctx, norm_g, w_mod, b_mod, w_in, conv_a, conv_h, filt_w1, filt_b1, filt_w2,
           filt_b2, filt_w3, filt_b3, filt_freq, hyena_bias, q_norm_g, k_norm_g, w_branch, w_out):
    B, S, D = x.shape
    Lc = ctx.shape[1]
    cc = jnp.concatenate([c, c_ctx[None, :], jnp.zeros((MOD_ROWS - B - 1, D), F32)], axis=0)
    mod = _modulation(cc, w_mod, b_mod)

    tabs_s, tabs_c = _dft_tables(S), _dft_tables(Lc)
    filt = (filt_w1, filt_b1, filt_w2, filt_b2, filt_w3, filt_b3, filt_freq)
    spec_s = _filter_spectra(S, tabs_s, *filt)
    spec_c = _filter_spectra(Lc, tabs_c, *filt)

    cos, sin = _rope_tables(S)
    ones = jnp.ones((HEAD_DIM, HEAD_DIM), F32)
    bdq = jnp.kron(jnp.eye(N_HEADS, dtype=F32), ones).astype(BF16)
    bdk = jnp.kron(jnp.eye(N_KV_HEADS, dtype=F32), ones).astype(BF16)

    for i in range(DEPTH):
        last = i == DEPTH - 1
        wts = _layer_weights(w_in[i], w_branch[i], w_out[i])
        consts = dict(gq=jnp.tile(q_norm_g[i], N_HEADS)[None, :],
                      gk=jnp.tile(k_norm_g[i], N_KV_HEADS)[None, :],
                      bdq=bdq, bdk=bdk, cos=cos, sin=sin)
        score_bound = (HEAD_DIM * Q_SCALE) * jnp.max(jnp.abs(q_norm_g[i])) * jnp.max(jnp.abs(k_norm_g[i]))
        consts["shift_free"] = score_bound <= SAFE_LOG2_RANGE
        split = lambda rows: [rows[:, None, j * D:(j + 1) * D] for j in range(3)]
        sh, sc, gt = split(mod[i, :B])
        sh_c, sc_c, gt_c = [jnp.broadcast_to(v, (B, 1, D)) for v in split(mod[i, B:B + 1])]
        if last:
            ctx_kv = tuple(_inproj(ctx, sc_c, sh_c, norm_g[i], wts, consts, rope=False, kv_only=True))
        else:
            ctx_next, ctx_kv = _mixers(ctx, sc_c, sh_c, gt_c, norm_g[i], wts, consts, conv_a[i], conv_h[i],
                                       hyena_bias[i], tabs_c, spec_c, i, False, None)
        x, _ = _mixers(x, sc, sh, gt, norm_g[i], wts, consts, conv_a[i], conv_h[i],
                       hyena_bias[i], tabs_s, spec_s, i, True, ctx_kv)
        if not last:
            ctx = ctx_next
    return x
```

```python
import functools
import math

import jax
import jax.numpy as jnp
from jax import lax
from jax.experimental import pallas as pl
from jax.experimental.pallas import tpu as pltpu

F32 = jnp.float32
BF16 = jnp.bfloat16

D_MODEL = 1024
DEPTH = 4
GRID_W = 64
W_BRANCH = 512
N_BRANCH = 3
N_HEADS = 8
N_KV_HEADS = 2
HEAD_DIM = 64
ROPE_FREQS = HEAD_DIM // 4
ROPE_BASE = 10000.0
ATTN_SCALE = HEAD_DIM ** -0.5
HYENA_ORDER = 2
FILTER_EMB = 33
FILTER_BANDS = (FILTER_EMB - 1) // 2
FILTER_HIDDEN = 64
HYENA_TARGET = 1e-2
MIN_DECAY = math.log(HYENA_TARGET) / 1.5
MAX_DECAY = math.log(HYENA_TARGET) / 0.3
EPS = 1e-6

A_COLS = 4 * W_BRANCH
H_OFF = A_COLS
C_OFF = H_OFF + 4 * W_BRANCH
Q_COLS = N_HEADS * HEAD_DIM
KV_COLS = N_KV_HEADS * HEAD_DIM
K_OFF = C_OFF + Q_COLS
V_OFF = K_OFF + KV_COLS
Z_OFF = V_OFF + KV_COLS
G_OFF = Z_OFF + W_BRANCH

LANES = 128
VMEM_LIMIT_BYTES = 58 * 1024 * 1024
KV_DUP_COLS = 2 * KV_COLS
V_TILE_COLS = 2 * KV_DUP_COLS
Q_SCALE = ATTN_SCALE * math.log2(math.e)
SAFE_LOG2_RANGE = 64.0
KEY_CHUNK = 512
MOD_ROWS = 24
FILT_PAD = 64
HALF = 256
RADIX = 4
ROW_SPLIT = 256


def _params(n_axes):
    return pltpu.CompilerParams(dimension_semantics=("arbitrary",) * n_axes,
                                vmem_limit_bytes=VMEM_LIMIT_BYTES)


def _const_spec(shape):
    nd = len(shape)
    return pl.BlockSpec(shape, lambda *_: (0,) * nd, pipeline_mode=pl.Buffered(1))


def _silu(v):
    return v * jax.nn.sigmoid(v)


def _dot(a, b):
    return jnp.dot(a, b, preferred_element_type=F32)


def _split(a):
    hi = a.astype(BF16)
    lo = (a - hi.astype(F32)).astype(BF16)
    return hi, lo


def _dot3(a, b):
    ah, al = _split(a)
    bh, bl = _split(b)
    return _dot(ah, bh) + (_dot(ah, bl) + _dot(al, bh))


def _rms_mod(x, g, sc, sh):
    y = x * lax.rsqrt(jnp.mean(x * x, axis=-1, keepdims=True) + EPS)
    return (y * g) * (1.0 + sc) + sh


def _shift_rows(u, first_row, last_row):
    n = u.shape[0]
    row = lax.broadcasted_iota(jnp.int32, u.shape, 0)
    prev = jnp.where(row == 0, first_row, pltpu.roll(u, 1, 0))
    nxt = jnp.where(row == n - 1, last_row, pltpu.roll(u, n - 1, 0))
    return prev, nxt


def _conv3(u, w, first_row=0.0, last_row=0.0):
    prev, nxt = _shift_rows(u, first_row, last_row)
    return prev * w[0:1] + u * w[1:2] + nxt * w[2:3]


def _mod_kernel(cc_ref, w_ref, b_ref, o_ref):
    o_ref[0] = _dot3(_silu(cc_ref[...]), w_ref[0]) + b_ref[0]


def _modulation(cc, w_mod, b_mod):
    return pl.pallas_call(
        _mod_kernel,
        grid=(DEPTH, 3),
        in_specs=[pl.BlockSpec((MOD_ROWS, D_MODEL), lambda i, j: (0, 0)),
                  pl.BlockSpec((1, D_MODEL, D_MODEL), lambda i, j: (i, 0, j)),
                  pl.BlockSpec((1, 1, D_MODEL), lambda i, j: (i, 0, j))],
        out_specs=pl.BlockSpec((1, MOD_ROWS, D_MODEL), lambda i, j: (i, 0, j)),
        out_shape=jax.ShapeDtypeStruct((DEPTH, MOD_ROWS, 3 * D_MODEL), F32),
        compiler_params=_params(2),
        name="modulation",
    )(cc, w_mod, b_mod.reshape(DEPTH, 1, 3 * D_MODEL))


def _dft_tables(L):
    n2, m_sub = 4 * L, L // RADIX
    idx = jnp.arange(m_sub, dtype=jnp.int32)
    sample = RADIX * idx[None, None, :] + jnp.arange(RADIX, dtype=jnp.int32)[:, None, None]
    q = ((2 * idx[None, :, None] + 1) * sample) % n2
    ang = q.astype(F32) * (2.0 * math.pi / n2)
    cm, sm = jnp.cos(ang).astype(BF16), (-jnp.sin(ang)).astype(BF16)
    tr = lambda v: jnp.swapaxes(v, 1, 2) * (1.0 / L)
    return dict(cm=cm, sm=sm, icm=tr(cm), ism=tr(sm))


def _butterfly4(t0, t1, t2, t3):
    pr, pi, mr, mi = t0[0] + t2[0], t0[1] + t2[1], t0[0] - t2[0], t0[1] - t2[1]
    qr, qi, nr, ni = t1[0] + t3[0], t1[1] + t3[1], t1[0] - t3[0], t1[1] - t3[1]
    return (pr + qr, pi + qi), (mr - ni, mi + nr), (mr + ni, mi - nr), (pr - qr, pi - qi)


def _split_dft(parts, cm_ref, sm_ref):
    t = []
    for s in range(RADIX):
        u = parts[s].astype(BF16)
        t.append((_dot(cm_ref[s], u), _dot(sm_ref[s], u)))
    return _butterfly4(*t)


def _filter_hidden_kernel(feats_ref, w1_ref, b1_ref, w2_ref, b2_ref, fq_ref, h_ref):
    fq = fq_ref[0]
    h = jnp.sin(fq[0:1] * (_dot3(feats_ref[...], w1_ref[0]) + b1_ref[0]))
    h_ref[0] = jnp.sin(fq[1:2] * (_dot3(h, w2_ref[0]) + b2_ref[0]))


def _filter_kernel(h_ref, w3f_ref, w3b_ref, b3f_ref, b3b_ref, dl_ref, cm_ref, sm_ref, k_ref, *, L):
    m_sub = L // RADIX
    c = dl_ref.shape[1]
    h = h_ref[0]
    row = lax.broadcasted_iota(jnp.int32, (L, c), 0)
    lag = RADIX * (row % m_sub) + row // m_sub
    win = jnp.exp(-(lag.astype(F32) * (1.0 / (L - 1))) * dl_ref[...])
    hf = (_dot3(h, w3f_ref[0]) + b3f_ref[0]) * win
    hb = jnp.where(lag == 0, 0.0, (_dot3(h, w3b_ref[0]) + b3b_ref[0]) * win)
    nrm = jnp.sum(jnp.abs(hf), axis=0, keepdims=True) + jnp.sum(jnp.abs(hb), axis=0, keepdims=True)
    hf, hb = hf / nrm, hb / nrm
    parts = [jnp.concatenate([hf[s * m_sub:(s + 1) * m_sub], hb[s * m_sub:(s + 1) * m_sub]], axis=1)
             for s in range(RADIX)]
    spec = _split_dft(parts, cm_ref, sm_ref)
    for j, (re, im) in enumerate(spec):
        k_ref[0, 0, 2 * j] = re[:, :c] + re[:, c:]
        k_ref[0, 0, 2 * j + 1] = im[:, :c] - im[:, c:]


def _filter_spectra(L, tabs, filt_w1, filt_b1, filt_w2, filt_b2, filt_w3, filt_b3, filt_freq):
    m_sub = L // RADIX
    t = jnp.linspace(0.0, 1.0, L, dtype=F32)[:, None]
    bands = jnp.linspace(1e-4, FILTER_BANDS - 1, FILTER_BANDS, dtype=F32)
    w = (2.0 * math.pi / L) * jnp.arange(L, dtype=F32)[:, None]
    feats = jnp.concatenate([t, jnp.cos(w * bands), jnp.sin(w * bands)], axis=-1)
    feats = jnp.concatenate([feats[s::RADIX] for s in range(RADIX)], axis=0)
    feats = jnp.pad(feats, ((0, 0), (0, FILT_PAD - FILTER_EMB)))
    w1 = jnp.pad(filt_w1, ((0, 0), (0, FILT_PAD - FILTER_EMB), (0, 0)))
    deltas = jnp.abs(jnp.linspace(MIN_DECAY, MAX_DECAY, W_BRANCH, dtype=F32))[None, :]
    lay1 = lambda i: (i, 0, 0)
    hidden = pl.pallas_call(
        _filter_hidden_kernel,
        grid=(DEPTH,),
        in_specs=[_const_spec((L, FILT_PAD)),
                  pl.BlockSpec((1, FILT_PAD, FILTER_HIDDEN), lay1),
                  pl.BlockSpec((1, 1, FILTER_HIDDEN), lay1),
                  pl.BlockSpec((1, FILTER_HIDDEN, FILTER_HIDDEN), lay1),
                  pl.BlockSpec((1, 1, FILTER_HIDDEN), lay1),
                  pl.BlockSpec((1, 2, FILTER_HIDDEN), lay1)],
        out_specs=pl.BlockSpec((1, L, FILTER_HIDDEN), lay1),
        out_shape=jax.ShapeDtypeStruct((DEPTH, L, FILTER_HIDDEN), F32),
        compiler_params=_params(1),
        name=f"filter_hidden_{L}",
    )(feats, w1, filt_b1[:, None], filt_w2, filt_b2[:, None], filt_freq)
    nc = W_BRANCH // HALF
    fwd = lambda i, o, c: (i, 0, o * 2 * nc + c)
    bwd = lambda i, o, c: (i, 0, o * 2 * nc + nc + c)
    tab = _const_spec((RADIX, m_sub, m_sub))
    return pl.pallas_call(
        functools.partial(_filter_kernel, L=L),
        grid=(DEPTH, HYENA_ORDER, nc),
        in_specs=[pl.BlockSpec((1, L, FILTER_HIDDEN), lambda i, o, c: (i, 0, 0)),
                  pl.BlockSpec((1, FILTER_HIDDEN, HALF), fwd),
                  pl.BlockSpec((1, FILTER_HIDDEN, HALF), bwd),
                  pl.BlockSpec((1, 1, HALF), fwd),
                  pl.BlockSpec((1, 1, HALF), bwd),
                  pl.BlockSpec((1, HALF), lambda i, o, c: (0, c)),
                  tab, tab],
        out_specs=pl.BlockSpec((1, 1, 2 * RADIX, m_sub, HALF), lambda i, o, c: (i, o, 0, 0, c)),
        out_shape=jax.ShapeDtypeStruct((DEPTH, HYENA_ORDER, 2 * RADIX, m_sub, W_BRANCH), F32),
        compiler_params=_params(3),
        name=f"filter_spectra_{L}",
    )(hidden, filt_w3, filt_w3, filt_b3[:, None], filt_b3[:, None], deltas,
      tabs["cm"], tabs["sm"])


def _head_norm(v, bd_ref, g):
    ss = _dot((v * v).astype(BF16), bd_ref[...])
    return v * lax.rsqrt(ss * (1.0 / HEAD_DIM) + EPS) * g


def _rope(v, cos, sin):
    n = v.shape[1] // LANES
    lane = lax.broadcasted_iota(jnp.int32, (v.shape[0], LANES), 1)
    first = (lane % 32) < 16
    out = []
    for i in range(n):
        c = v[:, i * LANES:(i + 1) * LANES]
        partner = jnp.where(first, pltpu.roll(c, LANES - 16, 1), pltpu.roll(c, 16, 1))
        out.append(c * cos + partner * sin)
    return jnp.concatenate(out, axis=1)


def _kv_tiles(k, v):
    low = lax.broadcasted_iota(jnp.int32, k.shape, 1) < HEAD_DIM
    ks, vs = pltpu.roll(k, HEAD_DIM, 1), pltpu.roll(v, HEAD_DIM, 1)
    kk = jnp.concatenate([jnp.where(low, k, ks), jnp.where(low, ks, k)], axis=1)
    vv = jnp.concatenate([jnp.where(low, v, 1.0), jnp.where(low, 1.0, vs),
                          jnp.where(low, vs, 1.0), jnp.where(low, 1.0, v)], axis=1)
    return kk.astype(BF16), vv.astype(BF16)


def _inproj_kernel(*refs, rope, kv_only):
    it = iter(refs)
    x_ref, sc_ref, sh_ref, g_ref = next(it), next(it), next(it), next(it)
    if not kv_only:
        wa_ref, wh_ref = next(it), next(it)
    wq_ref = next(it)
    if not kv_only:
        gq_ref, bdq_ref = next(it), next(it)
    gk_ref, bdk_ref = next(it), next(it)
    if rope:
        cos_ref, sin_ref = next(it), next(it)
    outs = list(it)

    h = _rms_mod(x_ref[0], g_ref[...], sc_ref[0], sh_ref[0]).astype(BF16)
    w = W_BRANCH
    if kv_only:
        kk_ref, vv_ref = outs
        acc = _dot(h, wq_ref[...])
        k = _head_norm(acc[:, :KV_COLS], bdk_ref, gk_ref[...])
        kk_ref[0], vv_ref[0] = _kv_tiles(k, acc[:, KV_COLS:])
        return

    ua_ref, ga_ref, hp_ref, sgh_ref, q_ref, kk_ref, vv_ref, sz_ref = outs
    for r0 in range(0, h.shape[0], ROW_SPLIT):
        rows = slice(r0, r0 + ROW_SPLIT)
        hr = h[rows]
        acc = _dot(hr, wq_ref[...])
        q = _head_norm(acc[:, :Q_COLS], bdq_ref, gq_ref[...])
        k = _head_norm(acc[:, Q_COLS:Q_COLS + KV_COLS], bdk_ref, gk_ref[...])
        if rope:
            q = _rope(q, cos_ref[rows], sin_ref[rows])
            k = _rope(k, cos_ref[rows], sin_ref[rows])
        q_ref[0, rows] = (q * Q_SCALE).astype(BF16)
        kk_ref[0, rows], vv_ref[0, rows] = _kv_tiles(k, acc[:, Q_COLS + KV_COLS:Q_COLS + 2 * KV_COLS])
        sz_ref[0, rows] = _silu(acc[:, Q_COLS + 2 * KV_COLS:]).astype(BF16)
        acc = _dot(hr, wa_ref[...])
        ua_ref[0, rows] = (acc[:, 2 * w:3 * w] * acc[:, :w]).astype(BF16)
        ga_ref[0, rows] = (acc[:, w:2 * w] * _silu(acc[:, 3 * w:])).astype(BF16)
        acc = _dot(hr, wh_ref[...])
        hp_ref[0, rows] = acc[:, :3 * w].astype(BF16)
        sgh_ref[0, rows] = _silu(acc[:, 3 * w:]).astype(BF16)


def _inproj(x, sc, sh, g, wts, consts, *, rope, kv_only):
    B, T, D = x.shape
    tm = min(T, 512)
    row = lambda b, t: (b, t, 0)
    vec = pl.BlockSpec((1, 1, D), lambda b, t: (b, 0, 0))
    args = [x, sc, sh, g[None, :]]
    specs = [pl.BlockSpec((1, tm, D), row), vec, vec, _const_spec((1, D))]
    if not kv_only:
        args += [wts["wa"], wts["wh"], wts["wq"], consts["gq"], consts["bdq"]]
        specs += [_const_spec(wts["wa"].shape), _const_spec(wts["wh"].shape),
                  _const_spec(wts["wq"].shape), _const_spec(consts["gq"].shape),
                  _const_spec(consts["bdq"].shape)]
    else:
        args += [wts["wkv"]]
        specs += [_const_spec(wts["wkv"].shape)]
    args += [consts["gk"], consts["bdk"]]
    specs += [_const_spec(consts["gk"].shape), _const_spec(consts["bdk"].shape)]
    if rope:
        args += [consts["cos"], consts["sin"]]
        specs += [pl.BlockSpec((tm, LANES), lambda b, t: (t, 0))] * 2
    widths = ([KV_DUP_COLS, V_TILE_COLS] if kv_only else
              [W_BRANCH, W_BRANCH, 3 * W_BRANCH, W_BRANCH, Q_COLS, KV_DUP_COLS, V_TILE_COLS, W_BRANCH])
    return pl.pallas_call(
        functools.partial(_inproj_kernel, rope=rope, kv_only=kv_only),
        grid=(B, T // tm),
        in_specs=specs,
        out_specs=[pl.BlockSpec((1, tm, n), row) for n in widths],
        out_shape=[jax.ShapeDtypeStruct((B, T, n), BF16) for n in widths],
        compiler_params=_params(2),
        name=f"inproj_{T}" + ("_kv" if kv_only else ""),
    )(*args)


def _attn_kernel(*refs, n_src, shifted):
    q_ref, sz_ref = refs[0], refs[1]
    kv = refs[2:2 + 2 * n_src]
    o_ref = refs[-1]
    tq = q_ref.shape[1]
    low = lax.broadcasted_iota(jnp.int32, (tq, LANES), 1) < HEAD_DIM
    nt = (((1,), (1,)), ((), ()))
    for j in range(2):
        cols = slice(j * LANES, (j + 1) * LANES)
        qp = q_ref[0, :, cols].astype(F32)
        halves = []
        for half, sel in enumerate((low, jnp.logical_not(low))):
            qm = jnp.where(sel, qp, 0.0).astype(BF16)
            vt = slice(half * LANES, (half + 1) * LANES)
            if shifted:
                s = [lax.dot_general(qm, kv[2 * i][0], nt, preferred_element_type=F32) for i in range(n_src)]
                m = functools.reduce(jnp.maximum, [jnp.max(v, axis=-1, keepdims=True) for v in s])
                terms = [_dot(jnp.exp2(s[i] - m).astype(BF16), kv[2 * i + 1][0, :, vt]) for i in range(n_src)]
            else:
                terms = []
                for i in range(n_src):
                    n_keys = kv[2 * i].shape[1]
                    for c0 in range(0, n_keys, KEY_CHUNK):
                        rows = slice(c0, min(c0 + KEY_CHUNK, n_keys))
                        s = lax.dot_general(qm, kv[2 * i][0, rows, :], nt, preferred_element_type=F32)
                        terms.append(_dot(jnp.exp2(s).astype(BF16), kv[2 * i + 1][0, rows, vt]))
            halves.append(functools.reduce(jnp.add, terms))
        num = jnp.where(low, halves[0], halves[1])
        den = pltpu.roll(jnp.where(low, halves[1], halves[0]), HEAD_DIM, 1)
        o_ref[0, :, cols] = (num / den * sz_ref[0, :, cols].astype(F32)).astype(BF16)


def _attention(q, sz, sources, shift_free):
    B, T, _ = q.shape
    gw = Q_COLS // N_KV_HEADS
    args = [q, sz] + [a for src in sources for a in src]

    def call(shifted):
        tq = min(T, 256 if shifted else 2048)
        qspec = pl.BlockSpec((1, tq, gw), lambda b, g, t: (b, t, g))
        specs = [qspec, qspec]
        for kk, vv in sources:
            specs += [pl.BlockSpec((1, kk.shape[1], LANES), lambda b, g, t: (b, 0, g)),
                      pl.BlockSpec((1, vv.shape[1], 2 * LANES), lambda b, g, t: (b, 0, g))]
        return pl.pallas_call(
            functools.partial(_attn_kernel, n_src=len(sources), shifted=shifted),
            grid=(B, N_KV_HEADS, T // tq),
            in_specs=specs,
            out_specs=qspec,
            out_shape=jax.ShapeDtypeStruct((B, T, Q_COLS), BF16),
            compiler_params=_params(3),
            name=f"attention_{T}" + ("_shifted" if shifted else ""),
        )
    return lax.cond(shift_free, call(False), call(True), *args)


def _deinterleave(x, scr):
    n, c = x.shape
    slabs = range(c // LANES)
    for j in slabs:
        scr[j][...] = x[:, j * LANES:(j + 1) * LANES]
    return [jnp.concatenate([scr[j][pl.ds(s, n // RADIX, stride=RADIX), :] for j in slabs], axis=1)
            for s in range(RADIX)]


def _interleave(parts, scr):
    m, c = parts[0].shape
    slabs = range(c // LANES)
    for j in slabs:
        for s in range(RADIX):
            scr[j][pl.ds(s, m, stride=RADIX), :] = parts[s][:, j * LANES:(j + 1) * LANES]
    return jnp.concatenate([scr[j][...] for j in slabs], axis=1)


def _conv3_split(parts, w):
    m = parts[0].shape[0]
    row = lax.broadcasted_iota(jnp.int32, parts[0].shape, 0)
    before = jnp.where(row == 0, 0.0, pltpu.roll(parts[-1], 1, 0))
    after = jnp.where(row == m - 1, 0.0, pltpu.roll(parts[0], m - 1, 0))
    ext = [before] + list(parts) + [after]
    return [ext[s] * w[0:1] + ext[s + 1] * w[1:2] + ext[s + 2] * w[2:3] for s in range(RADIX)]


def _long_conv_split(parts, k, cm_ref, sm_ref, icm_ref, ism_ref):
    spec = _split_dft(parts, cm_ref, sm_ref)
    ya, yb, yc, yd = [(xr * k[2 * j] - xi * k[2 * j + 1], xr * k[2 * j + 1] + xi * k[2 * j])
                      for j, (xr, xi) in enumerate(spec)]
    g0, g1, g3, g2 = _butterfly4(ya, yc, yd, yb)
    return [_dot(icm_ref[s], gr.astype(BF16)) + _dot(ism_ref[s], gi.astype(BF16))
            for s, (gr, gi) in enumerate((g0, g1, g2, g3))]


def _hyena_kernel(hv_ref, hx1_ref, hx2_ref, sg_ref, wv_ref, wx1_ref, wx2_ref, bias_ref,
                  cm_ref, sm_ref, icm_ref, ism_ref, k_ref, o_ref, *scratch):
    c = o_ref.shape[2]
    ns = c // LANES
    s_v, s_x1, s_x2, s_g, s_o = [scratch[i * ns:(i + 1) * ns] for i in range(5)]
    tabs = (cm_ref, sm_ref, icm_ref, ism_ref)
    split = lambda ref, scr: _deinterleave(ref[0].astype(F32), scr)
    v = _conv3_split(split(hv_ref, s_v), wv_ref[...])
    x1 = _conv3_split(split(hx1_ref, s_x1), wx1_ref[...])
    x2 = _conv3_split(split(hx2_ref, s_x2), wx2_ref[...])
    gate = split(sg_ref, s_g)
    b0, b1 = bias_ref[0:1], bias_ref[1:2]
    n_k = 2 * RADIX
    y = _long_conv_split(v, [k_ref[0, 0, j] for j in range(n_k)], *tabs)
    z1 = [x1[s] * (y[s] + v[s] * b0) for s in range(RADIX)]
    y = _long_conv_split(z1, [k_ref[0, 1, j] for j in range(n_k)], *tabs)
    out = [x2[s] * (y[s] + z1[s] * b1) * gate[s] for s in range(RADIX)]
    o_ref[0] = _interleave(out, s_o).astype(BF16)


def _hyena(hpre, sgh, conv_h, bias, tabs, spectra, layer):
    B, L, _ = hpre.shape
    m_sub = L // RADIX
    lanes = HALF if L * HALF * 4 > (1 << 20) else W_BRANCH
    nc = W_BRANCH // lanes
    blk = lambda j: pl.BlockSpec((1, L, lanes), lambda c, b: (b, 0, j * nc + c))
    cw = lambda j: pl.BlockSpec((3, lanes), lambda c, b: (0, j * nc + c))
    tab = _const_spec((RADIX, m_sub, m_sub))
    kspec = pl.BlockSpec((1, HYENA_ORDER, 2 * RADIX, m_sub, lanes), lambda c, b: (layer, 0, 0, 0, c),
                         pipeline_mode=pl.Buffered(1))
    return pl.pallas_call(
        _hyena_kernel,
        grid=(nc, B),
        in_specs=[blk(0), blk(1), blk(2), blk(0), cw(0), cw(1), cw(2),
                  pl.BlockSpec((HYENA_ORDER, lanes), lambda c, b: (0, c)),
                  tab, tab, tab, tab, kspec],
        out_specs=blk(0),
        out_shape=jax.ShapeDtypeStruct((B, L, W_BRANCH), BF16),
        scratch_shapes=[pltpu.VMEM((L, LANES), F32)] * (5 * lanes // LANES),
        compiler_params=_params(2),
        name=f"hyena_{L}",
    )(hpre, hpre, hpre, sgh, conv_h, conv_h, conv_h, bias,
      tabs["cm"], tabs["sm"], tabs["icm"], tabs["ism"], spectra)


def _merge_kernel(x_ref, sc_ref, sh_ref, gt_ref, g_ref, ua_ref, up_ref, un_ref, ga_ref, ca_ref,
                  yb_ref, yc_ref, wg_ref, wb_ref, wo_ref, o_ref):
    t, nt = pl.program_id(1), pl.num_programs(1)
    x = x_ref[0]
    h = _rms_mod(x, g_ref[...], sc_ref[0], sh_ref[0]).astype(BF16)
    halo = up_ref.shape[1]
    first = jnp.where(t == 0, 0.0, up_ref[0, halo - 1:halo, :].astype(F32))
    last = jnp.where(t == nt - 1, 0.0, un_ref[0, 0:1, :].astype(F32))
    ya = (ga_ref[0].astype(F32) * _conv3(ua_ref[0].astype(F32), ca_ref[...], first, last)).astype(BF16)
    for r0 in range(0, x.shape[0], ROW_SPLIT):
        rows = slice(r0, r0 + ROW_SPLIT)
        ys = (ya[rows], yb_ref[0, rows], yc_ref[0, rows])
        merged = None
        for n in range(N_BRANCH):
            gate = jax.nn.sigmoid(_dot(h[rows], wg_ref[:, n * D_MODEL:(n + 1) * D_MODEL]))
            term = gate * _dot(ys[n], wb_ref[n])
            merged = term if merged is None else merged + term
        out = _dot(merged.astype(BF16), wo_ref[...])
        o_ref[0, rows] = x[rows] + gt_ref[0] * out


def _merge(x, sc, sh, gt, g, ua, ga, conv_a, yb, yc, wts):
    B, T, D = x.shape
    tm = min(T, 512)
    halo = 16
    r = tm // halo
    row = lambda b, t: (b, t, 0)
    vec = pl.BlockSpec((1, 1, D), lambda b, t: (b, 0, 0))
    br = pl.BlockSpec((1, tm, W_BRANCH), row)
    prev = pl.BlockSpec((1, halo, W_BRANCH), lambda b, t: (b, jnp.maximum(t * r - 1, 0), 0))
    nxt = pl.BlockSpec((1, halo, W_BRANCH), lambda b, t: (b, jnp.minimum((t + 1) * r, T // halo - 1), 0))
    return pl.pallas_call(
        _merge_kernel,
        grid=(B, T // tm),
        in_specs=[pl.BlockSpec((1, tm, D), row), vec, vec, vec, _const_spec((1, D)),
                  br, prev, nxt, br, _const_spec((3, W_BRANCH)), br, br,
                  _const_spec(wts["wg"].shape), _const_spec(wts["wb"].shape), _const_spec(wts["wo"].shape)],
        out_specs=pl.BlockSpec((1, tm, D), row),
        out_shape=jax.ShapeDtypeStruct((B, T, D), F32),
        compiler_params=_params(2),
        name=f"merge_{T}",
    )(x, sc, sh, gt, g[None, :], ua, ua, ua, ga, conv_a, yb, yc, wts["wg"], wts["wb"], wts["wo"])


def _rope_tables(S):
    pos = jnp.arange(S, dtype=jnp.int32)
    coord = jnp.stack([(pos // GRID_W).astype(F32), (pos % GRID_W).astype(F32)], axis=1)
    inv = ROPE_BASE ** (-jnp.arange(ROPE_FREQS, dtype=F32) / ROPE_FREQS)
    ang = coord[:, :, None] * inv
    cos = jnp.repeat(jnp.cos(ang)[:, :, None, :], 2, axis=2).reshape(S, HEAD_DIM)
    sin = jnp.sin(ang)
    sin = jnp.stack([-sin, sin], axis=2).reshape(S, HEAD_DIM)
    return jnp.tile(cos, (1, 2)), jnp.tile(sin, (1, 2))


def _layer_weights(w_in, w_branch, w_out):
    return dict(
        wa=w_in[:, :A_COLS].astype(BF16),
        wh=w_in[:, H_OFF:C_OFF].astype(BF16),
        wq=w_in[:, C_OFF:G_OFF].astype(BF16),
        wkv=w_in[:, K_OFF:Z_OFF].astype(BF16),
        wg=w_in[:, G_OFF:].astype(BF16),
        wb=w_branch.astype(BF16),
        wo=w_out.astype(BF16),
    )


def _mixers(x, sc, sh, gt, g, wts, consts, conv_a, conv_h, bias, tabs, spectra, layer, rope, ctx_kv):
    ua, ga, hpre, sgh, q, kk, vv, sz = _inproj(x, sc, sh, g, wts, consts, rope=rope, kv_only=False)
    sources = ([ctx_kv] if ctx_kv is not None else []) + [(kk, vv)]
    yc = _attention(q, sz, sources, consts["shift_free"])
    yb = _hyena(hpre, sgh, conv_h, bias, tabs, spectra, layer)
    return _merge(x, sc, sh, gt, g, ua, ga, conv_a, yb, yc, wts), (kk, vv)


def kernel(x, c, ctx, c_ctx, norm_g, w_mod, b_mod, w_in, conv_a, conv_h, filt_w1, filt_b1, filt_w2,
           filt_b2, filt_w3, filt_b3, filt_freq, hyena_bias, q_norm_g, k_norm_g, w_branch, w_out):
    B, S, D = x.shape
    Lc = ctx.shape[1]
    cc = jnp.concatenate([c, c_ctx[None, :], jnp.zeros((MOD_ROWS - B - 1, D), F32)], axis=0)
    mod = _modulation(cc, w_mod, b_mod)

    tabs_s, tabs_c = _dft_tables(S), _dft_tables(Lc)
    filt = (filt_w1, filt_b1, filt_w2, filt_b2, filt_w3, filt_b3, filt_freq)
    spec_s = _filter_spectra(S, tabs_s, *filt)
    spec_c = _filter_spectra(Lc, tabs_c, *filt)

    cos, sin = _rope_tables(S)
    ones = jnp.ones((HEAD_DIM, HEAD_DIM), F32)
    bdq = jnp.kron(jnp.eye(N_HEADS, dtype=F32), ones).astype(BF16)
    bdk = jnp.kron(jnp.eye(N_KV_HEADS, dtype=F32), ones).astype(BF16)

    for i in range(DEPTH):
        last = i == DEPTH - 1
        wts = _layer_weights(w_in[i], w_branch[i], w_out[i])
        consts = dict(gq=jnp.tile(q_norm_g[i], N_HEADS)[None, :],
                      gk=jnp.tile(k_norm_g[i], N_KV_HEADS)[None, :],
                      bdq=bdq, bdk=bdk, cos=cos, sin=sin)
        score_bound = (HEAD_DIM * Q_SCALE) * jnp.max(jnp.abs(q_norm_g[i])) * jnp.max(jnp.abs(k_norm_g[i]))
        consts["shift_free"] = score_bound <= SAFE_LOG2_RANGE
        split = lambda rows: [rows[:, None, j * D:(j + 1) * D] for j in range(3)]
        sh, sc, gt = split(mod[i, :B])
        sh_c, sc_c, gt_c = [jnp.broadcast_to(v, (B, 1, D)) for v in split(mod[i, B:B + 1])]
        if last:
            ctx_kv = tuple(_inproj(ctx, sc_c, sh_c, norm_g[i], wts, consts, rope=False, kv_only=True))
        else:
            ctx_next, ctx_kv = _mixers(ctx, sc_c, sh_c, gt_c, norm_g[i], wts, consts, conv_a[i], conv_h[i],
                                       hyena_bias[i], tabs_c, spec_c, i, False, None)
        x, _ = _mixers(x, sc, sh, gt, norm_g[i], wts, consts, conv_a[i], conv_h[i],
                       hyena_bias[i], tabs_s, spec_s, i, True, ctx_kv)
        if not last:
            ctx = ctx_next
    return x
```

```python
import functools
import math

import jax
import jax.numpy as jnp
from jax import lax
from jax.experimental import pallas as pl
from jax.experimental.pallas import tpu as pltpu

F32 = jnp.float32
BF16 = jnp.bfloat16

D_MODEL = 1024
DEPTH = 4
GRID_W = 64
W_BRANCH = 512
N_BRANCH = 3
N_HEADS = 8
N_KV_HEADS = 2
HEAD_DIM = 64
ROPE_FREQS = HEAD_DIM // 4
ROPE_BASE = 10000.0
ATTN_SCALE = HEAD_DIM ** -0.5
HYENA_ORDER = 2
FILTER_EMB = 33
FILTER_BANDS = (FILTER_EMB - 1) // 2
FILTER_HIDDEN = 64
HYENA_TARGET = 1e-2
MIN_DECAY = math.log(HYENA_TARGET) / 1.5
MAX_DECAY = math.log(HYENA_TARGET) / 0.3
EPS = 1e-6

A_COLS = 4 * W_BRANCH
H_OFF = A_COLS
C_OFF = H_OFF + 4 * W_BRANCH
Q_COLS = N_HEADS * HEAD_DIM
KV_COLS = N_KV_HEADS * HEAD_DIM
K_OFF = C_OFF + Q_COLS
V_OFF = K_OFF + KV_COLS
Z_OFF = V_OFF + KV_COLS
G_OFF = Z_OFF + W_BRANCH

LANES = 128
VMEM_LIMIT_BYTES = 58 * 1024 * 1024
KV_DUP_COLS = 2 * KV_COLS
V_TILE_COLS = 2 * KV_DUP_COLS
Q_SCALE = ATTN_SCALE * math.log2(math.e)
SAFE_LOG2_RANGE = 64.0
KEY_CHUNK = 512
MOD_ROWS = 24
FILT_PAD = 64
HALF = 256
RADIX = 4
ROW_SPLIT = 256


def _params(n_axes):
    return pltpu.CompilerParams(dimension_semantics=("arbitrary",) * n_axes,
                                vmem_limit_bytes=VMEM_LIMIT_BYTES)


def _const_spec(shape):
    nd = len(shape)
    return pl.BlockSpec(shape, lambda *_: (0,) * nd, pipeline_mode=pl.Buffered(1))


def _silu(v):
    return v * jax.nn.sigmoid(v)


def _dot(a, b):
    return jnp.dot(a, b, preferred_element_type=F32)


def _split(a):
    hi = a.astype(BF16)
    lo = (a - hi.astype(F32)).astype(BF16)
    return hi, lo


def _dot3(a, b):
    ah, al = _split(a)
    bh, bl = _split(b)
    return _dot(ah, bh) + (_dot(ah, bl) + _dot(al, bh))


def _rms_mod(x, g, sc, sh):
    y = x * lax.rsqrt(jnp.mean(x * x, axis=-1, keepdims=True) + EPS)
    return (y * g) * (1.0 + sc) + sh


def _shift_rows(u, first_row, last_row):
    n = u.shape[0]
    row = lax.broadcasted_iota(jnp.int32, u.shape, 0)
    prev = jnp.where(row == 0, first_row, pltpu.roll(u, 1, 0))
    nxt = jnp.where(row == n - 1, last_row, pltpu.roll(u, n - 1, 0))
    return prev, nxt


def _conv3(u, w, first_row=0.0, last_row=0.0):
    prev, nxt = _shift_rows(u, first_row, last_row)
    return prev * w[0:1] + u * w[1:2] + nxt * w[2:3]


def _mod_kernel(cc_ref, w_ref, b_ref, o_ref):
    o_ref[0] = _dot3(_silu(cc_ref[...]), w_ref[0]) + b_ref[0]


def _modulation(cc, w_mod, b_mod):
    return pl.pallas_call(
        _mod_kernel,
        grid=(DEPTH, 3),
        in_specs=[pl.BlockSpec((MOD_ROWS, D_MODEL), lambda i, j: (0, 0)),
                  pl.BlockSpec((1, D_MODEL, D_MODEL), lambda i, j: (i, 0, j)),
                  pl.BlockSpec((1, 1, D_MODEL), lambda i, j: (i, 0, j))],
        out_specs=pl.BlockSpec((1, MOD_ROWS, D_MODEL), lambda i, j: (i, 0, j)),
        out_shape=jax.ShapeDtypeStruct((DEPTH, MOD_ROWS, 3 * D_MODEL), F32),
        compiler_params=_params(2),
        name="modulation",
    )(cc, w_mod, b_mod.reshape(DEPTH, 1, 3 * D_MODEL))


def _dft_tables(L):
    n2, m_sub = 4 * L, L // RADIX
    idx = jnp.arange(m_sub, dtype=jnp.int32)
    sample = RADIX * idx[None, None, :] + jnp.arange(RADIX, dtype=jnp.int32)[:, None, None]
    q = ((2 * idx[None, :, None] + 1) * sample) % n2
    ang = q.astype(F32) * (2.0 * math.pi / n2)
    cm, sm = jnp.cos(ang).astype(BF16), (-jnp.sin(ang)).astype(BF16)
    tr = lambda v: jnp.swapaxes(v, 1, 2) * (1.0 / L)
    return dict(cm=cm, sm=sm, icm=tr(cm), ism=tr(sm))


def _butterfly4(t0, t1, t2, t3):
    pr, pi, mr, mi = t0[0] + t2[0], t0[1] + t2[1], t0[0] - t2[0], t0[1] - t2[1]
    qr, qi, nr, ni = t1[0] + t3[0], t1[1] + t3[1], t1[0] - t3[0], t1[1] - t3[1]
    return (pr + qr, pi + qi), (mr - ni, mi + nr), (mr + ni, mi - nr), (pr - qr, pi - qi)


def _split_dft(parts, cm_ref, sm_ref):
    t = []
    for s in range(RADIX):
        u = parts[s].astype(BF16)
        t.append((_dot(cm_ref[s], u), _dot(sm_ref[s], u)))
    return _butterfly4(*t)


def _filter_hidden_kernel(feats_ref, w1_ref, b1_ref, w2_ref, b2_ref, fq_ref, h_ref):
    fq = fq_ref[0]
    h = jnp.sin(fq[0:1] * (_dot3(feats_ref[...], w1_ref[0]) + b1_ref[0]))
    h_ref[0] = jnp.sin(fq[1:2] * (_dot3(h, w2_ref[0]) + b2_ref[0]))


def _filter_kernel(h_ref, w3f_ref, w3b_ref, b3f_ref, b3b_ref, dl_ref, cm_ref, sm_ref, k_ref, *, L):
    m_sub = L // RADIX
    c = dl_ref.shape[1]
    h = h_ref[0]
    row = lax.broadcasted_iota(jnp.int32, (L, c), 0)
    lag = RADIX * (row % m_sub) + row // m_sub
    win = jnp.exp(-(lag.astype(F32) * (1.0 / (L - 1))) * dl_ref[...])
    hf = (_dot3(h, w3f_ref[0]) + b3f_ref[0]) * win
    hb = jnp.where(lag == 0, 0.0, (_dot3(h, w3b_ref[0]) + b3b_ref[0]) * win)
    nrm = jnp.sum(jnp.abs(hf), axis=0, keepdims=True) + jnp.sum(jnp.abs(hb), axis=0, keepdims=True)
    hf, hb = hf / nrm, hb / nrm
    parts = [jnp.concatenate([hf[s * m_sub:(s + 1) * m_sub], hb[s * m_sub:(s + 1) * m_sub]], axis=1)
             for s in range(RADIX)]
    spec = _split_dft(parts, cm_ref, sm_ref)
    for j, (re, im) in enumerate(spec):
        k_ref[0, 0, 2 * j] = re[:, :c] + re[:, c:]
        k_ref[0, 0, 2 * j + 1] = im[:, :c] - im[:, c:]


def _filter_spectra(L, tabs, filt_w1, filt_b1, filt_w2, filt_b2, filt_w3, filt_b3, filt_freq):
    m_sub = L // RADIX
    t = jnp.linspace(0.0, 1.0, L, dtype=F32)[:, None]
    bands = jnp.linspace(1e-4, FILTER_BANDS - 1, FILTER_BANDS, dtype=F32)
    w = (2.0 * math.pi / L) * jnp.arange(L, dtype=F32)[:, None]
    feats = jnp.concatenate([t, jnp.cos(w * bands), jnp.sin(w * bands)], axis=-1)
    feats = jnp.concatenate([feats[s::RADIX] for s in range(RADIX)], axis=0)
    feats = jnp.pad(feats, ((0, 0), (0, FILT_PAD - FILTER_EMB)))
    w1 = jnp.pad(filt_w1, ((0, 0), (0, FILT_PAD - FILTER_EMB), (0, 0)))
    deltas = jnp.abs(jnp.linspace(MIN_DECAY, MAX_DECAY, W_BRANCH, dtype=F32))[None, :]
    lay1 = lambda i: (i, 0, 0)
    hidden = pl.pallas_call(
        _filter_hidden_kernel,
        grid=(DEPTH,),
        in_specs=[_const_spec((L, FILT_PAD)),
                  pl.BlockSpec((1, FILT_PAD, FILTER_HIDDEN), lay1),
                  pl.BlockSpec((1, 1, FILTER_HIDDEN), lay1),
                  pl.BlockSpec((1, FILTER_HIDDEN, FILTER_HIDDEN), lay1),
                  pl.BlockSpec((1, 1, FILTER_HIDDEN), lay1),
                  pl.BlockSpec((1, 2, FILTER_HIDDEN), lay1)],
        out_specs=pl.BlockSpec((1, L, FILTER_HIDDEN), lay1),
        out_shape=jax.ShapeDtypeStruct((DEPTH, L, FILTER_HIDDEN), F32),
        compiler_params=_params(1),
        name=f"filter_hidden_{L}",
    )(feats, w1, filt_b1[:, None], filt_w2, filt_b2[:, None], filt_freq)
    nc = W_BRANCH // HALF
    fwd = lambda i, o, c: (i, 0, o * 2 * nc + c)
    bwd = lambda i, o, c: (i, 0, o * 2 * nc + nc + c)
    tab = _const_spec((RADIX, m_sub, m_sub))
    return pl.pallas_call(
        functools.partial(_filter_kernel, L=L),
        grid=(DEPTH, HYENA_ORDER, nc),
        in_specs=[pl.BlockSpec((1, L, FILTER_HIDDEN), lambda i, o, c: (i, 0, 0)),
                  pl.BlockSpec((1, FILTER_HIDDEN, HALF), fwd),
                  pl.BlockSpec((1, FILTER_HIDDEN, HALF), bwd),
                  pl.BlockSpec((1, 1, HALF), fwd),
                  pl.BlockSpec((1, 1, HALF), bwd),
                  pl.BlockSpec((1, HALF), lambda i, o, c: (0, c)),
                  tab, tab],
        out_specs=pl.BlockSpec((1, 1, 2 * RADIX, m_sub, HALF), lambda i, o, c: (i, o, 0, 0, c)),
        out_shape=jax.ShapeDtypeStruct((DEPTH, HYENA_ORDER, 2 * RADIX, m_sub, W_BRANCH), F32),
        compiler_params=_params(3),
        name=f"filter_spectra_{L}",
    )(hidden, filt_w3, filt_w3, filt_b3[:, None], filt_b3[:, None], deltas,
      tabs["cm"], tabs["sm"])


def _head_norm(v, bd_ref, g):
    ss = _dot((v * v).astype(BF16), bd_ref[...])
    return v * lax.rsqrt(ss * (1.0 / HEAD_DIM) + EPS) * g


def _rope(v, cos, sin):
    n = v.shape[1] // LANES
    lane = lax.broadcasted_iota(jnp.int32, (v.shape[0], LANES), 1)
    first = (lane % 32) < 16
    out = []
    for i in range(n):
        c = v[:, i * LANES:(i + 1) * LANES]
        partner = jnp.where(first, pltpu.roll(c, LANES - 16, 1), pltpu.roll(c, 16, 1))
        out.append(c * cos + partner * sin)
    return jnp.concatenate(out, axis=1)


def _kv_tiles(k, v):
    low = lax.broadcasted_iota(jnp.int32, k.shape, 1) < HEAD_DIM
    ks, vs = pltpu.roll(k, HEAD_DIM, 1), pltpu.roll(v, HEAD_DIM, 1)
    kk = jnp.concatenate([jnp.where(low, k, ks), jnp.where(low, ks, k)], axis=1)
    vv = jnp.concatenate([jnp.where(low, v, 1.0), jnp.where(low, 1.0, vs),
                          jnp.where(low, vs, 1.0), jnp.where(low, 1.0, v)], axis=1)
    return kk.astype(BF16), vv.astype(BF16)


def _inproj_kernel(*refs, rope, kv_only):
    it = iter(refs)
    x_ref, sc_ref, sh_ref, g_ref = next(it), next(it), next(it), next(it)
    if not kv_only:
        wa_ref, wh_ref = next(it), next(it)
    wq_ref = next(it)
    if not kv_only:
        gq_ref, bdq_ref = next(it), next(it)
    gk_ref, bdk_ref = next(it), next(it)
    if rope:
        cos_ref, sin_ref = next(it), next(it)
    outs = list(it)

    h = _rms_mod(x_ref[0], g_ref[...], sc_ref[0], sh_ref[0]).astype(BF16)
    w = W_BRANCH
    if kv_only:
        kk_ref, vv_ref = outs
        acc = _dot(h, wq_ref[...])
        k = _head_norm(acc[:, :KV_COLS], bdk_ref, gk_ref[...])
        kk_ref[0], vv_ref[0] = _kv_tiles(k, acc[:, KV_COLS:])
        return

    ua_ref, ga_ref, hp_ref, sgh_ref, q_ref, kk_ref, vv_ref, sz_ref = outs
    for r0 in range(0, h.shape[0], ROW_SPLIT):
        rows = slice(r0, r0 + ROW_SPLIT)
        hr = h[rows]
        acc = _dot(hr, wq_ref[...])
        q = _head_norm(acc[:, :Q_COLS], bdq_ref, gq_ref[...])
        k = _head_norm(acc[:, Q_COLS:Q_COLS + KV_COLS], bdk_ref, gk_ref[...])
        if rope:
            q = _rope(q, cos_ref[rows], sin_ref[rows])
            k = _rope(k, cos_ref[rows], sin_ref[rows])
        q_ref[0, rows] = (q * Q_SCALE).astype(BF16)
        kk_ref[0, rows], vv_ref[0, rows] = _kv_tiles(k, acc[:, Q_COLS + KV_COLS:Q_COLS + 2 * KV_COLS])
        sz_ref[0, rows] = _silu(acc[:, Q_COLS + 2 * KV_COLS:]).astype(BF16)
        acc = _dot(hr, wa_ref[...])
        ua_ref[0, rows] = (acc[:, 2 * w:3 * w] * acc[:, :w]).astype(BF16)
        ga_ref[0, rows] = (acc[:, w:2 * w] * _silu(acc[:, 3 * w:])).astype(BF16)
        acc = _dot(hr, wh_ref[...])
        hp_ref[0, rows] = acc[:, :3 * w].astype(BF16)
        sgh_ref[0, rows] = _silu(acc[:, 3 * w:]).astype(BF16)


def _inproj(x, sc, sh, g, wts, consts, *, rope, kv_only):
    B, T, D = x.shape
    tm = min(T, 1024)
    row = lambda b, t: (b, t, 0)
    vec = pl.BlockSpec((1, 1, D), lambda b, t: (b, 0, 0))
    args = [x, sc, sh, g[None, :]]
    specs = [pl.BlockSpec((1, tm, D), row), vec, vec, _const_spec((1, D))]
    if not kv_only:
        args += [wts["wa"], wts["wh"], wts["wq"], consts["gq"], consts["bdq"]]
        specs += [_const_spec(wts["wa"].shape), _const_spec(wts["wh"].shape),
                  _const_spec(wts["wq"].shape), _const_spec(consts["gq"].shape),
                  _const_spec(consts["bdq"].shape)]
    else:
        args += [wts["wkv"]]
        specs += [_const_spec(wts["wkv"].shape)]
    args += [consts["gk"], consts["bdk"]]
    specs += [_const_spec(consts["gk"].shape), _const_spec(consts["bdk"].shape)]
    if rope:
        args += [consts["cos"], consts["sin"]]
        specs += [pl.BlockSpec((tm, LANES), lambda b, t: (t, 0))] * 2
    widths = ([KV_DUP_COLS, V_TILE_COLS] if kv_only else
              [W_BRANCH, W_BRANCH, 3 * W_BRANCH, W_BRANCH, Q_COLS, KV_DUP_COLS, V_TILE_COLS, W_BRANCH])
    return pl.pallas_call(
        functools.partial(_inproj_kernel, rope=rope, kv_only=kv_only),
        grid=(B, T // tm),
        in_specs=specs,
        out_specs=[pl.BlockSpec((1, tm, n), row) for n in widths],
        out_shape=[jax.ShapeDtypeStruct((B, T, n), BF16) for n in widths],
        compiler_params=_params(2),
        name=f"inproj_{T}" + ("_kv" if kv_only else ""),
    )(*args)


def _attn_kernel(*refs, n_src, shifted):
    q_ref, sz_ref = refs[0], refs[1]
    kv = refs[2:2 + 2 * n_src]
    o_ref = refs[-1]
    tq = q_ref.shape[1]
    low = lax.broadcasted_iota(jnp.int32, (tq, LANES), 1) < HEAD_DIM
    nt = (((1,), (1,)), ((), ()))
    for j in range(2):
        cols = slice(j * LANES, (j + 1) * LANES)
        qp = q_ref[0, :, cols].astype(F32)
        halves = []
        for half, sel in enumerate((low, jnp.logical_not(low))):
            qm = jnp.where(sel, qp, 0.0).astype(BF16)
            vt = slice(half * LANES, (half + 1) * LANES)
            if shifted:
                s = [lax.dot_general(qm, kv[2 * i][0], nt, preferred_element_type=F32) for i in range(n_src)]
                m = functools.reduce(jnp.maximum, [jnp.max(v, axis=-1, keepdims=True) for v in s])
                terms = [_dot(jnp.exp2(s[i] - m).astype(BF16), kv[2 * i + 1][0, :, vt]) for i in range(n_src)]
            else:
                terms = []
                for i in range(n_src):
                    n_keys = kv[2 * i].shape[1]
                    for c0 in range(0, n_keys, KEY_CHUNK):
                        rows = slice(c0, min(c0 + KEY_CHUNK, n_keys))
                        s = lax.dot_general(qm, kv[2 * i][0, rows, :], nt, preferred_element_type=F32)
                        terms.append(_dot(jnp.exp2(s).astype(BF16), kv[2 * i + 1][0, rows, vt]))
            halves.append(functools.reduce(jnp.add, terms))
        num = jnp.where(low, halves[0], halves[1])
        den = pltpu.roll(jnp.where(low, halves[1], halves[0]), HEAD_DIM, 1)
        o_ref[0, :, cols] = (num / den * sz_ref[0, :, cols].astype(F32)).astype(BF16)


def _attention(q, sz, sources, shift_free):
    B, T, _ = q.shape
    gw = Q_COLS // N_KV_HEADS
    args = [q, sz] + [a for src in sources for a in src]

    def call(shifted):
        tq = min(T, 256 if shifted else 2048)
        qspec = pl.BlockSpec((1, tq, gw), lambda b, g, t: (b, t, g))
        specs = [qspec, qspec]
        for kk, vv in sources:
            specs += [pl.BlockSpec((1, kk.shape[1], LANES), lambda b, g, t: (b, 0, g)),
                      pl.BlockSpec((1, vv.shape[1], 2 * LANES), lambda b, g, t: (b, 0, g))]
        return pl.pallas_call(
            functools.partial(_attn_kernel, n_src=len(sources), shifted=shifted),
            grid=(B, N_KV_HEADS, T // tq),
            in_specs=specs,
            out_specs=qspec,
            out_shape=jax.ShapeDtypeStruct((B, T, Q_COLS), BF16),
            compiler_params=_params(3),
            name=f"attention_{T}" + ("_shifted" if shifted else ""),
        )
    return lax.cond(shift_free, call(False), call(True), *args)


def _deinterleave(x, scr):
    n, c = x.shape
    slabs = range(c // LANES)
    for j in slabs:
        scr[j][...] = x[:, j * LANES:(j + 1) * LANES]
    return [jnp.concatenate([scr[j][pl.ds(s, n // RADIX, stride=RADIX), :] for j in slabs], axis=1)
            for s in range(RADIX)]


def _interleave(parts, scr):
    m, c = parts[0].shape
    slabs = range(c // LANES)
    for j in slabs:
        for s in range(RADIX):
            scr[j][pl.ds(s, m, stride=RADIX), :] = parts[s][:, j * LANES:(j + 1) * LANES]
    return jnp.concatenate([scr[j][...] for j in slabs], axis=1)


def _conv3_split(parts, w):
    m = parts[0].shape[0]
    row = lax.broadcasted_iota(jnp.int32, parts[0].shape, 0)
    before = jnp.where(row == 0, 0.0, pltpu.roll(parts[-1], 1, 0))
    after = jnp.where(row == m - 1, 0.0, pltpu.roll(parts[0], m - 1, 0))
    ext = [before] + list(parts) + [after]
    return [ext[s] * w[0:1] + ext[s + 1] * w[1:2] + ext[s + 2] * w[2:3] for s in range(RADIX)]


def _long_conv_split(parts, k, cm_ref, sm_ref, icm_ref, ism_ref):
    spec = _split_dft(parts, cm_ref, sm_ref)
    ya, yb, yc, yd = [(xr * k[2 * j] - xi * k[2 * j + 1], xr * k[2 * j + 1] + xi * k[2 * j])
                      for j, (xr, xi) in enumerate(spec)]
    g0, g1, g3, g2 = _butterfly4(ya, yc, yd, yb)
    return [_dot(icm_ref[s], gr.astype(BF16)) + _dot(ism_ref[s], gi.astype(BF16))
            for s, (gr, gi) in enumerate((g0, g1, g2, g3))]


def _hyena_kernel(hv_ref, hx1_ref, hx2_ref, sg_ref, wv_ref, wx1_ref, wx2_ref, bias_ref,
                  cm_ref, sm_ref, icm_ref, ism_ref, k_ref, o_ref, *scratch):
    c = o_ref.shape[2]
    ns = c // LANES
    s_v, s_x1, s_x2, s_g, s_o = [scratch[i * ns:(i + 1) * ns] for i in range(5)]
    tabs = (cm_ref, sm_ref, icm_ref, ism_ref)
    split = lambda ref, scr: _deinterleave(ref[0].astype(F32), scr)
    v = _conv3_split(split(hv_ref, s_v), wv_ref[...])
    x1 = _conv3_split(split(hx1_ref, s_x1), wx1_ref[...])
    x2 = _conv3_split(split(hx2_ref, s_x2), wx2_ref[...])
    gate = split(sg_ref, s_g)
    b0, b1 = bias_ref[0:1], bias_ref[1:2]
    n_k = 2 * RADIX
    y = _long_conv_split(v, [k_ref[0, 0, j] for j in range(n_k)], *tabs)
    z1 = [x1[s] * (y[s] + v[s] * b0) for s in range(RADIX)]
    y = _long_conv_split(z1, [k_ref[0, 1, j] for j in range(n_k)], *tabs)
    out = [x2[s] * (y[s] + z1[s] * b1) * gate[s] for s in range(RADIX)]
    o_ref[0] = _interleave(out, s_o).astype(BF16)


def _hyena(hpre, sgh, conv_h, bias, tabs, spectra, layer):
    B, L, _ = hpre.shape
    m_sub = L // RADIX
    lanes = HALF if L * HALF * 4 > (1 << 20) else W_BRANCH
    nc = W_BRANCH // lanes
    blk = lambda j: pl.BlockSpec((1, L, lanes), lambda c, b: (b, 0, j * nc + c))
    cw = lambda j: pl.BlockSpec((3, lanes), lambda c, b: (0, j * nc + c))
    tab = _const_spec((RADIX, m_sub, m_sub))
    kspec = pl.BlockSpec((1, HYENA_ORDER, 2 * RADIX, m_sub, lanes), lambda c, b: (layer, 0, 0, 0, c),
                         pipeline_mode=pl.Buffered(1))
    return pl.pallas_call(
        _hyena_kernel,
        grid=(nc, B),
        in_specs=[blk(0), blk(1), blk(2), blk(0), cw(0), cw(1), cw(2),
                  pl.BlockSpec((HYENA_ORDER, lanes), lambda c, b: (0, c)),
                  tab, tab, tab, tab, kspec],
        out_specs=blk(0),
        out_shape=jax.ShapeDtypeStruct((B, L, W_BRANCH), BF16),
        scratch_shapes=[pltpu.VMEM((L, LANES), F32)] * (5 * lanes // LANES),
        compiler_params=_params(2),
        name=f"hyena_{L}",
    )(hpre, hpre, hpre, sgh, conv_h, conv_h, conv_h, bias,
      tabs["cm"], tabs["sm"], tabs["icm"], tabs["ism"], spectra)


def _merge_kernel(x_ref, sc_ref, sh_ref, gt_ref, g_ref, ua_ref, up_ref, un_ref, ga_ref, ca_ref,
                  yb_ref, yc_ref, wg_ref, wb_ref, wo_ref, o_ref):
    t, nt = pl.program_id(1), pl.num_programs(1)
    x = x_ref[0]
    h = _rms_mod(x, g_ref[...], sc_ref[0], sh_ref[0]).astype(BF16)
    halo = up_ref.shape[1]
    first = jnp.where(t == 0, 0.0, up_ref[0, halo - 1:halo, :].astype(F32))
    last = jnp.where(t == nt - 1, 0.0, un_ref[0, 0:1, :].astype(F32))
    ya = (ga_ref[0].astype(F32) * _conv3(ua_ref[0].astype(F32), ca_ref[...], first, last)).astype(BF16)
    for r0 in range(0, x.shape[0], ROW_SPLIT):
        rows = slice(r0, r0 + ROW_SPLIT)
        ys = (ya[rows], yb_ref[0, rows], yc_ref[0, rows])
        merged = None
        for n in range(N_BRANCH):
            gate = jax.nn.sigmoid(_dot(h[rows], wg_ref[:, n * D_MODEL:(n + 1) * D_MODEL]))
            term = gate * _dot(ys[n], wb_ref[n])
            merged = term if merged is None else merged + term
        out = _dot(merged.astype(BF16), wo_ref[...])
        o_ref[0, rows] = x[rows] + gt_ref[0] * out


def _merge(x, sc, sh, gt, g, ua, ga, conv_a, yb, yc, wts):
    B, T, D = x.shape
    tm = min(T, 1024)
    halo = 16
    r = tm // halo
    row = lambda b, t: (b, t, 0)
    vec = pl.BlockSpec((1, 1, D), lambda b, t: (b, 0, 0))
    br = pl.BlockSpec((1, tm, W_BRANCH), row)
    prev = pl.BlockSpec((1, halo, W_BRANCH), lambda b, t: (b, jnp.maximum(t * r - 1, 0), 0))
    nxt = pl.BlockSpec((1, halo, W_BRANCH), lambda b, t: (b, jnp.minimum((t + 1) * r, T // halo - 1), 0))
    return pl.pallas_call(
        _merge_kernel,
        grid=(B, T // tm),
        in_specs=[pl.BlockSpec((1, tm, D), row), vec, vec, vec, _const_spec((1, D)),
                  br, prev, nxt, br, _const_spec((3, W_BRANCH)), br, br,
                  _const_spec(wts["wg"].shape), _const_spec(wts["wb"].shape), _const_spec(wts["wo"].shape)],
        out_specs=pl.BlockSpec((1, tm, D), row),
        out_shape=jax.ShapeDtypeStruct((B, T, D), F32),
        compiler_params=_params(2),
        name=f"merge_{T}",
    )(x, sc, sh, gt, g[None, :], ua, ua, ua, ga, conv_a, yb, yc, wts["wg"], wts["wb"], wts["wo"])


def _rope_tables(S):
    pos = jnp.arange(S, dtype=jnp.int32)
    coord = jnp.stack([(pos // GRID_W).astype(F32), (pos % GRID_W).astype(F32)], axis=1)
    inv = ROPE_BASE ** (-jnp.arange(ROPE_FREQS, dtype=F32) / ROPE_FREQS)
    ang = coord[:, :, None] * inv
    cos = jnp.repeat(jnp.cos(ang)[:, :, None, :], 2, axis=2).reshape(S, HEAD_DIM)
    sin = jnp.sin(ang)
    sin = jnp.stack([-sin, sin], axis=2).reshape(S, HEAD_DIM)
    return jnp.tile(cos, (1, 2)), jnp.tile(sin, (1, 2))


def _layer_weights(w_in, w_branch, w_out):
    return dict(
        wa=w_in[:, :A_COLS].astype(BF16),
        wh=w_in[:, H_OFF:C_OFF].astype(BF16),
        wq=w_in[:, C_OFF:G_OFF].astype(BF16),
        wkv=w_in[:, K_OFF:Z_OFF].astype(BF16),
        wg=w_in[:, G_OFF:].astype(BF16),
        wb=w_branch.astype(BF16),
        wo=w_out.astype(BF16),
    )


def _mixers(x, sc, sh, gt, g, wts, consts, conv_a, conv_h, bias, tabs, spectra, layer, rope, ctx_kv):
    ua, ga, hpre, sgh, q, kk, vv, sz = _inproj(x, sc, sh, g, wts, consts, rope=rope, kv_only=False)
    sources = ([ctx_kv] if ctx_kv is not None else []) + [(kk, vv)]
    yc = _attention(q, sz, sources, consts["shift_free"])
    yb = _hyena(hpre, sgh, conv_h, bias, tabs, spectra, layer)
    return _merge(x, sc, sh, gt, g, ua, ga, conv_a, yb, yc, wts), (kk, vv)


def kernel(x, c, ctx, c_ctx, norm_g, w_mod, b_mod, w_in, conv_a, conv_h, filt_w1, filt_b1, filt_w2,
           filt_b2, filt_w3, filt_b3, filt_freq, hyena_bias, q_norm_g, k_norm_g, w_branch, w_out):
    B, S, D = x.shape
    Lc = ctx.shape[1]
    cc = jnp.concatenate([c, c_ctx[None, :], jnp.zeros((MOD_ROWS - B - 1, D), F32)], axis=0)
    mod = _modulation(cc, w_mod, b_mod)

    tabs_s, tabs_c = _dft_tables(S), _dft_tables(Lc)
    filt = (filt_w1, filt_b1, filt_w2, filt_b2, filt_w3, filt_b3, filt_freq)
    spec_s = _filter_spectra(S, tabs_s, *filt)
    spec_c = _filter_spectra(Lc, tabs_c, *filt)

    cos, sin = _rope_tables(S)
    ones = jnp.ones((HEAD_DIM, HEAD_DIM), F32)
    bdq = jnp.kron(jnp.eye(N_HEADS, dtype=F32), ones).astype(BF16)
    bdk = jnp.kron(jnp.eye(N_KV_HEADS, dtype=F32), ones).astype(BF16)

    for i in range(DEPTH):
        last = i == DEPTH - 1
        wts = _layer_weights(w_in[i], w_branch[i], w_out[i])
        consts = dict(gq=jnp.tile(q_norm_g[i], N_HEADS)[None, :],
                      gk=jnp.tile(k_norm_g[i], N_KV_HEADS)[None, :],
                      bdq=bdq, bdk=bdk, cos=cos, sin=sin)
        score_bound = (HEAD_DIM * Q_SCALE) * jnp.max(jnp.abs(q_norm_g[i])) * jnp.max(jnp.abs(k_norm_g[i]))
        consts["shift_free"] = score_bound <= SAFE_LOG2_RANGE
        split = lambda rows: [rows[:, None, j * D:(j + 1) * D] for j in range(3)]
        sh, sc, gt = split(mod[i, :B])
        sh_c, sc_c, gt_c = [jnp.broadcast_to(v, (B, 1, D)) for v in split(mod[i, B:B + 1])]
        if last:
            ctx_kv = tuple(_inproj(ctx, sc_c, sh_c, norm_g[i], wts, consts, rope=False, kv_only=True))
        else:
            ctx_next, ctx_kv = _mixers(ctx, sc_c, sh_c, gt_c, norm_g[i], wts, consts, conv_a[i], conv_h[i],
                                       hyena_bias[i], tabs_c, spec_c, i, False, None)
        x, _ = _mixers(x, sc, sh, gt, norm_g[i], wts, consts, conv_a[i], conv_h[i],
                       hyena_bias[i], tabs_s, spec_s, i, True, ctx_kv)
        if not last:
            ctx = ctx_next
    return x
```

```python
import functools
import math

import jax
import jax.numpy as jnp
from jax import lax
from jax.experimental import pallas as pl
from jax.experimental.pallas import tpu as pltpu

F32 = jnp.float32
BF16 = jnp.bfloat16

D_MODEL = 1024
DEPTH = 4
GRID_W = 64
W_BRANCH = 512
N_BRANCH = 3
N_HEADS = 8
N_KV_HEADS = 2
HEAD_DIM = 64
ROPE_FREQS = HEAD_DIM // 4
ROPE_BASE = 10000.0
ATTN_SCALE = HEAD_DIM ** -0.5
HYENA_ORDER = 2
FILTER_EMB = 33
FILTER_BANDS = (FILTER_EMB - 1) // 2
FILTER_HIDDEN = 64
HYENA_TARGET = 1e-2
MIN_DECAY = math.log(HYENA_TARGET) / 1.5
MAX_DECAY = math.log(HYENA_TARGET) / 0.3
EPS = 1e-6

A_COLS = 4 * W_BRANCH
H_OFF = A_COLS
C_OFF = H_OFF + 4 * W_BRANCH
Q_COLS = N_HEADS * HEAD_DIM
KV_COLS = N_KV_HEADS * HEAD_DIM
K_OFF = C_OFF + Q_COLS
V_OFF = K_OFF + KV_COLS
Z_OFF = V_OFF + KV_COLS
G_OFF = Z_OFF + W_BRANCH

LANES = 128
VMEM_LIMIT_BYTES = 58 * 1024 * 1024
KV_DUP_COLS = 2 * KV_COLS
V_TILE_COLS = 2 * KV_DUP_COLS
Q_SCALE = ATTN_SCALE * math.log2(math.e)
SAFE_LOG2_RANGE = 64.0
Q_SPLIT = 512
KEY_CHUNK = 512
MOD_ROWS = 24
FILT_PAD = 64
HALF = 256
RADIX = 4
ROW_SPLIT = 256


def _params(n_axes):
    return pltpu.CompilerParams(dimension_semantics=("arbitrary",) * n_axes,
                                vmem_limit_bytes=VMEM_LIMIT_BYTES)


def _const_spec(shape):
    nd = len(shape)
    return pl.BlockSpec(shape, lambda *_: (0,) * nd, pipeline_mode=pl.Buffered(1))


def _layer_spec(shape, layer):
    nd = len(shape)
    return pl.BlockSpec((1,) + tuple(shape[1:]), lambda *_: (layer,) + (0,) * (nd - 1),
                        pipeline_mode=pl.Buffered(1))


def _silu(v):
    return v * jax.nn.sigmoid(v)


def _dot(a, b):
    return jnp.dot(a, b, preferred_element_type=F32)


def _split(a):
    hi = a.astype(BF16)
    lo = (a - hi.astype(F32)).astype(BF16)
    return hi, lo


def _dot3(a, b):
    ah, al = _split(a)
    bh, bl = _split(b)
    return _dot(ah, bh) + (_dot(ah, bl) + _dot(al, bh))


def _rms_mod(x, g, sc, sh):
    y = x * lax.rsqrt(jnp.mean(x * x, axis=-1, keepdims=True) + EPS)
    return (y * g) * (1.0 + sc) + sh


def _shift_rows(u, first_row, last_row):
    n = u.shape[0]
    row = lax.broadcasted_iota(jnp.int32, u.shape, 0)
    prev = jnp.where(row == 0, first_row, pltpu.roll(u, 1, 0))
    nxt = jnp.where(row == n - 1, last_row, pltpu.roll(u, n - 1, 0))
    return prev, nxt


def _conv3(u, w, first_row=0.0, last_row=0.0):
    prev, nxt = _shift_rows(u, first_row, last_row)
    return prev * w[0:1] + u * w[1:2] + nxt * w[2:3]


def _mod_kernel(cc_ref, w_ref, b_ref, o_ref):
    o_ref[0] = _dot3(_silu(cc_ref[...]), w_ref[0]) + b_ref[0]


def _modulation(cc, w_mod, b_mod):
    return pl.pallas_call(
        _mod_kernel,
        grid=(DEPTH, 3),
        in_specs=[pl.BlockSpec((MOD_ROWS, D_MODEL), lambda i, j: (0, 0)),
                  pl.BlockSpec((1, D_MODEL, D_MODEL), lambda i, j: (i, 0, j)),
                  pl.BlockSpec((1, 1, D_MODEL), lambda i, j: (i, 0, j))],
        out_specs=pl.BlockSpec((1, MOD_ROWS, D_MODEL), lambda i, j: (i, 0, j)),
        out_shape=jax.ShapeDtypeStruct((DEPTH, MOD_ROWS, 3 * D_MODEL), F32),
        compiler_params=_params(2),
        name="modulation",
    )(cc, w_mod, b_mod.reshape(DEPTH, 1, 3 * D_MODEL))


def _dft_tables(L):
    n2, m_sub = 4 * L, L // RADIX
    idx = jnp.arange(m_sub, dtype=jnp.int32)
    sample = RADIX * idx[None, None, :] + jnp.arange(RADIX, dtype=jnp.int32)[:, None, None]
    q = ((2 * idx[None, :, None] + 1) * sample) % n2
    ang = q.astype(F32) * (2.0 * math.pi / n2)
    cm, sm = jnp.cos(ang).astype(BF16), (-jnp.sin(ang)).astype(BF16)
    tr = lambda v: jnp.swapaxes(v, 1, 2) * (1.0 / L)
    return dict(cm=cm, sm=sm, icm=tr(cm), ism=tr(sm))


def _butterfly4(t0, t1, t2, t3):
    pr, pi, mr, mi = t0[0] + t2[0], t0[1] + t2[1], t0[0] - t2[0], t0[1] - t2[1]
    qr, qi, nr, ni = t1[0] + t3[0], t1[1] + t3[1], t1[0] - t3[0], t1[1] - t3[1]
    return (pr + qr, pi + qi), (mr - ni, mi + nr), (mr + ni, mi - nr), (pr - qr, pi - qi)


def _split_dft(parts, cm_ref, sm_ref):
    t = []
    for s in range(RADIX):
        u = parts[s].astype(BF16)
        t.append((_dot(cm_ref[s], u), _dot(sm_ref[s], u)))
    return _butterfly4(*t)


def _filter_hidden_kernel(feats_ref, w1_ref, b1_ref, w2_ref, b2_ref, fq_ref, h_ref):
    fq = fq_ref[0]
    h = jnp.sin(fq[0:1] * (_dot3(feats_ref[...], w1_ref[0]) + b1_ref[0]))
    h_ref[0] = jnp.sin(fq[1:2] * (_dot3(h, w2_ref[0]) + b2_ref[0]))


def _filter_kernel(h_ref, w3f_ref, w3b_ref, b3f_ref, b3b_ref, dl_ref, cm_ref, sm_ref, k_ref, *, L):
    m_sub = L // RADIX
    c = dl_ref.shape[1]
    h = h_ref[0]
    row = lax.broadcasted_iota(jnp.int32, (L, c), 0)
    lag = RADIX * (row % m_sub) + row // m_sub
    win = jnp.exp(-(lag.astype(F32) * (1.0 / (L - 1))) * dl_ref[...])
    hf = (_dot3(h, w3f_ref[0]) + b3f_ref[0]) * win
    hb = jnp.where(lag == 0, 0.0, (_dot3(h, w3b_ref[0]) + b3b_ref[0]) * win)
    nrm = jnp.sum(jnp.abs(hf), axis=0, keepdims=True) + jnp.sum(jnp.abs(hb), axis=0, keepdims=True)
    hf, hb = hf / nrm, hb / nrm
    parts = [jnp.concatenate([hf[s * m_sub:(s + 1) * m_sub], hb[s * m_sub:(s + 1) * m_sub]], axis=1)
             for s in range(RADIX)]
    spec = _split_dft(parts, cm_ref, sm_ref)
    for j, (re, im) in enumerate(spec):
        k_ref[0, 0, 2 * j] = re[:, :c] + re[:, c:]
        k_ref[0, 0, 2 * j + 1] = im[:, :c] - im[:, c:]


def _filter_spectra(L, tabs, filt_w1, filt_b1, filt_w2, filt_b2, filt_w3, filt_b3, filt_freq):
    m_sub = L // RADIX
    t = jnp.linspace(0.0, 1.0, L, dtype=F32)[:, None]
    bands = jnp.linspace(1e-4, FILTER_BANDS - 1, FILTER_BANDS, dtype=F32)
    w = (2.0 * math.pi / L) * jnp.arange(L, dtype=F32)[:, None]
    feats = jnp.concatenate([t, jnp.cos(w * bands), jnp.sin(w * bands)], axis=-1)
    feats = jnp.concatenate([feats[s::RADIX] for s in range(RADIX)], axis=0)
    feats = jnp.pad(feats, ((0, 0), (0, FILT_PAD - FILTER_EMB)))
    w1 = jnp.pad(filt_w1, ((0, 0), (0, FILT_PAD - FILTER_EMB), (0, 0)))
    deltas = jnp.abs(jnp.linspace(MIN_DECAY, MAX_DECAY, W_BRANCH, dtype=F32))[None, :]
    lay1 = lambda i: (i, 0, 0)
    hidden = pl.pallas_call(
        _filter_hidden_kernel,
        grid=(DEPTH,),
        in_specs=[_const_spec((L, FILT_PAD)),
                  pl.BlockSpec((1, FILT_PAD, FILTER_HIDDEN), lay1),
                  pl.BlockSpec((1, 1, FILTER_HIDDEN), lay1),
                  pl.BlockSpec((1, FILTER_HIDDEN, FILTER_HIDDEN), lay1),
                  pl.BlockSpec((1, 1, FILTER_HIDDEN), lay1),
                  pl.BlockSpec((1, 2, FILTER_HIDDEN), lay1)],
        out_specs=pl.BlockSpec((1, L, FILTER_HIDDEN), lay1),
        out_shape=jax.ShapeDtypeStruct((DEPTH, L, FILTER_HIDDEN), F32),
        compiler_params=_params(1),
        name=f"filter_hidden_{L}",
    )(feats, w1, filt_b1[:, None], filt_w2, filt_b2[:, None], filt_freq)
    nc = W_BRANCH // HALF
    fwd = lambda i, o, c: (i, 0, o * 2 * nc + c)
    bwd = lambda i, o, c: (i, 0, o * 2 * nc + nc + c)
    tab = _const_spec((RADIX, m_sub, m_sub))
    return pl.pallas_call(
        functools.partial(_filter_kernel, L=L),
        grid=(DEPTH, HYENA_ORDER, nc),
        in_specs=[pl.BlockSpec((1, L, FILTER_HIDDEN), lambda i, o, c: (i, 0, 0)),
                  pl.BlockSpec((1, FILTER_HIDDEN, HALF), fwd),
                  pl.BlockSpec((1, FILTER_HIDDEN, HALF), bwd),
                  pl.BlockSpec((1, 1, HALF), fwd),
                  pl.BlockSpec((1, 1, HALF), bwd),
                  pl.BlockSpec((1, HALF), lambda i, o, c: (0, c)),
                  tab, tab],
        out_specs=pl.BlockSpec((1, 1, 2 * RADIX, m_sub, HALF), lambda i, o, c: (i, o, 0, 0, c)),
        out_shape=jax.ShapeDtypeStruct((DEPTH, HYENA_ORDER, 2 * RADIX, m_sub, W_BRANCH), F32),
        compiler_params=_params(3),
        name=f"filter_spectra_{L}",
    )(hidden, filt_w3, filt_w3, filt_b3[:, None], filt_b3[:, None], deltas,
      tabs["cm"], tabs["sm"])


def _head_norm(v, bd_ref, g):
    ss = _dot((v * v).astype(BF16), bd_ref[...])
    return v * lax.rsqrt(ss * (1.0 / HEAD_DIM) + EPS) * g


def _rope(v, cos, sin):
    n = v.shape[1] // LANES
    lane = lax.broadcasted_iota(jnp.int32, (v.shape[0], LANES), 1)
    first = (lane % 32) < 16
    out = []
    for i in range(n):
        c = v[:, i * LANES:(i + 1) * LANES]
        partner = jnp.where(first, pltpu.roll(c, LANES - 16, 1), pltpu.roll(c, 16, 1))
        out.append(c * cos + partner * sin)
    return jnp.concatenate(out, axis=1)


def _kv_tiles(k, v):
    low = lax.broadcasted_iota(jnp.int32, k.shape, 1) < HEAD_DIM
    ks, vs = pltpu.roll(k, HEAD_DIM, 1), pltpu.roll(v, HEAD_DIM, 1)
    kk = jnp.concatenate([jnp.where(low, k, ks), jnp.where(low, ks, k)], axis=1)
    vv = jnp.concatenate([jnp.where(low, v, 1.0), jnp.where(low, 1.0, vs),
                          jnp.where(low, vs, 1.0), jnp.where(low, 1.0, v)], axis=1)
    return kk.astype(BF16), vv.astype(BF16)


def _inproj_kernel(*refs, rope, kv_only):
    it = iter(refs)
    x_ref, sc_ref, sh_ref, g_ref = next(it), next(it), next(it), next(it)
    if not kv_only:
        wa_ref, wh_ref = next(it), next(it)
    wq_ref = next(it)
    if not kv_only:
        gq_ref, bdq_ref = next(it), next(it)
    gk_ref, bdk_ref = next(it), next(it)
    if rope:
        cos_ref, sin_ref = next(it), next(it)
    outs = list(it)

    h = _rms_mod(x_ref[0], g_ref[...], sc_ref[0], sh_ref[0]).astype(BF16)
    w = W_BRANCH
    if kv_only:
        kk_ref, vv_ref = outs
        acc = _dot(h, wq_ref[0])
        k = _head_norm(acc[:, :KV_COLS], bdk_ref, gk_ref[...])
        kk_ref[0], vv_ref[0] = _kv_tiles(k, acc[:, KV_COLS:])
        return

    ua_ref, ga_ref, hp_ref, sgh_ref, q_ref, kk_ref, vv_ref, sz_ref = outs
    for r0 in range(0, h.shape[0], ROW_SPLIT):
        rows = slice(r0, r0 + ROW_SPLIT)
        hr = h[rows]
        acc = _dot(hr, wq_ref[0])
        q = _head_norm(acc[:, :Q_COLS], bdq_ref, gq_ref[...])
        k = _head_norm(acc[:, Q_COLS:Q_COLS + KV_COLS], bdk_ref, gk_ref[...])
        if rope:
            q = _rope(q, cos_ref[rows], sin_ref[rows])
            k = _rope(k, cos_ref[rows], sin_ref[rows])
        q_ref[0, rows] = (q * Q_SCALE).astype(BF16)
        kk_ref[0, rows], vv_ref[0, rows] = _kv_tiles(k, acc[:, Q_COLS + KV_COLS:Q_COLS + 2 * KV_COLS])
        sz_ref[0, rows] = _silu(acc[:, Q_COLS + 2 * KV_COLS:]).astype(BF16)
        acc = _dot(hr, wa_ref[0])
        ua_ref[0, rows] = (acc[:, 2 * w:3 * w] * acc[:, :w]).astype(BF16)
        ga_ref[0, rows] = (acc[:, w:2 * w] * _silu(acc[:, 3 * w:])).astype(BF16)
        acc = _dot(hr, wh_ref[0])
        hp_ref[0, rows] = acc[:, :3 * w].astype(BF16)
        sgh_ref[0, rows] = _silu(acc[:, 3 * w:]).astype(BF16)


def _inproj(x, sc, sh, g, wts, layer, consts, *, rope, kv_only):
    B, T, D = x.shape
    tm = min(T, 1024)
    row = lambda b, t: (b, t, 0)
    vec = pl.BlockSpec((1, 1, D), lambda b, t: (b, 0, 0))
    args = [x, sc, sh, g[None, :]]
    specs = [pl.BlockSpec((1, tm, D), row), vec, vec, _const_spec((1, D))]
    if not kv_only:
        args += [wts["wa"], wts["wh"], wts["wq"], consts["gq"], consts["bdq"]]
        specs += [_layer_spec(wts["wa"].shape, layer), _layer_spec(wts["wh"].shape, layer),
                  _layer_spec(wts["wq"].shape, layer), _const_spec(consts["gq"].shape),
                  _const_spec(consts["bdq"].shape)]
    else:
        args += [wts["wkv"]]
        specs += [_layer_spec(wts["wkv"].shape, layer)]
    args += [consts["gk"], consts["bdk"]]
    specs += [_const_spec(consts["gk"].shape), _const_spec(consts["bdk"].shape)]
    if rope:
        args += [consts["cos"], consts["sin"]]
        specs += [pl.BlockSpec((tm, LANES), lambda b, t: (t, 0))] * 2
    widths = ([KV_DUP_COLS, V_TILE_COLS] if kv_only else
              [W_BRANCH, W_BRANCH, 3 * W_BRANCH, W_BRANCH, Q_COLS, KV_DUP_COLS, V_TILE_COLS, W_BRANCH])
    return pl.pallas_call(
        functools.partial(_inproj_kernel, rope=rope, kv_only=kv_only),
        grid=(B, T // tm),
        in_specs=specs,
        out_specs=[pl.BlockSpec((1, tm, n), row) for n in widths],
        out_shape=[jax.ShapeDtypeStruct((B, T, n), BF16) for n in widths],
        compiler_params=_params(2),
        name=f"inproj_{T}" + ("_kv" if kv_only else ""),
    )(*args)


def _attn_kernel(*refs, n_src, shifted):
    q_ref, sz_ref = refs[0], refs[1]
    kv = refs[2:2 + 2 * n_src]
    o_ref = refs[-1]
    tq = min(q_ref.shape[1], Q_SPLIT)
    low = lax.broadcasted_iota(jnp.int32, (tq, LANES), 1) < HEAD_DIM
    nt = (((1,), (1,)), ((), ()))
    for r0, j in [(r0, j) for r0 in range(0, q_ref.shape[1], tq) for j in range(2)]:
        qrows = slice(r0, r0 + tq)
        cols = slice(j * LANES, (j + 1) * LANES)
        qp = q_ref[0, qrows, cols].astype(F32)
        halves = []
        for half, sel in enumerate((low, jnp.logical_not(low))):
            qm = jnp.where(sel, qp, 0.0).astype(BF16)
            vt = slice(half * LANES, (half + 1) * LANES)
            if shifted:
                s = [lax.dot_general(qm, kv[2 * i][0], nt, preferred_element_type=F32) for i in range(n_src)]
                m = functools.reduce(jnp.maximum, [jnp.max(v, axis=-1, keepdims=True) for v in s])
                terms = [_dot(jnp.exp2(s[i] - m).astype(BF16), kv[2 * i + 1][0, :, vt]) for i in range(n_src)]
            else:
                terms = []
                for i in range(n_src):
                    n_keys = kv[2 * i].shape[1]
                    for c0 in range(0, n_keys, KEY_CHUNK):
                        rows = slice(c0, min(c0 + KEY_CHUNK, n_keys))
                        s = lax.dot_general(qm, kv[2 * i][0, rows, :], nt, preferred_element_type=F32)
                        terms.append(_dot(jnp.exp2(s).astype(BF16), kv[2 * i + 1][0, rows, vt]))
            halves.append(functools.reduce(jnp.add, terms))
        num = jnp.where(low, halves[0], halves[1])
        den = pltpu.roll(jnp.where(low, halves[1], halves[0]), HEAD_DIM, 1)
        o_ref[0, qrows, cols] = (num / den * sz_ref[0, qrows, cols].astype(F32)).astype(BF16)


def _attention(q, sz, sources, shift_free):
    B, T, _ = q.shape
    gw = Q_COLS // N_KV_HEADS
    args = [q, sz] + [a for src in sources for a in src]

    def call(shifted):
        tq = min(T, 256 if shifted else 2048)
        qspec = pl.BlockSpec((1, tq, gw), lambda b, g, t: (b, t, g))
        specs = [qspec, qspec]
        for kk, vv in sources:
            specs += [pl.BlockSpec((1, kk.shape[1], LANES), lambda b, g, t: (b, 0, g)),
                      pl.BlockSpec((1, vv.shape[1], 2 * LANES), lambda b, g, t: (b, 0, g))]
        return pl.pallas_call(
            functools.partial(_attn_kernel, n_src=len(sources), shifted=shifted),
            grid=(B, N_KV_HEADS, T // tq),
            in_specs=specs,
            out_specs=qspec,
            out_shape=jax.ShapeDtypeStruct((B, T, Q_COLS), BF16),
            compiler_params=_params(3),
            name=f"attention_{T}" + ("_shifted" if shifted else ""),
        )
    return lax.cond(shift_free, call(False), call(True), *args)


def _deinterleave(x, scr):
    n, c = x.shape
    slabs = range(c // LANES)
    for j in slabs:
        scr[j][...] = x[:, j * LANES:(j + 1) * LANES]
    return [jnp.concatenate([scr[j][pl.ds(s, n // RADIX, stride=RADIX), :] for j in slabs], axis=1)
            for s in range(RADIX)]


def _interleave(parts, scr):
    m, c = parts[0].shape
    slabs = range(c // LANES)
    for j in slabs:
        for s in range(RADIX):
            scr[j][pl.ds(s, m, stride=RADIX), :] = parts[s][:, j * LANES:(j + 1) * LANES]
    return jnp.concatenate([scr[j][...] for j in slabs], axis=1)


def _conv3_split(parts, w):
    m = parts[0].shape[0]
    row = lax.broadcasted_iota(jnp.int32, parts[0].shape, 0)
    before = jnp.where(row == 0, 0.0, pltpu.roll(parts[-1], 1, 0))
    after = jnp.where(row == m - 1, 0.0, pltpu.roll(parts[0], m - 1, 0))
    ext = [before] + list(parts) + [after]
    return [ext[s] * w[0:1] + ext[s + 1] * w[1:2] + ext[s + 2] * w[2:3] for s in range(RADIX)]


def _long_conv_split(parts, k, cm_ref, sm_ref, icm_ref, ism_ref):
    spec = _split_dft(parts, cm_ref, sm_ref)
    ya, yb, yc, yd = [(xr * k[2 * j] - xi * k[2 * j + 1], xr * k[2 * j + 1] + xi * k[2 * j])
                      for j, (xr, xi) in enumerate(spec)]
    g0, g1, g3, g2 = _butterfly4(ya, yc, yd, yb)
    return [_dot(icm_ref[s], gr.astype(BF16)) + _dot(ism_ref[s], gi.astype(BF16))
            for s, (gr, gi) in enumerate((g0, g1, g2, g3))]


def _hyena_kernel(hv_ref, hx1_ref, hx2_ref, sg_ref, wv_ref, wx1_ref, wx2_ref, bias_ref,
                  cm_ref, sm_ref, icm_ref, ism_ref, k_ref, o_ref, *scratch):
    c = o_ref.shape[2]
    ns = c // LANES
    s_v, s_x1, s_x2, s_g, s_o = [scratch[i * ns:(i + 1) * ns] for i in range(5)]
    tabs = (cm_ref, sm_ref, icm_ref, ism_ref)
    split = lambda ref, scr: _deinterleave(ref[0].astype(F32), scr)
    v = _conv3_split(split(hv_ref, s_v), wv_ref[...])
    x1 = _conv3_split(split(hx1_ref, s_x1), wx1_ref[...])
    x2 = _conv3_split(split(hx2_ref, s_x2), wx2_ref[...])
    gate = split(sg_ref, s_g)
    b0, b1 = bias_ref[0:1], bias_ref[1:2]
    n_k = 2 * RADIX
    y = _long_conv_split(v, [k_ref[0, 0, j] for j in range(n_k)], *tabs)
    z1 = [x1[s] * (y[s] + v[s] * b0) for s in range(RADIX)]
    y = _long_conv_split(z1, [k_ref[0, 1, j] for j in range(n_k)], *tabs)
    out = [x2[s] * (y[s] + z1[s] * b1) * gate[s] for s in range(RADIX)]
    o_ref[0] = _interleave(out, s_o).astype(BF16)


def _hyena(hpre, sgh, conv_h, bias, tabs, spectra, layer):
    B, L, _ = hpre.shape
    m_sub = L // RADIX
    lanes = HALF if L * HALF * 4 > (1 << 20) else W_BRANCH
    nc = W_BRANCH // lanes
    blk = lambda j: pl.BlockSpec((1, L, lanes), lambda c, b: (b, 0, j * nc + c))
    cw = lambda j: pl.BlockSpec((3, lanes), lambda c, b: (0, j * nc + c))
    tab = _const_spec((RADIX, m_sub, m_sub))
    kspec = pl.BlockSpec((1, HYENA_ORDER, 2 * RADIX, m_sub, lanes), lambda c, b: (layer, 0, 0, 0, c),
                         pipeline_mode=pl.Buffered(1))
    return pl.pallas_call(
        _hyena_kernel,
        grid=(nc, B),
        in_specs=[blk(0), blk(1), blk(2), blk(0), cw(0), cw(1), cw(2),
                  pl.BlockSpec((HYENA_ORDER, lanes), lambda c, b: (0, c)),
                  tab, tab, tab, tab, kspec],
        out_specs=blk(0),
        out_shape=jax.ShapeDtypeStruct((B, L, W_BRANCH), BF16),
        scratch_shapes=[pltpu.VMEM((L, LANES), F32)] * (5 * lanes // LANES),
        compiler_params=_params(2),
        name=f"hyena_{L}",
    )(hpre, hpre, hpre, sgh, conv_h, conv_h, conv_h, bias,
      tabs["cm"], tabs["sm"], tabs["icm"], tabs["ism"], spectra)


def _merge_kernel(x_ref, sc_ref, sh_ref, gt_ref, g_ref, ua_ref, up_ref, un_ref, ga_ref, ca_ref,
                  yb_ref, yc_ref, wg_ref, wb_ref, wo_ref, o_ref):
    t, nt = pl.program_id(1), pl.num_programs(1)
    x = x_ref[0]
    h = _rms_mod(x, g_ref[...], sc_ref[0], sh_ref[0]).astype(BF16)
    halo = up_ref.shape[1]
    first = jnp.where(t == 0, 0.0, up_ref[0, halo - 1:halo, :].astype(F32))
    last = jnp.where(t == nt - 1, 0.0, un_ref[0, 0:1, :].astype(F32))
    ya = (ga_ref[0].astype(F32) * _conv3(ua_ref[0].astype(F32), ca_ref[...], first, last)).astype(BF16)
    for r0 in range(0, x.shape[0], ROW_SPLIT):
        rows = slice(r0, r0 + ROW_SPLIT)
        ys = (ya[rows], yb_ref[0, rows], yc_ref[0, rows])
        merged = None
        for n in range(N_BRANCH):
            gate = jax.nn.sigmoid(_dot(h[rows], wg_ref[0, :, n * D_MODEL:(n + 1) * D_MODEL]))
            term = gate * _dot(ys[n], wb_ref[0, n])
            merged = term if merged is None else merged + term
        out = _dot(merged.astype(BF16), wo_ref[0])
        o_ref[0, rows] = x[rows] + gt_ref[0] * out


def _merge(x, sc, sh, gt, g, ua, ga, conv_a, yb, yc, wts, layer):
    B, T, D = x.shape
    tm = min(T, 1024)
    halo = 16
    r = tm // halo
    row = lambda b, t: (b, t, 0)
    vec = pl.BlockSpec((1, 1, D), lambda b, t: (b, 0, 0))
    br = pl.BlockSpec((1, tm, W_BRANCH), row)
    prev = pl.BlockSpec((1, halo, W_BRANCH), lambda b, t: (b, jnp.maximum(t * r - 1, 0), 0))
    nxt = pl.BlockSpec((1, halo, W_BRANCH), lambda b, t: (b, jnp.minimum((t + 1) * r, T // halo - 1), 0))
    return pl.pallas_call(
        _merge_kernel,
        grid=(B, T // tm),
        in_specs=[pl.BlockSpec((1, tm, D), row), vec, vec, vec, _const_spec((1, D)),
                  br, prev, nxt, br, _const_spec((3, W_BRANCH)), br, br,
                  _layer_spec(wts["wg"].shape, layer), _layer_spec(wts["wb"].shape, layer),
                  _layer_spec(wts["wo"].shape, layer)],
        out_specs=pl.BlockSpec((1, tm, D), row),
        out_shape=jax.ShapeDtypeStruct((B, T, D), F32),
        compiler_params=_params(2),
        name=f"merge_{T}",
    )(x, sc, sh, gt, g[None, :], ua, ua, ua, ga, conv_a, yb, yc, wts["wg"], wts["wb"], wts["wo"])


def _rope_tables(S):
    pos = jnp.arange(S, dtype=jnp.int32)
    coord = jnp.stack([(pos // GRID_W).astype(F32), (pos % GRID_W).astype(F32)], axis=1)
    inv = ROPE_BASE ** (-jnp.arange(ROPE_FREQS, dtype=F32) / ROPE_FREQS)
    ang = coord[:, :, None] * inv
    cos = jnp.repeat(jnp.cos(ang)[:, :, None, :], 2, axis=2).reshape(S, HEAD_DIM)
    sin = jnp.sin(ang)
    sin = jnp.stack([-sin, sin], axis=2).reshape(S, HEAD_DIM)
    return jnp.tile(cos, (1, 2)), jnp.tile(sin, (1, 2))


def _bf16_weights(w_in, w_branch, w_out):
    return dict(
        wa=w_in[:, :, :A_COLS].astype(BF16),
        wh=w_in[:, :, H_OFF:C_OFF].astype(BF16),
        wq=w_in[:, :, C_OFF:G_OFF].astype(BF16),
        wkv=w_in[:, :, K_OFF:Z_OFF].astype(BF16),
        wg=w_in[:, :, G_OFF:].astype(BF16),
        wb=w_branch.astype(BF16),
        wo=w_out.astype(BF16),
    )


def _mixers(x, sc, sh, gt, g, wts, consts, conv_a, conv_h, bias, tabs, spectra, layer, rope, ctx_kv):
    ua, ga, hpre, sgh, q, kk, vv, sz = _inproj(x, sc, sh, g, wts, layer, consts, rope=rope, kv_only=False)
    sources = ([ctx_kv] if ctx_kv is not None else []) + [(kk, vv)]
    yc = _attention(q, sz, sources, consts["shift_free"])
    yb = _hyena(hpre, sgh, conv_h, bias, tabs, spectra, layer)
    return _merge(x, sc, sh, gt, g, ua, ga, conv_a, yb, yc, wts, layer), (kk, vv)


def kernel(x, c, ctx, c_ctx, norm_g, w_mod, b_mod, w_in, conv_a, conv_h, filt_w1, filt_b1, filt_w2,
           filt_b2, filt_w3, filt_b3, filt_freq, hyena_bias, q_norm_g, k_norm_g, w_branch, w_out):
    B, S, D = x.shape
    Lc = ctx.shape[1]
    cc = jnp.concatenate([c, c_ctx[None, :], jnp.zeros((MOD_ROWS - B - 1, D), F32)], axis=0)
    mod = _modulation(cc, w_mod, b_mod)

    tabs_s, tabs_c = _dft_tables(S), _dft_tables(Lc)
    filt = (filt_w1, filt_b1, filt_w2, filt_b2, filt_w3, filt_b3, filt_freq)
    spec_s = _filter_spectra(S, tabs_s, *filt)
    spec_c = _filter_spectra(Lc, tabs_c, *filt)

    cos, sin = _rope_tables(S)
    wts = _bf16_weights(w_in, w_branch, w_out)
    ones = jnp.ones((HEAD_DIM, HEAD_DIM), F32)
    bdq = jnp.kron(jnp.eye(N_HEADS, dtype=F32), ones).astype(BF16)
    bdk = jnp.kron(jnp.eye(N_KV_HEADS, dtype=F32), ones).astype(BF16)

    for i in range(DEPTH):
        last = i == DEPTH - 1
        consts = dict(gq=jnp.tile(q_norm_g[i], N_HEADS)[None, :],
                      gk=jnp.tile(k_norm_g[i], N_KV_HEADS)[None, :],
                      bdq=bdq, bdk=bdk, cos=cos, sin=sin)
        score_bound = (HEAD_DIM * Q_SCALE) * jnp.max(jnp.abs(q_norm_g[i])) * jnp.max(jnp.abs(k_norm_g[i]))
        consts["shift_free"] = score_bound <= SAFE_LOG2_RANGE
        split = lambda rows: [rows[:, None, j * D:(j + 1) * D] for j in range(3)]
        sh, sc, gt = split(mod[i, :B])
        sh_c, sc_c, gt_c = [jnp.broadcast_to(v, (B, 1, D)) for v in split(mod[i, B:B + 1])]
        if last:
            ctx_kv = tuple(_inproj(ctx, sc_c, sh_c, norm_g[i], wts, i, consts, rope=False, kv_only=True))
        else:
            ctx_next, ctx_kv = _mixers(ctx, sc_c, sh_c, gt_c, norm_g[i], wts, consts, conv_a[i], conv_h[i],
                                       hyena_bias[i], tabs_c, spec_c, i, False, None)
        x, _ = _mixers(x, sc, sh, gt, norm_g[i], wts, consts, conv_a[i], conv_h[i],
                       hyena_bias[i], tabs_s, spec_s, i, True, ctx_kv)
        if not last:
            ctx = ctx_next
    return x
```

```python
import functools
import math

import jax
import jax.numpy as jnp
from jax import lax
from jax.experimental import pallas as pl
from jax.experimental.pallas import tpu as pltpu

F32 = jnp.float32
BF16 = jnp.bfloat16

D_MODEL = 1024
DEPTH = 4
GRID_W = 64
W_BRANCH = 512
N_BRANCH = 3
N_HEADS = 8
N_KV_HEADS = 2
HEAD_DIM = 64
ROPE_FREQS = HEAD_DIM // 4
ROPE_BASE = 10000.0
ATTN_SCALE = HEAD_DIM ** -0.5
HYENA_ORDER = 2
FILTER_EMB = 33
FILTER_BANDS = (FILTER_EMB - 1) // 2
FILTER_HIDDEN = 64
HYENA_TARGET = 1e-2
MIN_DECAY = math.log(HYENA_TARGET) / 1.5
MAX_DECAY = math.log(HYENA_TARGET) / 0.3
EPS = 1e-6

A_COLS = 4 * W_BRANCH
H_OFF = A_COLS
C_OFF = H_OFF + 4 * W_BRANCH
Q_COLS = N_HEADS * HEAD_DIM
KV_COLS = N_KV_HEADS * HEAD_DIM
K_OFF = C_OFF + Q_COLS
V_OFF = K_OFF + KV_COLS
Z_OFF = V_OFF + KV_COLS
G_OFF = Z_OFF + W_BRANCH

LANES = 128
VMEM_LIMIT_BYTES = 58 * 1024 * 1024
KV_DUP_COLS = 2 * KV_COLS
V_TILE_COLS = 2 * KV_DUP_COLS
Q_SCALE = ATTN_SCALE * math.log2(math.e)
SAFE_LOG2_RANGE = 64.0
Q_SPLIT = 512
KEY_CHUNK = 512
MOD_ROWS = 24
FILT_PAD = 64
HALF = 256
RADIX = 4
ROW_SPLIT = 256


def _params(n_axes):
    return pltpu.CompilerParams(dimension_semantics=("arbitrary",) * n_axes,
                                vmem_limit_bytes=VMEM_LIMIT_BYTES)


def _const_spec(shape):
    nd = len(shape)
    return pl.BlockSpec(shape, lambda *_: (0,) * nd, pipeline_mode=pl.Buffered(1))


def _layer_spec(shape, layer):
    nd = len(shape)
    return pl.BlockSpec((1,) + tuple(shape[1:]), lambda *_: (layer,) + (0,) * (nd - 1),
                        pipeline_mode=pl.Buffered(1))


def _silu(v):
    return v * jax.nn.sigmoid(v)


def _dot(a, b):
    return jnp.dot(a, b, preferred_element_type=F32)


def _split(a):
    hi = a.astype(BF16)
    lo = (a - hi.astype(F32)).astype(BF16)
    return hi, lo


def _dot3(a, b):
    ah, al = _split(a)
    bh, bl = _split(b)
    return _dot(ah, bh) + (_dot(ah, bl) + _dot(al, bh))


def _rms_mod(x, g, sc, sh):
    y = x * lax.rsqrt(jnp.mean(x * x, axis=-1, keepdims=True) + EPS)
    return (y * g) * (1.0 + sc) + sh


def _shift_rows(u, first_row, last_row):
    n = u.shape[0]
    row = lax.broadcasted_iota(jnp.int32, u.shape, 0)
    prev = jnp.where(row == 0, first_row, pltpu.roll(u, 1, 0))
    nxt = jnp.where(row == n - 1, last_row, pltpu.roll(u, n - 1, 0))
    return prev, nxt


def _mod_kernel(cc_ref, w_ref, b_ref, o_ref):
    o_ref[0] = _dot3(_silu(cc_ref[...]), w_ref[0]) + b_ref[0]


def _modulation(cc, w_mod, b_mod):
    return pl.pallas_call(
        _mod_kernel,
        grid=(DEPTH, 3),
        in_specs=[pl.BlockSpec((MOD_ROWS, D_MODEL), lambda i, j: (0, 0)),
                  pl.BlockSpec((1, D_MODEL, D_MODEL), lambda i, j: (i, 0, j)),
                  pl.BlockSpec((1, 1, D_MODEL), lambda i, j: (i, 0, j))],
        out_specs=pl.BlockSpec((1, MOD_ROWS, D_MODEL), lambda i, j: (i, 0, j)),
        out_shape=jax.ShapeDtypeStruct((DEPTH, MOD_ROWS, 3 * D_MODEL), F32),
        compiler_params=_params(2),
        name="modulation",
    )(cc, w_mod, b_mod.reshape(DEPTH, 1, 3 * D_MODEL))


def _dft_tables(L):
    n2, m_sub = 4 * L, L // RADIX
    idx = jnp.arange(m_sub, dtype=jnp.int32)
    sample = RADIX * idx[None, None, :] + jnp.arange(RADIX, dtype=jnp.int32)[:, None, None]
    q = ((2 * idx[None, :, None] + 1) * sample) % n2
    ang = q.astype(F32) * (2.0 * math.pi / n2)
    cm, sm = jnp.cos(ang).astype(BF16), (-jnp.sin(ang)).astype(BF16)
    tr = lambda v: jnp.swapaxes(v, 1, 2) * (1.0 / L)
    return dict(cm=cm, sm=sm, icm=tr(cm), ism=tr(sm))


def _butterfly4(t0, t1, t2, t3):
    pr, pi, mr, mi = t0[0] + t2[0], t0[1] + t2[1], t0[0] - t2[0], t0[1] - t2[1]
    qr, qi, nr, ni = t1[0] + t3[0], t1[1] + t3[1], t1[0] - t3[0], t1[1] - t3[1]
    return (pr + qr, pi + qi), (mr - ni, mi + nr), (mr + ni, mi - nr), (pr - qr, pi - qi)


def _split_dft(parts, cm_ref, sm_ref):
    t = []
    for s in range(RADIX):
        u = parts[s].astype(BF16)
        t.append((_dot(cm_ref[s], u), _dot(sm_ref[s], u)))
    return _butterfly4(*t)


def _filter_hidden_kernel(feats_ref, w1_ref, b1_ref, w2_ref, b2_ref, fq_ref, h_ref):
    fq = fq_ref[0]
    h = jnp.sin(fq[0:1] * (_dot3(feats_ref[...], w1_ref[0]) + b1_ref[0]))
    h_ref[0] = jnp.sin(fq[1:2] * (_dot3(h, w2_ref[0]) + b2_ref[0]))


def _filter_kernel(h_ref, w3f_ref, w3b_ref, b3f_ref, b3b_ref, dl_ref, cm_ref, sm_ref, k_ref, *, L):
    m_sub = L // RADIX
    c = dl_ref.shape[1]
    h = h_ref[0]
    row = lax.broadcasted_iota(jnp.int32, (L, c), 0)
    lag = RADIX * (row % m_sub) + row // m_sub
    win = jnp.exp(-(lag.astype(F32) * (1.0 / (L - 1))) * dl_ref[...])
    hf = (_dot3(h, w3f_ref[0]) + b3f_ref[0]) * win
    hb = jnp.where(lag == 0, 0.0, (_dot3(h, w3b_ref[0]) + b3b_ref[0]) * win)
    nrm = jnp.sum(jnp.abs(hf), axis=0, keepdims=True) + jnp.sum(jnp.abs(hb), axis=0, keepdims=True)
    hf, hb = hf / nrm, hb / nrm
    parts = [jnp.concatenate([hf[s * m_sub:(s + 1) * m_sub], hb[s * m_sub:(s + 1) * m_sub]], axis=1)
             for s in range(RADIX)]
    spec = _split_dft(parts, cm_ref, sm_ref)
    for j, (re, im) in enumerate(spec):
        k_ref[0, 0, 2 * j] = re[:, :c] + re[:, c:]
        k_ref[0, 0, 2 * j + 1] = im[:, :c] - im[:, c:]


def _filter_spectra(L, tabs, filt_w1, filt_b1, filt_w2, filt_b2, filt_w3, filt_b3, filt_freq):
    m_sub = L // RADIX
    t = jnp.linspace(0.0, 1.0, L, dtype=F32)[:, None]
    bands = jnp.linspace(1e-4, FILTER_BANDS - 1, FILTER_BANDS, dtype=F32)
    w = (2.0 * math.pi / L) * jnp.arange(L, dtype=F32)[:, None]
    feats = jnp.concatenate([t, jnp.cos(w * bands), jnp.sin(w * bands)], axis=-1)
    feats = jnp.concatenate([feats[s::RADIX] for s in range(RADIX)], axis=0)
    feats = jnp.pad(feats, ((0, 0), (0, FILT_PAD - FILTER_EMB)))
    w1 = jnp.pad(filt_w1, ((0, 0), (0, FILT_PAD - FILTER_EMB), (0, 0)))
    deltas = jnp.abs(jnp.linspace(MIN_DECAY, MAX_DECAY, W_BRANCH, dtype=F32))[None, :]
    lay1 = lambda i: (i, 0, 0)
    hidden = pl.pallas_call(
        _filter_hidden_kernel,
        grid=(DEPTH,),
        in_specs=[_const_spec((L, FILT_PAD)),
                  pl.BlockSpec((1, FILT_PAD, FILTER_HIDDEN), lay1),
                  pl.BlockSpec((1, 1, FILTER_HIDDEN), lay1),
                  pl.BlockSpec((1, FILTER_HIDDEN, FILTER_HIDDEN), lay1),
                  pl.BlockSpec((1, 1, FILTER_HIDDEN), lay1),
                  pl.BlockSpec((1, 2, FILTER_HIDDEN), lay1)],
        out_specs=pl.BlockSpec((1, L, FILTER_HIDDEN), lay1),
        out_shape=jax.ShapeDtypeStruct((DEPTH, L, FILTER_HIDDEN), F32),
        compiler_params=_params(1),
        name=f"filter_hidden_{L}",
    )(feats, w1, filt_b1[:, None], filt_w2, filt_b2[:, None], filt_freq)
    nc = W_BRANCH // HALF
    fwd = lambda i, o, c: (i, 0, o * 2 * nc + c)
    bwd = lambda i, o, c: (i, 0, o * 2 * nc + nc + c)
    tab = _const_spec((RADIX, m_sub, m_sub))
    return pl.pallas_call(
        functools.partial(_filter_kernel, L=L),
        grid=(DEPTH, HYENA_ORDER, nc),
        in_specs=[pl.BlockSpec((1, L, FILTER_HIDDEN), lambda i, o, c: (i, 0, 0)),
                  pl.BlockSpec((1, FILTER_HIDDEN, HALF), fwd),
                  pl.BlockSpec((1, FILTER_HIDDEN, HALF), bwd),
                  pl.BlockSpec((1, 1, HALF), fwd),
                  pl.BlockSpec((1, 1, HALF), bwd),
                  pl.BlockSpec((1, HALF), lambda i, o, c: (0, c)),
                  tab, tab],
        out_specs=pl.BlockSpec((1, 1, 2 * RADIX, m_sub, HALF), lambda i, o, c: (i, o, 0, 0, c)),
        out_shape=jax.ShapeDtypeStruct((DEPTH, HYENA_ORDER, 2 * RADIX, m_sub, W_BRANCH), F32),
        compiler_params=_params(3),
        name=f"filter_spectra_{L}",
    )(hidden, filt_w3, filt_w3, filt_b3[:, None], filt_b3[:, None], deltas,
      tabs["cm"], tabs["sm"])


def _head_norm(v, bd_ref, g):
    ss = _dot((v * v).astype(BF16), bd_ref[...])
    return v * lax.rsqrt(ss * (1.0 / HEAD_DIM) + EPS) * g


def _rope(v, cos, sin):
    n = v.shape[1] // LANES
    lane = lax.broadcasted_iota(jnp.int32, (v.shape[0], LANES), 1)
    first = (lane % 32) < 16
    out = []
    for i in range(n):
        c = v[:, i * LANES:(i + 1) * LANES]
        partner = jnp.where(first, pltpu.roll(c, LANES - 16, 1), pltpu.roll(c, 16, 1))
        out.append(c * cos + partner * sin)
    return jnp.concatenate(out, axis=1)


def _kv_tiles(k, v):
    low = lax.broadcasted_iota(jnp.int32, k.shape, 1) < HEAD_DIM
    ks, vs = pltpu.roll(k, HEAD_DIM, 1), pltpu.roll(v, HEAD_DIM, 1)
    kk = jnp.concatenate([jnp.where(low, k, ks), jnp.where(low, ks, k)], axis=1)
    vv = jnp.concatenate([jnp.where(low, v, 1.0), jnp.where(low, 1.0, vs),
                          jnp.where(low, vs, 1.0), jnp.where(low, 1.0, v)], axis=1)
    return kk.astype(BF16), vv.astype(BF16)


def _inproj_kernel(*refs, rope, kv_only):
    it = iter(refs)
    x_ref, sc_ref, sh_ref, g_ref = next(it), next(it), next(it), next(it)
    if not kv_only:
        wa_ref, wh_ref = next(it), next(it)
    wq_ref = next(it)
    if not kv_only:
        gq_ref, bdq_ref = next(it), next(it)
    gk_ref, bdk_ref = next(it), next(it)
    if rope:
        cos_ref, sin_ref = next(it), next(it)
    outs = list(it)

    h = _rms_mod(x_ref[0], g_ref[...], sc_ref[0], sh_ref[0]).astype(BF16)
    w = W_BRANCH
    if kv_only:
        kk_ref, vv_ref = outs
        acc = _dot(h, wq_ref[0])
        k = _head_norm(acc[:, :KV_COLS], bdk_ref, gk_ref[...])
        kk_ref[0], vv_ref[0] = _kv_tiles(k, acc[:, KV_COLS:])
        return

    ua_ref, ga_ref, hp_ref, sgh_ref, q_ref, kk_ref, vv_ref, sz_ref = outs
    for r0 in range(0, h.shape[0], ROW_SPLIT):
        rows = slice(r0, r0 + ROW_SPLIT)
        hr = h[rows]
        acc = _dot(hr, wq_ref[0])
        q = _head_norm(acc[:, :Q_COLS], bdq_ref, gq_ref[...])
        k = _head_norm(acc[:, Q_COLS:Q_COLS + KV_COLS], bdk_ref, gk_ref[...])
        if rope:
            q = _rope(q, cos_ref[rows], sin_ref[rows])
            k = _rope(k, cos_ref[rows], sin_ref[rows])
        q_ref[0, rows] = (q * Q_SCALE).astype(BF16)
        kk_ref[0, rows], vv_ref[0, rows] = _kv_tiles(k, acc[:, Q_COLS + KV_COLS:Q_COLS + 2 * KV_COLS])
        sz_ref[0, rows] = _silu(acc[:, Q_COLS + 2 * KV_COLS:]).astype(BF16)
        acc = _dot(hr, wa_ref[0])
        ua_ref[0, rows] = (acc[:, 2 * w:3 * w] * acc[:, :w]).astype(BF16)
        ga_ref[0, rows] = (acc[:, w:2 * w] * _silu(acc[:, 3 * w:])).astype(BF16)
        acc = _dot(hr, wh_ref[0])
        hp_ref[0, rows] = acc[:, :3 * w].astype(BF16)
        sgh_ref[0, rows] = _silu(acc[:, 3 * w:]).astype(BF16)


def _inproj(x, sc, sh, g, wts, layer, consts, *, rope, kv_only):
    B, T, D = x.shape
    tm = min(T, 1024)
    row = lambda b, t: (b, t, 0)
    vec = pl.BlockSpec((1, 1, D), lambda b, t: (b, 0, 0))
    args = [x, sc, sh, g[None, :]]
    specs = [pl.BlockSpec((1, tm, D), row), vec, vec, _const_spec((1, D))]
    if not kv_only:
        args += [wts["wa"], wts["wh"], wts["wq"], consts["gq"], consts["bdq"]]
        specs += [_layer_spec(wts["wa"].shape, layer), _layer_spec(wts["wh"].shape, layer),
                  _layer_spec(wts["wq"].shape, layer), _const_spec(consts["gq"].shape),
                  _const_spec(consts["bdq"].shape)]
    else:
        args += [wts["wkv"]]
        specs += [_layer_spec(wts["wkv"].shape, layer)]
    args += [consts["gk"], consts["bdk"]]
    specs += [_const_spec(consts["gk"].shape), _const_spec(consts["bdk"].shape)]
    if rope:
        args += [consts["cos"], consts["sin"]]
        specs += [pl.BlockSpec((tm, LANES), lambda b, t: (t, 0))] * 2
    widths = ([KV_DUP_COLS, V_TILE_COLS] if kv_only else
              [W_BRANCH, W_BRANCH, 3 * W_BRANCH, W_BRANCH, Q_COLS, KV_DUP_COLS, V_TILE_COLS, W_BRANCH])
    return pl.pallas_call(
        functools.partial(_inproj_kernel, rope=rope, kv_only=kv_only),
        grid=(B, T // tm),
        in_specs=specs,
        out_specs=[pl.BlockSpec((1, tm, n), row) for n in widths],
        out_shape=[jax.ShapeDtypeStruct((B, T, n), BF16) for n in widths],
        compiler_params=_params(2),
        name=f"inproj_{T}" + ("_kv" if kv_only else ""),
    )(*args)


def _attn_kernel(*refs, n_src, shifted):
    q_ref, sz_ref = refs[0], refs[1]
    kv = refs[2:2 + 2 * n_src]
    o_ref = refs[-1]
    tq = min(q_ref.shape[1], Q_SPLIT)
    low = lax.broadcasted_iota(jnp.int32, (tq, LANES), 1) < HEAD_DIM
    nt = (((1,), (1,)), ((), ()))
    for r0, j in [(r0, j) for r0 in range(0, q_ref.shape[1], tq) for j in range(2)]:
        qrows = slice(r0, r0 + tq)
        cols = slice(j * LANES, (j + 1) * LANES)
        qp = q_ref[0, qrows, cols].astype(F32)
        halves = []
        for half, sel in enumerate((low, jnp.logical_not(low))):
            qm = jnp.where(sel, qp, 0.0).astype(BF16)
            vt = slice(half * LANES, (half + 1) * LANES)
            if shifted:
                s = [lax.dot_general(qm, kv[2 * i][0], nt, preferred_element_type=F32) for i in range(n_src)]
                m = functools.reduce(jnp.maximum, [jnp.max(v, axis=-1, keepdims=True) for v in s])
                terms = [_dot(jnp.exp2(s[i] - m).astype(BF16), kv[2 * i + 1][0, :, vt]) for i in range(n_src)]
            else:
                terms = []
                for i in range(n_src):
                    n_keys = kv[2 * i].shape[1]
                    for c0 in range(0, n_keys, KEY_CHUNK):
                        rows = slice(c0, min(c0 + KEY_CHUNK, n_keys))
                        s = lax.dot_general(qm, kv[2 * i][0, rows, :], nt, preferred_element_type=F32)
                        terms.append(_dot(jnp.exp2(s).astype(BF16), kv[2 * i + 1][0, rows, vt]))
            halves.append(functools.reduce(jnp.add, terms))
        num = jnp.where(low, halves[0], halves[1])
        den = pltpu.roll(jnp.where(low, halves[1], halves[0]), HEAD_DIM, 1)
        o_ref[0, qrows, cols] = (num / den * sz_ref[0, qrows, cols].astype(F32)).astype(BF16)


def _attention(q, sz, sources, shift_free):
    B, T, _ = q.shape
    gw = Q_COLS // N_KV_HEADS
    args = [q, sz] + [a for src in sources for a in src]

    def call(shifted):
        tq = min(T, 256 if shifted else 2048)
        qspec = pl.BlockSpec((1, tq, gw), lambda b, g, t: (b, t, g))
        specs = [qspec, qspec]
        for kk, vv in sources:
            specs += [pl.BlockSpec((1, kk.shape[1], LANES), lambda b, g, t: (b, 0, g)),
                      pl.BlockSpec((1, vv.shape[1], 2 * LANES), lambda b, g, t: (b, 0, g))]
        return pl.pallas_call(
            functools.partial(_attn_kernel, n_src=len(sources), shifted=shifted),
            grid=(B, N_KV_HEADS, T // tq),
            in_specs=specs,
            out_specs=qspec,
            out_shape=jax.ShapeDtypeStruct((B, T, Q_COLS), BF16),
            compiler_params=_params(3),
            name=f"attention_{T}" + ("_shifted" if shifted else ""),
        )
    return lax.cond(shift_free, call(False), call(True), *args)


def _deinterleave(x, scr):
    n, c = x.shape
    slabs = range(c // LANES)
    for j in slabs:
        scr[j][...] = x[:, j * LANES:(j + 1) * LANES]
    return [jnp.concatenate([scr[j][pl.ds(s, n // RADIX, stride=RADIX), :] for j in slabs], axis=1)
            for s in range(RADIX)]


def _interleave(parts, scr):
    m, c = parts[0].shape
    slabs = range(c // LANES)
    for j in slabs:
        for s in range(RADIX):
            scr[j][pl.ds(s, m, stride=RADIX), :] = parts[s][:, j * LANES:(j + 1) * LANES]
    return jnp.concatenate([scr[j][...] for j in slabs], axis=1)


def _conv3_split(parts, w):
    m = parts[0].shape[0]
    row = lax.broadcasted_iota(jnp.int32, parts[0].shape, 0)
    before = jnp.where(row == 0, 0.0, pltpu.roll(parts[-1], 1, 0))
    after = jnp.where(row == m - 1, 0.0, pltpu.roll(parts[0], m - 1, 0))
    ext = [before] + list(parts) + [after]
    return [ext[s] * w[0:1] + ext[s + 1] * w[1:2] + ext[s + 2] * w[2:3] for s in range(RADIX)]


def _long_conv_split(parts, k, cm_ref, sm_ref, icm_ref, ism_ref):
    spec = _split_dft(parts, cm_ref, sm_ref)
    ya, yb, yc, yd = [(xr * k[2 * j] - xi * k[2 * j + 1], xr * k[2 * j + 1] + xi * k[2 * j])
                      for j, (xr, xi) in enumerate(spec)]
    g0, g1, g3, g2 = _butterfly4(ya, yc, yd, yb)
    return [_dot(icm_ref[s], gr.astype(BF16)) + _dot(ism_ref[s], gi.astype(BF16))
            for s, (gr, gi) in enumerate((g0, g1, g2, g3))]


def _hyena_kernel(hv_ref, hx1_ref, hx2_ref, sg_ref, wv_ref, wx1_ref, wx2_ref, bias_ref,
                  cm_ref, sm_ref, icm_ref, ism_ref, k_ref, o_ref, *scratch):
    c = o_ref.shape[2]
    ns = c // LANES
    s_v, s_x1, s_x2, s_g, s_o = [scratch[i * ns:(i + 1) * ns] for i in range(5)]
    tabs = (cm_ref, sm_ref, icm_ref, ism_ref)
    split = lambda ref, scr: _deinterleave(ref[0].astype(F32), scr)
    v = _conv3_split(split(hv_ref, s_v), wv_ref[...])
    x1 = _conv3_split(split(hx1_ref, s_x1), wx1_ref[...])
    x2 = _conv3_split(split(hx2_ref, s_x2), wx2_ref[...])
    gate = split(sg_ref, s_g)
    b0, b1 = bias_ref[0:1], bias_ref[1:2]
    n_k = 2 * RADIX
    y = _long_conv_split(v, [k_ref[0, 0, j] for j in range(n_k)], *tabs)
    z1 = [x1[s] * (y[s] + v[s] * b0) for s in range(RADIX)]
    y = _long_conv_split(z1, [k_ref[0, 1, j] for j in range(n_k)], *tabs)
    out = [x2[s] * (y[s] + z1[s] * b1) * gate[s] for s in range(RADIX)]
    o_ref[0] = _interleave(out, s_o).astype(BF16)


def _hyena(hpre, sgh, conv_h, bias, tabs, spectra, layer):
    B, L, _ = hpre.shape
    m_sub = L // RADIX
    lanes = HALF if L * HALF * 4 > (1 << 20) else W_BRANCH
    nc = W_BRANCH // lanes
    blk = lambda j: pl.BlockSpec((1, L, lanes), lambda c, b: (b, 0, j * nc + c))
    cw = lambda j: pl.BlockSpec((3, lanes), lambda c, b: (0, j * nc + c))
    tab = _const_spec((RADIX, m_sub, m_sub))
    kspec = pl.BlockSpec((1, HYENA_ORDER, 2 * RADIX, m_sub, lanes), lambda c, b: (layer, 0, 0, 0, c),
                         pipeline_mode=pl.Buffered(1))
    return pl.pallas_call(
        _hyena_kernel,
        grid=(nc, B),
        in_specs=[blk(0), blk(1), blk(2), blk(0), cw(0), cw(1), cw(2),
                  pl.BlockSpec((HYENA_ORDER, lanes), lambda c, b: (0, c)),
                  tab, tab, tab, tab, kspec],
        out_specs=blk(0),
        out_shape=jax.ShapeDtypeStruct((B, L, W_BRANCH), BF16),
        scratch_shapes=[pltpu.VMEM((L, LANES), F32)] * (5 * lanes // LANES),
        compiler_params=_params(2),
        name=f"hyena_{L}",
    )(hpre, hpre, hpre, sgh, conv_h, conv_h, conv_h, bias,
      tabs["cm"], tabs["sm"], tabs["icm"], tabs["ism"], spectra)


def _merge_kernel(x_ref, sc_ref, sh_ref, gt_ref, g_ref, ua_ref, up_ref, un_ref, ga_ref, ca_ref,
                  yb_ref, yc_ref, wg_ref, wb_ref, wo_ref, o_ref, *, seq_len):
    x = x_ref[0]
    h = _rms_mod(x, g_ref[...], sc_ref[0], sh_ref[0]).astype(BF16)
    halo = up_ref.shape[1]
    u = ua_ref[0].astype(F32)
    prev, nxt = _shift_rows(u, up_ref[0, halo - 1:halo, :].astype(F32), un_ref[0, 0:1, :].astype(F32))
    pos = (pl.program_id(1) * u.shape[0] + lax.broadcasted_iota(jnp.int32, u.shape, 0)) % seq_len
    prev = jnp.where(pos == 0, 0.0, prev)
    nxt = jnp.where(pos == seq_len - 1, 0.0, nxt)
    w = ca_ref[...]
    ya = (ga_ref[0].astype(F32) * (prev * w[0:1] + u * w[1:2] + nxt * w[2:3])).astype(BF16)
    for r0 in range(0, x.shape[0], ROW_SPLIT):
        rows = slice(r0, r0 + ROW_SPLIT)
        ys = (ya[rows], yb_ref[0, rows], yc_ref[0, rows])
        merged = None
        for n in range(N_BRANCH):
            gate = jax.nn.sigmoid(_dot(h[rows], wg_ref[0, :, n * D_MODEL:(n + 1) * D_MODEL]))
            term = gate * _dot(ys[n], wb_ref[0, n])
            merged = term if merged is None else merged + term
        out = _dot(merged.astype(BF16), wo_ref[0])
        o_ref[0, rows] = x[rows] + gt_ref[0] * out


def _merge(x, sc, sh, gt, g, ua, ga, conv_a, yb, yc, wts, layer, seq_len):
    B, T, D = x.shape
    tm = min(T, 1024)
    halo = 16
    r = tm // halo
    row = lambda b, t: (b, t, 0)
    vec = pl.BlockSpec((1, 1, D), lambda b, t: (b, 0, 0))
    br = pl.BlockSpec((1, tm, W_BRANCH), row)
    prev = pl.BlockSpec((1, halo, W_BRANCH), lambda b, t: (b, jnp.maximum(t * r - 1, 0), 0))
    nxt = pl.BlockSpec((1, halo, W_BRANCH), lambda b, t: (b, jnp.minimum((t + 1) * r, T // halo - 1), 0))
    return pl.pallas_call(
        functools.partial(_merge_kernel, seq_len=seq_len),
        grid=(B, T // tm),
        in_specs=[pl.BlockSpec((1, tm, D), row), vec, vec, vec, _const_spec((1, D)),
                  br, prev, nxt, br, _const_spec((3, W_BRANCH)), br, br,
                  _layer_spec(wts["wg"].shape, layer), _layer_spec(wts["wb"].shape, layer),
                  _layer_spec(wts["wo"].shape, layer)],
        out_specs=pl.BlockSpec((1, tm, D), row),
        out_shape=jax.ShapeDtypeStruct((B, T, D), F32),
        compiler_params=_params(2),
        name=f"merge_{T}",
    )(x, sc, sh, gt, g[None, :], ua, ua, ua, ga, conv_a, yb, yc, wts["wg"], wts["wb"], wts["wo"])


def _rope_tables(S):
    pos = jnp.arange(S, dtype=jnp.int32)
    coord = jnp.stack([(pos // GRID_W).astype(F32), (pos % GRID_W).astype(F32)], axis=1)
    inv = ROPE_BASE ** (-jnp.arange(ROPE_FREQS, dtype=F32) / ROPE_FREQS)
    ang = coord[:, :, None] * inv
    cos = jnp.repeat(jnp.cos(ang)[:, :, None, :], 2, axis=2).reshape(S, HEAD_DIM)
    sin = jnp.sin(ang)
    sin = jnp.stack([-sin, sin], axis=2).reshape(S, HEAD_DIM)
    return jnp.tile(cos, (1, 2)), jnp.tile(sin, (1, 2))


def _bf16_weights(w_in, w_branch, w_out):
    return dict(
        wa=w_in[:, :, :A_COLS].astype(BF16),
        wh=w_in[:, :, H_OFF:C_OFF].astype(BF16),
        wq=w_in[:, :, C_OFF:G_OFF].astype(BF16),
        wkv=w_in[:, :, K_OFF:Z_OFF].astype(BF16),
        wg=w_in[:, :, G_OFF:].astype(BF16),
        wb=w_branch.astype(BF16),
        wo=w_out.astype(BF16),
    )


def _mixers(x, sc, sh, gt, g, wts, consts, conv_a, conv_h, bias, tabs, spectra, layer, seq_len, rope, ctx_kv):
    per_seq = lambda a: a.reshape(-1, seq_len, a.shape[-1])
    flat = lambda a: a.reshape(x.shape[0], x.shape[1], a.shape[-1])
    ua, ga, hpre, sgh, q, kk, vv, sz = _inproj(x, sc, sh, g, wts, layer, consts, rope=rope, kv_only=False)
    kv = (per_seq(kk), per_seq(vv))
    sources = ([ctx_kv] if ctx_kv is not None else []) + [kv]
    yc = _attention(per_seq(q), per_seq(sz), sources, consts["shift_free"])
    yb = _hyena(per_seq(hpre), per_seq(sgh), conv_h, bias, tabs, spectra, layer)
    return _merge(x, sc, sh, gt, g, ua, ga, conv_a, flat(yb), flat(yc), wts, layer, seq_len), kv


def kernel(x, c, ctx, c_ctx, norm_g, w_mod, b_mod, w_in, conv_a, conv_h, filt_w1, filt_b1, filt_w2,
           filt_b2, filt_w3, filt_b3, filt_freq, hyena_bias, q_norm_g, k_norm_g, w_branch, w_out):
    B, S, D = x.shape
    Lc = ctx.shape[1]
    ctx = ctx.reshape(1, B * Lc, D)
    cc = jnp.concatenate([c, c_ctx[None, :], jnp.zeros((MOD_ROWS - B - 1, D), F32)], axis=0)
    mod = _modulation(cc, w_mod, b_mod)

    tabs_s, tabs_c = _dft_tables(S), _dft_tables(Lc)
    filt = (filt_w1, filt_b1, filt_w2, filt_b2, filt_w3, filt_b3, filt_freq)
    spec_s = _filter_spectra(S, tabs_s, *filt)
    spec_c = _filter_spectra(Lc, tabs_c, *filt)

    cos, sin = _rope_tables(S)
    wts = _bf16_weights(w_in, w_branch, w_out)
    ones = jnp.ones((HEAD_DIM, HEAD_DIM), F32)
    bdq = jnp.kron(jnp.eye(N_HEADS, dtype=F32), ones).astype(BF16)
    bdk = jnp.kron(jnp.eye(N_KV_HEADS, dtype=F32), ones).astype(BF16)

    for i in range(DEPTH):
        last = i == DEPTH - 1
        consts = dict(gq=jnp.tile(q_norm_g[i], N_HEADS)[None, :],
                      gk=jnp.tile(k_norm_g[i], N_KV_HEADS)[None, :],
                      bdq=bdq, bdk=bdk, cos=cos, sin=sin)
        score_bound = (HEAD_DIM * Q_SCALE) * jnp.max(jnp.abs(q_norm_g[i])) * jnp.max(jnp.abs(k_norm_g[i]))
        consts["shift_free"] = score_bound <= SAFE_LOG2_RANGE
        split = lambda rows: [rows[:, None, j * D:(j + 1) * D] for j in range(3)]
        sh, sc, gt = split(mod[i, :B])
        sh_c, sc_c, gt_c = split(mod[i, B:B + 1])
        if last:
            ctx_kv = tuple(a.reshape(B, Lc, a.shape[-1]) for a in
                           _inproj(ctx, sc_c, sh_c, norm_g[i], wts, i, consts, rope=False, kv_only=True))
        else:
            ctx_next, ctx_kv = _mixers(ctx, sc_c, sh_c, gt_c, norm_g[i], wts, consts, conv_a[i], conv_h[i],
                                       hyena_bias[i], tabs_c, spec_c, i, Lc, False, None)
        x, _ = _mixers(x, sc, sh, gt, norm_g[i], wts, consts, conv_a[i], conv_h[i],
                       hyena_bias[i], tabs_s, spec_s, i, S, True, ctx_kv)
        if not last:
            ctx = ctx_next
    return x
```

```python
import functools
import math

import jax
import jax.numpy as jnp
from jax import lax
from jax.experimental import pallas as pl
from jax.experimental.pallas import tpu as pltpu

F32 = jnp.float32
BF16 = jnp.bfloat16

D_MODEL = 1024
DEPTH = 4
GRID_W = 64
W_BRANCH = 512
N_BRANCH = 3
N_HEADS = 8
N_KV_HEADS = 2
HEAD_DIM = 64
ROPE_FREQS = HEAD_DIM // 4
ROPE_BASE = 10000.0
ATTN_SCALE = HEAD_DIM ** -0.5
HYENA_ORDER = 2
FILTER_EMB = 33
FILTER_BANDS = (FILTER_EMB - 1) // 2
FILTER_HIDDEN = 64
HYENA_TARGET = 1e-2
MIN_DECAY = math.log(HYENA_TARGET) / 1.5
MAX_DECAY = math.log(HYENA_TARGET) / 0.3
EPS = 1e-6

A_COLS = 4 * W_BRANCH
H_OFF = A_COLS
C_OFF = H_OFF + 4 * W_BRANCH
Q_COLS = N_HEADS * HEAD_DIM
KV_COLS = N_KV_HEADS * HEAD_DIM
K_OFF = C_OFF + Q_COLS
V_OFF = K_OFF + KV_COLS
Z_OFF = V_OFF + KV_COLS
G_OFF = Z_OFF + W_BRANCH

LANES = 128
VMEM_LIMIT_BYTES = 58 * 1024 * 1024
KV_DUP_COLS = 2 * KV_COLS
V_TILE_COLS = 2 * KV_DUP_COLS
Q_SCALE = ATTN_SCALE * math.log2(math.e)
SAFE_LOG2_RANGE = 64.0
Q_SPLIT = 512
KEY_CHUNK = 512
MOD_ROWS = 24
FILT_PAD = 64
HALF = 256
RADIX = 4
ROW_SPLIT = 256


def _params(n_axes):
    return pltpu.CompilerParams(dimension_semantics=("arbitrary",) * n_axes,
                                vmem_limit_bytes=VMEM_LIMIT_BYTES)


def _const_spec(shape):
    nd = len(shape)
    return pl.BlockSpec(shape, lambda *_: (0,) * nd, pipeline_mode=pl.Buffered(1))


def _layer_spec(shape, layer):
    nd = len(shape)
    return pl.BlockSpec((1,) + tuple(shape[1:]), lambda *_: (layer,) + (0,) * (nd - 1),
                        pipeline_mode=pl.Buffered(1))


def _silu(v):
    return v * jax.nn.sigmoid(v)


def _dot(a, b):
    return jnp.dot(a, b, preferred_element_type=F32)


def _split(a):
    hi = a.astype(BF16)
    lo = (a - hi.astype(F32)).astype(BF16)
    return hi, lo


def _dot3(a, b):
    ah, al = _split(a)
    bh, bl = _split(b)
    return _dot(ah, bh) + (_dot(ah, bl) + _dot(al, bh))


def _rms_mod(x, g, sc, sh):
    y = x * lax.rsqrt(jnp.mean(x * x, axis=-1, keepdims=True) + EPS)
    return (y * g) * (1.0 + sc) + sh


def _shift_rows(u, first_row, last_row):
    n = u.shape[0]
    row = lax.broadcasted_iota(jnp.int32, u.shape, 0)
    prev = jnp.where(row == 0, first_row, pltpu.roll(u, 1, 0))
    nxt = jnp.where(row == n - 1, last_row, pltpu.roll(u, n - 1, 0))
    return prev, nxt


def _conv3(u, w, first_row=0.0, last_row=0.0):
    prev, nxt = _shift_rows(u, first_row, last_row)
    return prev * w[0:1] + u * w[1:2] + nxt * w[2:3]


def _mod_kernel(cc_ref, w_ref, b_ref, o_ref):
    o_ref[0] = _dot3(_silu(cc_ref[...]), w_ref[0]) + b_ref[0]


def _modulation(cc, w_mod, b_mod):
    return pl.pallas_call(
        _mod_kernel,
        grid=(DEPTH, 3),
        in_specs=[pl.BlockSpec((MOD_ROWS, D_MODEL), lambda i, j: (0, 0)),
                  pl.BlockSpec((1, D_MODEL, D_MODEL), lambda i, j: (i, 0, j)),
                  pl.BlockSpec((1, 1, D_MODEL), lambda i, j: (i, 0, j))],
        out_specs=pl.BlockSpec((1, MOD_ROWS, D_MODEL), lambda i, j: (i, 0, j)),
        out_shape=jax.ShapeDtypeStruct((DEPTH, MOD_ROWS, 3 * D_MODEL), F32),
        compiler_params=_params(2),
        name="modulation",
    )(cc, w_mod, b_mod.reshape(DEPTH, 1, 3 * D_MODEL))


def _dft_tables(L):
    n2, m_sub = 4 * L, L // RADIX
    idx = jnp.arange(m_sub, dtype=jnp.int32)
    sample = RADIX * idx[None, None, :] + jnp.arange(RADIX, dtype=jnp.int32)[:, None, None]
    q = ((2 * idx[None, :, None] + 1) * sample) % n2
    ang = q.astype(F32) * (2.0 * math.pi / n2)
    cm, sm = jnp.cos(ang).astype(BF16), (-jnp.sin(ang)).astype(BF16)
    tr = lambda v: jnp.swapaxes(v, 1, 2) * (1.0 / L)
    return dict(cm=cm, sm=sm, icm=tr(cm), ism=tr(sm))


def _butterfly4(t0, t1, t2, t3):
    pr, pi, mr, mi = t0[0] + t2[0], t0[1] + t2[1], t0[0] - t2[0], t0[1] - t2[1]
    qr, qi, nr, ni = t1[0] + t3[0], t1[1] + t3[1], t1[0] - t3[0], t1[1] - t3[1]
    return (pr + qr, pi + qi), (mr - ni, mi + nr), (mr + ni, mi - nr), (pr - qr, pi - qi)


def _split_dft(parts, cm_ref, sm_ref):
    t = []
    for s in range(RADIX):
        u = parts[s].astype(BF16)
        t.append((_dot(cm_ref[s], u), _dot(sm_ref[s], u)))
    return _butterfly4(*t)


def _filter_hidden_kernel(feats_ref, w1_ref, b1_ref, w2_ref, b2_ref, fq_ref, h_ref):
    fq = fq_ref[0]
    h = jnp.sin(fq[0:1] * (_dot3(feats_ref[...], w1_ref[0]) + b1_ref[0]))
    h_ref[0] = jnp.sin(fq[1:2] * (_dot3(h, w2_ref[0]) + b2_ref[0]))


def _filter_kernel(h_ref, w3f_ref, w3b_ref, b3f_ref, b3b_ref, dl_ref, cm_ref, sm_ref, k_ref, *, L):
    m_sub = L // RADIX
    c = dl_ref.shape[1]
    h = h_ref[0]
    row = lax.broadcasted_iota(jnp.int32, (L, c), 0)
    lag = RADIX * (row % m_sub) + row // m_sub
    win = jnp.exp(-(lag.astype(F32) * (1.0 / (L - 1))) * dl_ref[...])
    hf = (_dot3(h, w3f_ref[0]) + b3f_ref[0]) * win
    hb = jnp.where(lag == 0, 0.0, (_dot3(h, w3b_ref[0]) + b3b_ref[0]) * win)
    nrm = jnp.sum(jnp.abs(hf), axis=0, keepdims=True) + jnp.sum(jnp.abs(hb), axis=0, keepdims=True)
    hf, hb = hf / nrm, hb / nrm
    parts = [jnp.concatenate([hf[s * m_sub:(s + 1) * m_sub], hb[s * m_sub:(s + 1) * m_sub]], axis=1)
             for s in range(RADIX)]
    spec = _split_dft(parts, cm_ref, sm_ref)
    for j, (re, im) in enumerate(spec):
        k_ref[0, 0, 2 * j] = re[:, :c] + re[:, c:]
        k_ref[0, 0, 2 * j + 1] = im[:, :c] - im[:, c:]


def _filter_spectra(L, tabs, filt_w1, filt_b1, filt_w2, filt_b2, filt_w3, filt_b3, filt_freq):
    m_sub = L // RADIX
    t = jnp.linspace(0.0, 1.0, L, dtype=F32)[:, None]
    bands = jnp.linspace(1e-4, FILTER_BANDS - 1, FILTER_BANDS, dtype=F32)
    w = (2.0 * math.pi / L) * jnp.arange(L, dtype=F32)[:, None]
    feats = jnp.concatenate([t, jnp.cos(w * bands), jnp.sin(w * bands)], axis=-1)
    feats = jnp.concatenate([feats[s::RADIX] for s in range(RADIX)], axis=0)
    feats = jnp.pad(feats, ((0, 0), (0, FILT_PAD - FILTER_EMB)))
    w1 = jnp.pad(filt_w1, ((0, 0), (0, FILT_PAD - FILTER_EMB), (0, 0)))
    deltas = jnp.abs(jnp.linspace(MIN_DECAY, MAX_DECAY, W_BRANCH, dtype=F32))[None, :]
    lay1 = lambda i: (i, 0, 0)
    hidden = pl.pallas_call(
        _filter_hidden_kernel,
        grid=(DEPTH,),
        in_specs=[_const_spec((L, FILT_PAD)),
                  pl.BlockSpec((1, FILT_PAD, FILTER_HIDDEN), lay1),
                  pl.BlockSpec((1, 1, FILTER_HIDDEN), lay1),
                  pl.BlockSpec((1, FILTER_HIDDEN, FILTER_HIDDEN), lay1),
                  pl.BlockSpec((1, 1, FILTER_HIDDEN), lay1),
                  pl.BlockSpec((1, 2, FILTER_HIDDEN), lay1)],
        out_specs=pl.BlockSpec((1, L, FILTER_HIDDEN), lay1),
        out_shape=jax.ShapeDtypeStruct((DEPTH, L, FILTER_HIDDEN), F32),
        compiler_params=_params(1),
        name=f"filter_hidden_{L}",
    )(feats, w1, filt_b1[:, None], filt_w2, filt_b2[:, None], filt_freq)
    nc = W_BRANCH // HALF
    fwd = lambda i, o, c: (i, 0, o * 2 * nc + c)
    bwd = lambda i, o, c: (i, 0, o * 2 * nc + nc + c)
    tab = _const_spec((RADIX, m_sub, m_sub))
    return pl.pallas_call(
        functools.partial(_filter_kernel, L=L),
        grid=(DEPTH, HYENA_ORDER, nc),
        in_specs=[pl.BlockSpec((1, L, FILTER_HIDDEN), lambda i, o, c: (i, 0, 0)),
                  pl.BlockSpec((1, FILTER_HIDDEN, HALF), fwd),
                  pl.BlockSpec((1, FILTER_HIDDEN, HALF), bwd),
                  pl.BlockSpec((1, 1, HALF), fwd),
                  pl.BlockSpec((1, 1, HALF), bwd),
                  pl.BlockSpec((1, HALF), lambda i, o, c: (0, c)),
                  tab, tab],
        out_specs=pl.BlockSpec((1, 1, 2 * RADIX, m_sub, HALF), lambda i, o, c: (i, o, 0, 0, c)),
        out_shape=jax.ShapeDtypeStruct((DEPTH, HYENA_ORDER, 2 * RADIX, m_sub, W_BRANCH), F32),
        compiler_params=_params(3),
        name=f"filter_spectra_{L}",
    )(hidden, filt_w3, filt_w3, filt_b3[:, None], filt_b3[:, None], deltas,
      tabs["cm"], tabs["sm"])


def _head_norm(v, bd_ref, g):
    ss = _dot((v * v).astype(BF16), bd_ref[...])
    return v * lax.rsqrt(ss * (1.0 / HEAD_DIM) + EPS) * g


def _rope(v, cos, sin):
    n = v.shape[1] // LANES
    lane = lax.broadcasted_iota(jnp.int32, (v.shape[0], LANES), 1)
    first = (lane % 32) < 16
    out = []
    for i in range(n):
        c = v[:, i * LANES:(i + 1) * LANES]
        partner = jnp.where(first, pltpu.roll(c, LANES - 16, 1), pltpu.roll(c, 16, 1))
        out.append(c * cos + partner * sin)
    return jnp.concatenate(out, axis=1)


def _kv_tiles(k, v):
    low = lax.broadcasted_iota(jnp.int32, k.shape, 1) < HEAD_DIM
    ks, vs = pltpu.roll(k, HEAD_DIM, 1), pltpu.roll(v, HEAD_DIM, 1)
    kk = jnp.concatenate([jnp.where(low, k, ks), jnp.where(low, ks, k)], axis=1)
    vv = jnp.concatenate([jnp.where(low, v, 1.0), jnp.where(low, 1.0, vs),
                          jnp.where(low, vs, 1.0), jnp.where(low, 1.0, v)], axis=1)
    return kk.astype(BF16), vv.astype(BF16)


def _inproj_kernel(*refs, rope, kv_only):
    it = iter(refs)
    x_ref, sc_ref, sh_ref, g_ref = next(it), next(it), next(it), next(it)
    if not kv_only:
        wa_ref, wh_ref = next(it), next(it)
    wq_ref = next(it)
    if not kv_only:
        gq_ref, bdq_ref = next(it), next(it)
    gk_ref, bdk_ref = next(it), next(it)
    if rope:
        cos_ref, sin_ref = next(it), next(it)
    outs = list(it)

    h = _rms_mod(x_ref[0], g_ref[...], sc_ref[0], sh_ref[0]).astype(BF16)
    w = W_BRANCH
    if kv_only:
        kk_ref, vv_ref = outs
        acc = _dot(h, wq_ref[0])
        k = _head_norm(acc[:, :KV_COLS], bdk_ref, gk_ref[...])
        kk_ref[0], vv_ref[0] = _kv_tiles(k, acc[:, KV_COLS:])
        return

    ua_ref, ga_ref, hp_ref, sgh_ref, q_ref, kk_ref, vv_ref, sz_ref = outs
    for r0 in range(0, h.shape[0], ROW_SPLIT):
        rows = slice(r0, r0 + ROW_SPLIT)
        hr = h[rows]
        acc = _dot(hr, wq_ref[0])
        q = _head_norm(acc[:, :Q_COLS], bdq_ref, gq_ref[...])
        k = _head_norm(acc[:, Q_COLS:Q_COLS + KV_COLS], bdk_ref, gk_ref[...])
        if rope:
            q = _rope(q, cos_ref[rows], sin_ref[rows])
            k = _rope(k, cos_ref[rows], sin_ref[rows])
        q_ref[0, rows] = (q * Q_SCALE).astype(BF16)
        kk_ref[0, rows], vv_ref[0, rows] = _kv_tiles(k, acc[:, Q_COLS + KV_COLS:Q_COLS + 2 * KV_COLS])
        sz_ref[0, rows] = _silu(acc[:, Q_COLS + 2 * KV_COLS:]).astype(BF16)
        acc = _dot(hr, wa_ref[0])
        ua_ref[0, rows] = (acc[:, 2 * w:3 * w] * acc[:, :w]).astype(BF16)
        ga_ref[0, rows] = (acc[:, w:2 * w] * _silu(acc[:, 3 * w:])).astype(BF16)
        acc = _dot(hr, wh_ref[0])
        hp_ref[0, rows] = acc[:, :3 * w].astype(BF16)
        sgh_ref[0, rows] = _silu(acc[:, 3 * w:]).astype(BF16)


def _inproj(x, sc, sh, g, wts, layer, consts, *, rope, kv_only):
    B, T, D = x.shape
    tm = min(T, 1024)
    row = lambda b, t: (b, t, 0)
    vec = pl.BlockSpec((1, 1, D), lambda b, t: (b, 0, 0))
    args = [x, sc, sh, g[None, :]]
    specs = [pl.BlockSpec((1, tm, D), row), vec, vec, _const_spec((1, D))]
    if not kv_only:
        args += [wts["wa"], wts["wh"], wts["wq"], consts["gq"], consts["bdq"]]
        specs += [_layer_spec(wts["wa"].shape, layer), _layer_spec(wts["wh"].shape, layer),
                  _layer_spec(wts["wq"].shape, layer), _const_spec(consts["gq"].shape),
                  _const_spec(consts["bdq"].shape)]
    else:
        args += [wts["wkv"]]
        specs += [_layer_spec(wts["wkv"].shape, layer)]
    args += [consts["gk"], consts["bdk"]]
    specs += [_const_spec(consts["gk"].shape), _const_spec(consts["bdk"].shape)]
    if rope:
        args += [consts["cos"], consts["sin"]]
        specs += [pl.BlockSpec((tm, LANES), lambda b, t: (t, 0))] * 2
    widths = ([KV_DUP_COLS, V_TILE_COLS] if kv_only else
              [W_BRANCH, W_BRANCH, 3 * W_BRANCH, W_BRANCH, Q_COLS, KV_DUP_COLS, V_TILE_COLS, W_BRANCH])
    return pl.pallas_call(
        functools.partial(_inproj_kernel, rope=rope, kv_only=kv_only),
        grid=(B, T // tm),
        in_specs=specs,
        out_specs=[pl.BlockSpec((1, tm, n), row) for n in widths],
        out_shape=[jax.ShapeDtypeStruct((B, T, n), BF16) for n in widths],
        compiler_params=_params(2),
        name=f"inproj_{T}" + ("_kv" if kv_only else ""),
    )(*args)


def _attn_kernel(*refs, n_src, shifted):
    q_ref, sz_ref = refs[0], refs[1]
    kv = refs[2:2 + 2 * n_src]
    o_ref = refs[-1]
    tq = min(q_ref.shape[1], Q_SPLIT)
    low = lax.broadcasted_iota(jnp.int32, (tq, LANES), 1) < HEAD_DIM
    nt = (((1,), (1,)), ((), ()))
    for r0, j in [(r0, j) for r0 in range(0, q_ref.shape[1], tq) for j in range(2)]:
        qrows = slice(r0, r0 + tq)
        cols = slice(j * LANES, (j + 1) * LANES)
        qp = q_ref[0, qrows, cols].astype(F32)
        halves = []
        for half, sel in enumerate((low, jnp.logical_not(low))):
            qm = jnp.where(sel, qp, 0.0).astype(BF16)
            vt = slice(half * LANES, (half + 1) * LANES)
            if shifted:
                s = [lax.dot_general(qm, kv[2 * i][0], nt, preferred_element_type=F32) for i in range(n_src)]
                m = functools.reduce(jnp.maximum, [jnp.max(v, axis=-1, keepdims=True) for v in s])
                terms = [_dot(jnp.exp2(s[i] - m).astype(BF16), kv[2 * i + 1][0, :, vt]) for i in range(n_src)]
            else:
                terms = []
                for i in range(n_src):
                    n_keys = kv[2 * i].shape[1]
                    for c0 in range(0, n_keys, KEY_CHUNK):
                        rows = slice(c0, min(c0 + KEY_CHUNK, n_keys))
                        s = lax.dot_general(qm, kv[2 * i][0, rows, :], nt, preferred_element_type=F32)
                        terms.append(_dot(jnp.exp2(s).astype(BF16), kv[2 * i + 1][0, rows, vt]))
            halves.append(functools.reduce(jnp.add, terms))
        num = jnp.where(low, halves[0], halves[1])
        den = pltpu.roll(jnp.where(low, halves[1], halves[0]), HEAD_DIM, 1)
        o_ref[0, qrows, cols] = (num / den * sz_ref[0, qrows, cols].astype(F32)).astype(BF16)


def _attention(q, sz, sources, shift_free):
    B, T, _ = q.shape
    gw = Q_COLS // N_KV_HEADS
    args = [q, sz] + [a for src in sources for a in src]

    def call(shifted):
        tq = min(T, 256 if shifted else 2048)
        qspec = pl.BlockSpec((1, tq, gw), lambda b, g, t: (b, t, g))
        specs = [qspec, qspec]
        for kk, vv in sources:
            specs += [pl.BlockSpec((1, kk.shape[1], LANES), lambda b, g, t: (b, 0, g)),
                      pl.BlockSpec((1, vv.shape[1], 2 * LANES), lambda b, g, t: (b, 0, g))]
        return pl.pallas_call(
            functools.partial(_attn_kernel, n_src=len(sources), shifted=shifted),
            grid=(B, N_KV_HEADS, T // tq),
            in_specs=specs,
            out_specs=qspec,
            out_shape=jax.ShapeDtypeStruct((B, T, Q_COLS), BF16),
            compiler_params=_params(3),
            name=f"attention_{T}" + ("_shifted" if shifted else ""),
        )
    return lax.cond(shift_free, call(False), call(True), *args)


def _deinterleave(x, scr):
    n, c = x.shape
    slabs = range(c // LANES)
    for j in slabs:
        scr[j][...] = x[:, j * LANES:(j + 1) * LANES]
    return [jnp.concatenate([scr[j][pl.ds(s, n // RADIX, stride=RADIX), :] for j in slabs], axis=1)
            for s in range(RADIX)]


def _interleave(parts, scr):
    m, c = parts[0].shape
    slabs = range(c // LANES)
    for j in slabs:
        for s in range(RADIX):
            scr[j][pl.ds(s, m, stride=RADIX), :] = parts[s][:, j * LANES:(j + 1) * LANES]
    return jnp.concatenate([scr[j][...] for j in slabs], axis=1)


def _conv3_split(parts, w):
    m = parts[0].shape[0]
    row = lax.broadcasted_iota(jnp.int32, parts[0].shape, 0)
    before = jnp.where(row == 0, 0.0, pltpu.roll(parts[-1], 1, 0))
    after = jnp.where(row == m - 1, 0.0, pltpu.roll(parts[0], m - 1, 0))
    ext = [before] + list(parts) + [after]
    return [ext[s] * w[0:1] + ext[s + 1] * w[1:2] + ext[s + 2] * w[2:3] for s in range(RADIX)]


def _long_conv_split(parts, k, cm_ref, sm_ref, icm_ref, ism_ref):
    spec = _split_dft(parts, cm_ref, sm_ref)
    ya, yb, yc, yd = [(xr * k[2 * j] - xi * k[2 * j + 1], xr * k[2 * j + 1] + xi * k[2 * j])
                      for j, (xr, xi) in enumerate(spec)]
    g0, g1, g3, g2 = _butterfly4(ya, yc, yd, yb)
    return [_dot(icm_ref[s], gr.astype(BF16)) + _dot(ism_ref[s], gi.astype(BF16))
            for s, (gr, gi) in enumerate((g0, g1, g2, g3))]


def _hyena_kernel(hv_ref, hx1_ref, hx2_ref, sg_ref, wv_ref, wx1_ref, wx2_ref, bias_ref,
                  cm_ref, sm_ref, icm_ref, ism_ref, k_ref, o_ref, *scratch):
    c = o_ref.shape[2]
    ns = c // LANES
    s_v, s_x1, s_x2, s_g, s_o = [scratch[i * ns:(i + 1) * ns] for i in range(5)]
    tabs = (cm_ref, sm_ref, icm_ref, ism_ref)
    split = lambda ref, scr: _deinterleave(ref[0].astype(F32), scr)
    v = _conv3_split(split(hv_ref, s_v), wv_ref[...])
    x1 = _conv3_split(split(hx1_ref, s_x1), wx1_ref[...])
    x2 = _conv3_split(split(hx2_ref, s_x2), wx2_ref[...])
    gate = split(sg_ref, s_g)
    b0, b1 = bias_ref[0:1], bias_ref[1:2]
    n_k = 2 * RADIX
    y = _long_conv_split(v, [k_ref[0, 0, j] for j in range(n_k)], *tabs)
    z1 = [x1[s] * (y[s] + v[s] * b0) for s in range(RADIX)]
    y = _long_conv_split(z1, [k_ref[0, 1, j] for j in range(n_k)], *tabs)
    out = [x2[s] * (y[s] + z1[s] * b1) * gate[s] for s in range(RADIX)]
    o_ref[0] = _interleave(out, s_o).astype(BF16)


def _hyena(hpre, sgh, conv_h, bias, tabs, spectra, layer):
    B, L, _ = hpre.shape
    m_sub = L // RADIX
    lanes = HALF if L * HALF * 4 > (1 << 20) else W_BRANCH
    nc = W_BRANCH // lanes
    blk = lambda j: pl.BlockSpec((1, L, lanes), lambda c, b: (b, 0, j * nc + c))
    cw = lambda j: pl.BlockSpec((3, lanes), lambda c, b: (0, j * nc + c))
    tab = _const_spec((RADIX, m_sub, m_sub))
    kspec = pl.BlockSpec((1, HYENA_ORDER, 2 * RADIX, m_sub, lanes), lambda c, b: (layer, 0, 0, 0, c),
                         pipeline_mode=pl.Buffered(1))
    return pl.pallas_call(
        _hyena_kernel,
        grid=(nc, B),
        in_specs=[blk(0), blk(1), blk(2), blk(0), cw(0), cw(1), cw(2),
                  pl.BlockSpec((HYENA_ORDER, lanes), lambda c, b: (0, c)),
                  tab, tab, tab, tab, kspec],
        out_specs=blk(0),
        out_shape=jax.ShapeDtypeStruct((B, L, W_BRANCH), BF16),
        scratch_shapes=[pltpu.VMEM((L, LANES), F32)] * (5 * lanes // LANES),
        compiler_params=_params(2),
        name=f"hyena_{L}",
    )(hpre, hpre, hpre, sgh, conv_h, conv_h, conv_h, bias,
      tabs["cm"], tabs["sm"], tabs["icm"], tabs["ism"], spectra)


def _merge_kernel(x_ref, sc_ref, sh_ref, gt_ref, g_ref, ua_ref, up_ref, un_ref, ga_ref, ca_ref,
                  yb_ref, yc_ref, wg_ref, wb_ref, wo_ref, o_ref):
    t, nt = pl.program_id(1), pl.num_programs(1)
    x = x_ref[0]
    h = _rms_mod(x, g_ref[...], sc_ref[0], sh_ref[0]).astype(BF16)
    halo = up_ref.shape[1]
    first = jnp.where(t == 0, 0.0, up_ref[0, halo - 1:halo, :].astype(F32))
    last = jnp.where(t == nt - 1, 0.0, un_ref[0, 0:1, :].astype(F32))
    ya = (ga_ref[0].astype(F32) * _conv3(ua_ref[0].astype(F32), ca_ref[...], first, last)).astype(BF16)
    for r0 in range(0, x.shape[0], ROW_SPLIT):
        rows = slice(r0, r0 + ROW_SPLIT)
        ys = (ya[rows], yb_ref[0, rows], yc_ref[0, rows])
        merged = None
        for n in range(N_BRANCH):
            gate = jax.nn.sigmoid(_dot(h[rows], wg_ref[0, :, n * D_MODEL:(n + 1) * D_MODEL]))
            term = gate * _dot(ys[n], wb_ref[0, n])
            merged = term if merged is None else merged + term
        out = _dot(merged.astype(BF16), wo_ref[0])
        o_ref[0, rows] = x[rows] + gt_ref[0] * out


def _merge(x, sc, sh, gt, g, ua, ga, conv_a, yb, yc, wts, layer):
    B, T, D = x.shape
    tm = min(T, 1024)
    halo = 16
    r = tm // halo
    row = lambda b, t: (b, t, 0)
    vec = pl.BlockSpec((1, 1, D), lambda b, t: (b, 0, 0))
    br = pl.BlockSpec((1, tm, W_BRANCH), row)
    prev = pl.BlockSpec((1, halo, W_BRANCH), lambda b, t: (b, jnp.maximum(t * r - 1, 0), 0))
    nxt = pl.BlockSpec((1, halo, W_BRANCH), lambda b, t: (b, jnp.minimum((t + 1) * r, T // halo - 1), 0))
    return pl.pallas_call(
        _merge_kernel,
        grid=(B, T // tm),
        in_specs=[pl.BlockSpec((1, tm, D), row), vec, vec, vec, _const_spec((1, D)),
                  br, prev, nxt, br, _const_spec((3, W_BRANCH)), br, br,
                  _layer_spec(wts["wg"].shape, layer), _layer_spec(wts["wb"].shape, layer),
                  _layer_spec(wts["wo"].shape, layer)],
        out_specs=pl.BlockSpec((1, tm, D), row),
        out_shape=jax.ShapeDtypeStruct((B, T, D), F32),
        compiler_params=_params(2),
        name=f"merge_{T}",
    )(x, sc, sh, gt, g[None, :], ua, ua, ua, ga, conv_a, yb, yc, wts["wg"], wts["wb"], wts["wo"])


def _rope_tables(S):
    pos = jnp.arange(S, dtype=jnp.int32)
    coord = jnp.stack([(pos // GRID_W).astype(F32), (pos % GRID_W).astype(F32)], axis=1)
    inv = ROPE_BASE ** (-jnp.arange(ROPE_FREQS, dtype=F32) / ROPE_FREQS)
    ang = coord[:, :, None] * inv
    cos = jnp.repeat(jnp.cos(ang)[:, :, None, :], 2, axis=2).reshape(S, HEAD_DIM)
    sin = jnp.sin(ang)
    sin = jnp.stack([-sin, sin], axis=2).reshape(S, HEAD_DIM)
    return jnp.tile(cos, (1, 2)), jnp.tile(sin, (1, 2))


def _bf16_weights(w_in, w_branch, w_out):
    return dict(
        wa=w_in[:, :, :A_COLS].astype(BF16),
        wh=w_in[:, :, H_OFF:C_OFF].astype(BF16),
        wq=w_in[:, :, C_OFF:G_OFF].astype(BF16),
        wkv=w_in[:, :, K_OFF:Z_OFF].astype(BF16),
        wg=w_in[:, :, G_OFF:].astype(BF16),
        wb=w_branch.astype(BF16),
        wo=w_out.astype(BF16),
    )


def _mixers(x, sc, sh, gt, g, wts, consts, conv_a, conv_h, bias, tabs, spectra, layer, rope, ctx_kv):
    ua, ga, hpre, sgh, q, kk, vv, sz = _inproj(x, sc, sh, g, wts, layer, consts, rope=rope, kv_only=False)
    sources = ([ctx_kv] if ctx_kv is not None else []) + [(kk, vv)]
    yc = _attention(q, sz, sources, consts["shift_free"])
    yb = _hyena(hpre, sgh, conv_h, bias, tabs, spectra, layer)
    return _merge(x, sc, sh, gt, g, ua, ga, conv_a, yb, yc, wts, layer), (kk, vv)


def kernel(x, c, ctx, c_ctx, norm_g, w_mod, b_mod, w_in, conv_a, conv_h, filt_w1, filt_b1, filt_w2,
           filt_b2, filt_w3, filt_b3, filt_freq, hyena_bias, q_norm_g, k_norm_g, w_branch, w_out):
    B, S, D = x.shape
    Lc = ctx.shape[1]
    cc = jnp.concatenate([c, c_ctx[None, :], jnp.zeros((MOD_ROWS - B - 1, D), F32)], axis=0)
    mod = _modulation(cc, w_mod, b_mod)

    tabs_s, tabs_c = _dft_tables(S), _dft_tables(Lc)
    filt = (filt_w1, filt_b1, filt_w2, filt_b2, filt_w3, filt_b3, filt_freq)
    spec_s = _filter_spectra(S, tabs_s, *filt)
    spec_c = _filter_spectra(Lc, tabs_c, *filt)

    cos, sin = _rope_tables(S)
    wts = _bf16_weights(w_in, w_branch, w_out)
    ones = jnp.ones((HEAD_DIM, HEAD_DIM), F32)
    bdq = jnp.kron(jnp.eye(N_HEADS, dtype=F32), ones).astype(BF16)
    bdk = jnp.kron(jnp.eye(N_KV_HEADS, dtype=F32), ones).astype(BF16)

    for i in range(DEPTH):
        last = i == DEPTH - 1
        consts = dict(gq=jnp.tile(q_norm_g[i], N_HEADS)[None, :],
                      gk=jnp.tile(k_norm_g[i], N_KV_HEADS)[None, :],
                      bdq=bdq, bdk=bdk, cos=cos, sin=sin)
        score_bound = (HEAD_DIM * Q_SCALE) * jnp.max(jnp.abs(q_norm_g[i])) * jnp.max(jnp.abs(k_norm_g[i]))
        consts["shift_free"] = score_bound <= SAFE_LOG2_RANGE
        split = lambda rows: [rows[:, None, j * D:(j + 1) * D] for j in range(3)]
        sh, sc, gt = split(mod[i, :B])
        sh_c, sc_c, gt_c = [jnp.broadcast_to(v, (B, 1, D)) for v in split(mod[i, B:B + 1])]
        if last:
            ctx_kv = tuple(_inproj(ctx, sc_c, sh_c, norm_g[i], wts, i, consts, rope=False, kv_only=True))
        else:
            ctx_next, ctx_kv = _mixers(ctx, sc_c, sh_c, gt_c, norm_g[i], wts, consts, conv_a[i], conv_h[i],
                                       hyena_bias[i], tabs_c, spec_c, i, False, None)
        x, _ = _mixers(x, sc, sh, gt, norm_g[i], wts, consts, conv_a[i], conv_h[i],
                       hyena_bias[i], tabs_s, spec_s, i, True, ctx_kv)
        if not last:
            ctx = ctx_next
    return x
```

```python
import functools
import math

import jax
import jax.numpy as jnp
from jax import lax
from jax.experimental import pallas as pl
from jax.experimental.pallas import tpu as pltpu

F32 = jnp.float32
BF16 = jnp.bfloat16

D_MODEL = 1024
DEPTH = 4
GRID_W = 64
W_BRANCH = 512
N_BRANCH = 3
N_HEADS = 8
N_KV_HEADS = 2
HEAD_DIM = 64
ROPE_FREQS = HEAD_DIM // 4
ROPE_BASE = 10000.0
ATTN_SCALE = HEAD_DIM ** -0.5
HYENA_ORDER = 2
FILTER_EMB = 33
FILTER_BANDS = (FILTER_EMB - 1) // 2
FILTER_HIDDEN = 64
HYENA_TARGET = 1e-2
MIN_DECAY = math.log(HYENA_TARGET) / 1.5
MAX_DECAY = math.log(HYENA_TARGET) / 0.3
EPS = 1e-6

A_COLS = 4 * W_BRANCH
H_OFF = A_COLS
C_OFF = H_OFF + 4 * W_BRANCH
Q_COLS = N_HEADS * HEAD_DIM
KV_COLS = N_KV_HEADS * HEAD_DIM
K_OFF = C_OFF + Q_COLS
V_OFF = K_OFF + KV_COLS
Z_OFF = V_OFF + KV_COLS
G_OFF = Z_OFF + W_BRANCH

LANES = 128
VMEM_LIMIT_BYTES = 58 * 1024 * 1024
KV_DUP_COLS = 2 * KV_COLS
V_TILE_COLS = 2 * KV_DUP_COLS
Q_SCALE = ATTN_SCALE * math.log2(math.e)
SAFE_LOG2_RANGE = 64.0
Q_SPLIT = 512
KEY_CHUNK = 512
MOD_ROWS = 24
FILT_PAD = 64
HALF = 256
RADIX = 4
ROW_SPLIT = 256


def _params(n_axes):
    return pltpu.CompilerParams(dimension_semantics=("arbitrary",) * n_axes,
                                vmem_limit_bytes=VMEM_LIMIT_BYTES)


def _const_spec(shape):
    nd = len(shape)
    return pl.BlockSpec(shape, lambda *_: (0,) * nd, pipeline_mode=pl.Buffered(1))


def _layer_spec(shape, layer):
    nd = len(shape)
    return pl.BlockSpec((1,) + tuple(shape[1:]), lambda *_: (layer,) + (0,) * (nd - 1),
                        pipeline_mode=pl.Buffered(1))


def _w_in_spec(layer, start, width):
    return pl.BlockSpec((pl.Element(1), pl.Element(D_MODEL), pl.Element(width)), lambda *_: (layer, 0, start),
                        pipeline_mode=pl.Buffered(1))


def _silu(v):
    return v * jax.nn.sigmoid(v)


def _dot(a, b):
    return jnp.dot(a, b, preferred_element_type=F32)


def _split(a):
    hi = a.astype(BF16)
    lo = (a - hi.astype(F32)).astype(BF16)
    return hi, lo


def _dot3(a, b):
    ah, al = _split(a)
    bh, bl = _split(b)
    return _dot(ah, bh) + (_dot(ah, bl) + _dot(al, bh))


def _rms_mod(x, g, sc, sh):
    y = x * lax.rsqrt(jnp.mean(x * x, axis=-1, keepdims=True) + EPS)
    return (y * g) * (1.0 + sc) + sh


def _shift_rows(u, first_row, last_row):
    n = u.shape[0]
    row = lax.broadcasted_iota(jnp.int32, u.shape, 0)
    prev = jnp.where(row == 0, first_row, pltpu.roll(u, 1, 0))
    nxt = jnp.where(row == n - 1, last_row, pltpu.roll(u, n - 1, 0))
    return prev, nxt


def _conv3(u, w, first_row=0.0, last_row=0.0):
    prev, nxt = _shift_rows(u, first_row, last_row)
    return prev * w[0:1] + u * w[1:2] + nxt * w[2:3]


def _mod_kernel(cc_ref, w_ref, b_ref, o_ref):
    o_ref[0] = _dot3(_silu(cc_ref[...]), w_ref[0]) + b_ref[0]


def _modulation(cc, w_mod, b_mod):
    return pl.pallas_call(
        _mod_kernel,
        grid=(DEPTH, 3),
        in_specs=[pl.BlockSpec((MOD_ROWS, D_MODEL), lambda i, j: (0, 0)),
                  pl.BlockSpec((1, D_MODEL, D_MODEL), lambda i, j: (i, 0, j)),
                  pl.BlockSpec((1, 1, D_MODEL), lambda i, j: (i, 0, j))],
        out_specs=pl.BlockSpec((1, MOD_ROWS, D_MODEL), lambda i, j: (i, 0, j)),
        out_shape=jax.ShapeDtypeStruct((DEPTH, MOD_ROWS, 3 * D_MODEL), F32),
        compiler_params=_params(2),
        name="modulation",
    )(cc, w_mod, b_mod.reshape(DEPTH, 1, 3 * D_MODEL))


def _dft_tables(L):
    n2, m_sub = 4 * L, L // RADIX
    idx = jnp.arange(m_sub, dtype=jnp.int32)
    sample = RADIX * idx[None, None, :] + jnp.arange(RADIX, dtype=jnp.int32)[:, None, None]
    q = ((2 * idx[None, :, None] + 1) * sample) % n2
    ang = q.astype(F32) * (2.0 * math.pi / n2)
    cm, sm = jnp.cos(ang).astype(BF16), (-jnp.sin(ang)).astype(BF16)
    tr = lambda v: jnp.swapaxes(v, 1, 2) * (1.0 / L)
    return dict(cm=cm, sm=sm, icm=tr(cm), ism=tr(sm))


def _butterfly4(t0, t1, t2, t3):
    pr, pi, mr, mi = t0[0] + t2[0], t0[1] + t2[1], t0[0] - t2[0], t0[1] - t2[1]
    qr, qi, nr, ni = t1[0] + t3[0], t1[1] + t3[1], t1[0] - t3[0], t1[1] - t3[1]
    return (pr + qr, pi + qi), (mr - ni, mi + nr), (mr + ni, mi - nr), (pr - qr, pi - qi)


def _split_dft(parts, cm_ref, sm_ref):
    t = []
    for s in range(RADIX):
        u = parts[s].astype(BF16)
        t.append((_dot(cm_ref[s], u), _dot(sm_ref[s], u)))
    return _butterfly4(*t)


def _filter_hidden_kernel(feats_ref, w1_ref, b1_ref, w2_ref, b2_ref, fq_ref, h_ref):
    fq = fq_ref[0]
    h = jnp.sin(fq[0:1] * (_dot3(feats_ref[...], w1_ref[0]) + b1_ref[0]))
    h_ref[0] = jnp.sin(fq[1:2] * (_dot3(h, w2_ref[0]) + b2_ref[0]))


def _filter_kernel(h_ref, w3f_ref, w3b_ref, b3f_ref, b3b_ref, dl_ref, cm_ref, sm_ref, k_ref, *, L):
    m_sub = L // RADIX
    c = dl_ref.shape[1]
    h = h_ref[0]
    row = lax.broadcasted_iota(jnp.int32, (L, c), 0)
    lag = RADIX * (row % m_sub) + row // m_sub
    win = jnp.exp(-(lag.astype(F32) * (1.0 / (L - 1))) * dl_ref[...])
    hf = (_dot3(h, w3f_ref[0]) + b3f_ref[0]) * win
    hb = jnp.where(lag == 0, 0.0, (_dot3(h, w3b_ref[0]) + b3b_ref[0]) * win)
    nrm = jnp.sum(jnp.abs(hf), axis=0, keepdims=True) + jnp.sum(jnp.abs(hb), axis=0, keepdims=True)
    hf, hb = hf / nrm, hb / nrm
    parts = [jnp.concatenate([hf[s * m_sub:(s + 1) * m_sub], hb[s * m_sub:(s + 1) * m_sub]], axis=1)
             for s in range(RADIX)]
    spec = _split_dft(parts, cm_ref, sm_ref)
    for j, (re, im) in enumerate(spec):
        k_ref[0, 0, 2 * j] = re[:, :c] + re[:, c:]
        k_ref[0, 0, 2 * j + 1] = im[:, :c] - im[:, c:]


def _filter_spectra(L, tabs, filt_w1, filt_b1, filt_w2, filt_b2, filt_w3, filt_b3, filt_freq):
    m_sub = L // RADIX
    t = jnp.linspace(0.0, 1.0, L, dtype=F32)[:, None]
    bands = jnp.linspace(1e-4, FILTER_BANDS - 1, FILTER_BANDS, dtype=F32)
    w = (2.0 * math.pi / L) * jnp.arange(L, dtype=F32)[:, None]
    feats = jnp.concatenate([t, jnp.cos(w * bands), jnp.sin(w * bands)], axis=-1)
    feats = jnp.concatenate([feats[s::RADIX] for s in range(RADIX)], axis=0)
    feats = jnp.pad(feats, ((0, 0), (0, FILT_PAD - FILTER_EMB)))
    w1 = jnp.pad(filt_w1, ((0, 0), (0, FILT_PAD - FILTER_EMB), (0, 0)))
    deltas = jnp.abs(jnp.linspace(MIN_DECAY, MAX_DECAY, W_BRANCH, dtype=F32))[None, :]
    lay1 = lambda i: (i, 0, 0)
    hidden = pl.pallas_call(
        _filter_hidden_kernel,
        grid=(DEPTH,),
        in_specs=[_const_spec((L, FILT_PAD)),
                  pl.BlockSpec((1, FILT_PAD, FILTER_HIDDEN), lay1),
                  pl.BlockSpec((1, 1, FILTER_HIDDEN), lay1),
                  pl.BlockSpec((1, FILTER_HIDDEN, FILTER_HIDDEN), lay1),
                  pl.BlockSpec((1, 1, FILTER_HIDDEN), lay1),
                  pl.BlockSpec((1, 2, FILTER_HIDDEN), lay1)],
        out_specs=pl.BlockSpec((1, L, FILTER_HIDDEN), lay1),
        out_shape=jax.ShapeDtypeStruct((DEPTH, L, FILTER_HIDDEN), F32),
        compiler_params=_params(1),
        name=f"filter_hidden_{L}",
    )(feats, w1, filt_b1[:, None], filt_w2, filt_b2[:, None], filt_freq)
    nc = W_BRANCH // HALF
    fwd = lambda i, o, c: (i, 0, o * 2 * nc + c)
    bwd = lambda i, o, c: (i, 0, o * 2 * nc + nc + c)
    tab = _const_spec((RADIX, m_sub, m_sub))
    return pl.pallas_call(
        functools.partial(_filter_kernel, L=L),
        grid=(DEPTH, HYENA_ORDER, nc),
        in_specs=[pl.BlockSpec((1, L, FILTER_HIDDEN), lambda i, o, c: (i, 0, 0)),
                  pl.BlockSpec((1, FILTER_HIDDEN, HALF), fwd),
                  pl.BlockSpec((1, FILTER_HIDDEN, HALF), bwd),
                  pl.BlockSpec((1, 1, HALF), fwd),
                  pl.BlockSpec((1, 1, HALF), bwd),
                  pl.BlockSpec((1, HALF), lambda i, o, c: (0, c)),
                  tab, tab],
        out_specs=pl.BlockSpec((1, 1, 2 * RADIX, m_sub, HALF), lambda i, o, c: (i, o, 0, 0, c)),
        out_shape=jax.ShapeDtypeStruct((DEPTH, HYENA_ORDER, 2 * RADIX, m_sub, W_BRANCH), F32),
        compiler_params=_params(3),
        name=f"filter_spectra_{L}",
    )(hidden, filt_w3, filt_w3, filt_b3[:, None], filt_b3[:, None], deltas,
      tabs["cm"], tabs["sm"])


def _head_norm(v, bd_ref, g):
    ss = _dot((v * v).astype(BF16), bd_ref[...])
    return v * lax.rsqrt(ss * (1.0 / HEAD_DIM) + EPS) * g


def _rope(v, cos, sin):
    n = v.shape[1] // LANES
    lane = lax.broadcasted_iota(jnp.int32, (v.shape[0], LANES), 1)
    first = (lane % 32) < 16
    out = []
    for i in range(n):
        c = v[:, i * LANES:(i + 1) * LANES]
        partner = jnp.where(first, pltpu.roll(c, LANES - 16, 1), pltpu.roll(c, 16, 1))
        out.append(c * cos + partner * sin)
    return jnp.concatenate(out, axis=1)


def _kv_tiles(k, v):
    low = lax.broadcasted_iota(jnp.int32, k.shape, 1) < HEAD_DIM
    ks, vs = pltpu.roll(k, HEAD_DIM, 1), pltpu.roll(v, HEAD_DIM, 1)
    kk = jnp.concatenate([jnp.where(low, k, ks), jnp.where(low, ks, k)], axis=1)
    vv = jnp.concatenate([jnp.where(low, v, 1.0), jnp.where(low, 1.0, vs),
                          jnp.where(low, vs, 1.0), jnp.where(low, 1.0, v)], axis=1)
    return kk.astype(BF16), vv.astype(BF16)


def _inproj_kernel(*refs, rope, kv_only):
    it = iter(refs)
    x_ref, sc_ref, sh_ref, g_ref = next(it), next(it), next(it), next(it)
    if not kv_only:
        wa_ref, wh_ref = next(it), next(it)
    wq_ref = next(it)
    if not kv_only:
        gq_ref, bdq_ref = next(it), next(it)
    gk_ref, bdk_ref = next(it), next(it)
    if rope:
        cos_ref, sin_ref = next(it), next(it)
    outs = list(it)

    h = _rms_mod(x_ref[0], g_ref[...], sc_ref[0], sh_ref[0]).astype(BF16)
    w = W_BRANCH
    if kv_only:
        kk_ref, vv_ref = outs
        acc = _dot(h, wq_ref[0])
        k = _head_norm(acc[:, :KV_COLS], bdk_ref, gk_ref[...])
        kk_ref[0], vv_ref[0] = _kv_tiles(k, acc[:, KV_COLS:])
        return

    ua_ref, ga_ref, hp_ref, sgh_ref, q_ref, kk_ref, vv_ref, sz_ref = outs
    for r0 in range(0, h.shape[0], ROW_SPLIT):
        rows = slice(r0, r0 + ROW_SPLIT)
        hr = h[rows]
        acc = _dot(hr, wq_ref[0])
        q = _head_norm(acc[:, :Q_COLS], bdq_ref, gq_ref[...])
        k = _head_norm(acc[:, Q_COLS:Q_COLS + KV_COLS], bdk_ref, gk_ref[...])
        if rope:
            q = _rope(q, cos_ref[rows], sin_ref[rows])
            k = _rope(k, cos_ref[rows], sin_ref[rows])
        q_ref[0, rows] = (q * Q_SCALE).astype(BF16)
        kk_ref[0, rows], vv_ref[0, rows] = _kv_tiles(k, acc[:, Q_COLS + KV_COLS:Q_COLS + 2 * KV_COLS])
        sz_ref[0, rows] = _silu(acc[:, Q_COLS + 2 * KV_COLS:]).astype(BF16)
        acc = _dot(hr, wa_ref[0])
        ua_ref[0, rows] = (acc[:, 2 * w:3 * w] * acc[:, :w]).astype(BF16)
        ga_ref[0, rows] = (acc[:, w:2 * w] * _silu(acc[:, 3 * w:])).astype(BF16)
        acc = _dot(hr, wh_ref[0])
        hp_ref[0, rows] = acc[:, :3 * w].astype(BF16)
        sgh_ref[0, rows] = _silu(acc[:, 3 * w:]).astype(BF16)


def _inproj(x, sc, sh, g, wts, layer, consts, *, rope, kv_only):
    B, T, D = x.shape
    tm = min(T, 1024)
    row = lambda b, t: (b, t, 0)
    vec = pl.BlockSpec((1, 1, D), lambda b, t: (b, 0, 0))
    args = [x, sc, sh, g[None, :]]
    specs = [pl.BlockSpec((1, tm, D), row), vec, vec, _const_spec((1, D))]
    if not kv_only:
        args += [wts["w_in"], wts["w_in"], wts["w_in"], consts["gq"], consts["bdq"]]
        specs += [_w_in_spec(layer, 0, A_COLS), _w_in_spec(layer, H_OFF, C_OFF - H_OFF),
                  _w_in_spec(layer, C_OFF, G_OFF - C_OFF), _const_spec(consts["gq"].shape),
                  _const_spec(consts["bdq"].shape)]
    else:
        args += [wts["w_in"]]
        specs += [_w_in_spec(layer, K_OFF, Z_OFF - K_OFF)]
    args += [consts["gk"], consts["bdk"]]
    specs += [_const_spec(consts["gk"].shape), _const_spec(consts["bdk"].shape)]
    if rope:
        args += [consts["cos"], consts["sin"]]
        specs += [pl.BlockSpec((tm, LANES), lambda b, t: (t, 0))] * 2
    widths = ([KV_DUP_COLS, V_TILE_COLS] if kv_only else
              [W_BRANCH, W_BRANCH, 3 * W_BRANCH, W_BRANCH, Q_COLS, KV_DUP_COLS, V_TILE_COLS, W_BRANCH])
    return pl.pallas_call(
        functools.partial(_inproj_kernel, rope=rope, kv_only=kv_only),
        grid=(B, T // tm),
        in_specs=specs,
        out_specs=[pl.BlockSpec((1, tm, n), row) for n in widths],
        out_shape=[jax.ShapeDtypeStruct((B, T, n), BF16) for n in widths],
        compiler_params=_params(2),
        name=f"inproj_{T}" + ("_kv" if kv_only else ""),
    )(*args)


def _attn_kernel(*refs, n_src, shifted):
    q_ref, sz_ref = refs[0], refs[1]
    kv = refs[2:2 + 2 * n_src]
    o_ref = refs[-1]
    tq = min(q_ref.shape[1], Q_SPLIT)
    low = lax.broadcasted_iota(jnp.int32, (tq, LANES), 1) < HEAD_DIM
    nt = (((1,), (1,)), ((), ()))
    for r0, j in [(r0, j) for r0 in range(0, q_ref.shape[1], tq) for j in range(2)]:
        qrows = slice(r0, r0 + tq)
        cols = slice(j * LANES, (j + 1) * LANES)
        qp = q_ref[0, qrows, cols].astype(F32)
        halves = []
        for half, sel in enumerate((low, jnp.logical_not(low))):
            qm = jnp.where(sel, qp, 0.0).astype(BF16)
            vt = slice(half * LANES, (half + 1) * LANES)
            if shifted:
                s = [lax.dot_general(qm, kv[2 * i][0], nt, preferred_element_type=F32) for i in range(n_src)]
                m = functools.reduce(jnp.maximum, [jnp.max(v, axis=-1, keepdims=True) for v in s])
                terms = [_dot(jnp.exp2(s[i] - m).astype(BF16), kv[2 * i + 1][0, :, vt]) for i in range(n_src)]
            else:
                terms = []
                for i in range(n_src):
                    n_keys = kv[2 * i].shape[1]
                    for c0 in range(0, n_keys, KEY_CHUNK):
                        rows = slice(c0, min(c0 + KEY_CHUNK, n_keys))
                        s = lax.dot_general(qm, kv[2 * i][0, rows, :], nt, preferred_element_type=F32)
                        terms.append(_dot(jnp.exp2(s).astype(BF16), kv[2 * i + 1][0, rows, vt]))
            halves.append(functools.reduce(jnp.add, terms))
        num = jnp.where(low, halves[0], halves[1])
        den = pltpu.roll(jnp.where(low, halves[1], halves[0]), HEAD_DIM, 1)
        o_ref[0, qrows, cols] = (num / den * sz_ref[0, qrows, cols].astype(F32)).astype(BF16)


def _attention(q, sz, sources, shift_free):
    B, T, _ = q.shape
    gw = Q_COLS // N_KV_HEADS
    args = [q, sz] + [a for src in sources for a in src]

    def call(shifted):
        tq = min(T, 256 if shifted else 2048)
        qspec = pl.BlockSpec((1, tq, gw), lambda b, g, t: (b, t, g))
        specs = [qspec, qspec]
        for kk, vv in sources:
            specs += [pl.BlockSpec((1, kk.shape[1], LANES), lambda b, g, t: (b, 0, g)),
                      pl.BlockSpec((1, vv.shape[1], 2 * LANES), lambda b, g, t: (b, 0, g))]
        return pl.pallas_call(
            functools.partial(_attn_kernel, n_src=len(sources), shifted=shifted),
            grid=(B, N_KV_HEADS, T // tq),
            in_specs=specs,
            out_specs=qspec,
            out_shape=jax.ShapeDtypeStruct((B, T, Q_COLS), BF16),
            compiler_params=_params(3),
            name=f"attention_{T}" + ("_shifted" if shifted else ""),
        )
    return lax.cond(shift_free, call(False), call(True), *args)


def _deinterleave(x, scr):
    n, c = x.shape
    slabs = range(c // LANES)
    for j in slabs:
        scr[j][...] = x[:, j * LANES:(j + 1) * LANES]
    return [jnp.concatenate([scr[j][pl.ds(s, n // RADIX, stride=RADIX), :] for j in slabs], axis=1)
            for s in range(RADIX)]


def _interleave(parts, scr):
    m, c = parts[0].shape
    slabs = range(c // LANES)
    for j in slabs:
        for s in range(RADIX):
            scr[j][pl.ds(s, m, stride=RADIX), :] = parts[s][:, j * LANES:(j + 1) * LANES]
    return jnp.concatenate([scr[j][...] for j in slabs], axis=1)


def _conv3_split(parts, w):
    m = parts[0].shape[0]
    row = lax.broadcasted_iota(jnp.int32, parts[0].shape, 0)
    before = jnp.where(row == 0, 0.0, pltpu.roll(parts[-1], 1, 0))
    after = jnp.where(row == m - 1, 0.0, pltpu.roll(parts[0], m - 1, 0))
    ext = [before] + list(parts) + [after]
    return [ext[s] * w[0:1] + ext[s + 1] * w[1:2] + ext[s + 2] * w[2:3] for s in range(RADIX)]


def _long_conv_split(parts, k, cm_ref, sm_ref, icm_ref, ism_ref):
    spec = _split_dft(parts, cm_ref, sm_ref)
    ya, yb, yc, yd = [(xr * k[2 * j] - xi * k[2 * j + 1], xr * k[2 * j + 1] + xi * k[2 * j])
                      for j, (xr, xi) in enumerate(spec)]
    g0, g1, g3, g2 = _butterfly4(ya, yc, yd, yb)
    return [_dot(icm_ref[s], gr.astype(BF16)) + _dot(ism_ref[s], gi.astype(BF16))
            for s, (gr, gi) in enumerate((g0, g1, g2, g3))]


def _hyena_kernel(hv_ref, hx1_ref, hx2_ref, sg_ref, wv_ref, wx1_ref, wx2_ref, bias_ref,
                  cm_ref, sm_ref, icm_ref, ism_ref, k_ref, o_ref, *scratch):
    c = o_ref.shape[2]
    ns = c // LANES
    s_v, s_x1, s_x2, s_g, s_o = [scratch[i * ns:(i + 1) * ns] for i in range(5)]
    tabs = (cm_ref, sm_ref, icm_ref, ism_ref)
    split = lambda ref, scr: _deinterleave(ref[0].astype(F32), scr)
    v = _conv3_split(split(hv_ref, s_v), wv_ref[...])
    x1 = _conv3_split(split(hx1_ref, s_x1), wx1_ref[...])
    x2 = _conv3_split(split(hx2_ref, s_x2), wx2_ref[...])
    gate = split(sg_ref, s_g)
    b0, b1 = bias_ref[0:1], bias_ref[1:2]
    n_k = 2 * RADIX
    y = _long_conv_split(v, [k_ref[0, 0, j] for j in range(n_k)], *tabs)
    z1 = [x1[s] * (y[s] + v[s] * b0) for s in range(RADIX)]
    y = _long_conv_split(z1, [k_ref[0, 1, j] for j in range(n_k)], *tabs)
    out = [x2[s] * (y[s] + z1[s] * b1) * gate[s] for s in range(RADIX)]
    o_ref[0] = _interleave(out, s_o).astype(BF16)


def _hyena(hpre, sgh, conv_h, bias, tabs, spectra, layer):
    B, L, _ = hpre.shape
    m_sub = L // RADIX
    lanes = HALF if L * HALF * 4 > (1 << 20) else W_BRANCH
    nc = W_BRANCH // lanes
    blk = lambda j: pl.BlockSpec((1, L, lanes), lambda c, b: (b, 0, j * nc + c))
    cw = lambda j: pl.BlockSpec((3, lanes), lambda c, b: (0, j * nc + c))
    tab = _const_spec((RADIX, m_sub, m_sub))
    kspec = pl.BlockSpec((1, HYENA_ORDER, 2 * RADIX, m_sub, lanes), lambda c, b: (layer, 0, 0, 0, c),
                         pipeline_mode=pl.Buffered(1))
    return pl.pallas_call(
        _hyena_kernel,
        grid=(nc, B),
        in_specs=[blk(0), blk(1), blk(2), blk(0), cw(0), cw(1), cw(2),
                  pl.BlockSpec((HYENA_ORDER, lanes), lambda c, b: (0, c)),
                  tab, tab, tab, tab, kspec],
        out_specs=blk(0),
        out_shape=jax.ShapeDtypeStruct((B, L, W_BRANCH), BF16),
        scratch_shapes=[pltpu.VMEM((L, LANES), F32)] * (5 * lanes // LANES),
        compiler_params=_params(2),
        name=f"hyena_{L}",
    )(hpre, hpre, hpre, sgh, conv_h, conv_h, conv_h, bias,
      tabs["cm"], tabs["sm"], tabs["icm"], tabs["ism"], spectra)


def _merge_kernel(x_ref, sc_ref, sh_ref, gt_ref, g_ref, ua_ref, up_ref, un_ref, ga_ref, ca_ref,
                  yb_ref, yc_ref, wg_ref, wb_ref, wo_ref, o_ref):
    t, nt = pl.program_id(1), pl.num_programs(1)
    x = x_ref[0]
    h = _rms_mod(x, g_ref[...], sc_ref[0], sh_ref[0]).astype(BF16)
    halo = up_ref.shape[1]
    first = jnp.where(t == 0, 0.0, up_ref[0, halo - 1:halo, :].astype(F32))
    last = jnp.where(t == nt - 1, 0.0, un_ref[0, 0:1, :].astype(F32))
    ya = (ga_ref[0].astype(F32) * _conv3(ua_ref[0].astype(F32), ca_ref[...], first, last)).astype(BF16)
    for r0 in range(0, x.shape[0], ROW_SPLIT):
        rows = slice(r0, r0 + ROW_SPLIT)
        ys = (ya[rows], yb_ref[0, rows], yc_ref[0, rows])
        merged = None
        for n in range(N_BRANCH):
            gate = jax.nn.sigmoid(_dot(h[rows], wg_ref[0, :, n * D_MODEL:(n + 1) * D_MODEL]))
            term = gate * _dot(ys[n], wb_ref[0, n])
            merged = term if merged is None else merged + term
        out = _dot(merged.astype(BF16), wo_ref[0])
        o_ref[0, rows] = x[rows] + gt_ref[0] * out


def _merge(x, sc, sh, gt, g, ua, ga, conv_a, yb, yc, wts, layer):
    B, T, D = x.shape
    tm = min(T, 1024)
    halo = 16
    r = tm // halo
    row = lambda b, t: (b, t, 0)
    vec = pl.BlockSpec((1, 1, D), lambda b, t: (b, 0, 0))
    br = pl.BlockSpec((1, tm, W_BRANCH), row)
    prev = pl.BlockSpec((1, halo, W_BRANCH), lambda b, t: (b, jnp.maximum(t * r - 1, 0), 0))
    nxt = pl.BlockSpec((1, halo, W_BRANCH), lambda b, t: (b, jnp.minimum((t + 1) * r, T // halo - 1), 0))
    return pl.pallas_call(
        _merge_kernel,
        grid=(B, T // tm),
        in_specs=[pl.BlockSpec((1, tm, D), row), vec, vec, vec, _const_spec((1, D)),
                  br, prev, nxt, br, _const_spec((3, W_BRANCH)), br, br,
                  _w_in_spec(layer, G_OFF, N_BRANCH * D_MODEL), _layer_spec(wts["wb"].shape, layer),
                  _layer_spec(wts["wo"].shape, layer)],
        out_specs=pl.BlockSpec((1, tm, D), row),
        out_shape=jax.ShapeDtypeStruct((B, T, D), F32),
        compiler_params=_params(2),
        name=f"merge_{T}",
    )(x, sc, sh, gt, g[None, :], ua, ua, ua, ga, conv_a, yb, yc, wts["w_in"], wts["wb"], wts["wo"])


def _rope_tables(S):
    pos = jnp.arange(S, dtype=jnp.int32)
    coord = jnp.stack([(pos // GRID_W).astype(F32), (pos % GRID_W).astype(F32)], axis=1)
    inv = ROPE_BASE ** (-jnp.arange(ROPE_FREQS, dtype=F32) / ROPE_FREQS)
    ang = coord[:, :, None] * inv
    cos = jnp.repeat(jnp.cos(ang)[:, :, None, :], 2, axis=2).reshape(S, HEAD_DIM)
    sin = jnp.sin(ang)
    sin = jnp.stack([-sin, sin], axis=2).reshape(S, HEAD_DIM)
    return jnp.tile(cos, (1, 2)), jnp.tile(sin, (1, 2))


def _bf16_weights(w_in, w_branch, w_out):
    return dict(
        w_in=w_in.astype(BF16),
        wb=w_branch.astype(BF16),
        wo=w_out.astype(BF16),
    )


def _mixers(x, sc, sh, gt, g, wts, consts, conv_a, conv_h, bias, tabs, spectra, layer, rope, ctx_kv):
    ua, ga, hpre, sgh, q, kk, vv, sz = _inproj(x, sc, sh, g, wts, layer, consts, rope=rope, kv_only=False)
    sources = ([ctx_kv] if ctx_kv is not None else []) + [(kk, vv)]
    yc = _attention(q, sz, sources, consts["shift_free"])
    yb = _hyena(hpre, sgh, conv_h, bias, tabs, spectra, layer)
    return _merge(x, sc, sh, gt, g, ua, ga, conv_a, yb, yc, wts, layer), (kk, vv)


def kernel(x, c, ctx, c_ctx, norm_g, w_mod, b_mod, w_in, conv_a, conv_h, filt_w1, filt_b1, filt_w2,
           filt_b2, filt_w3, filt_b3, filt_freq, hyena_bias, q_norm_g, k_norm_g, w_branch, w_out):
    B, S, D = x.shape
    Lc = ctx.shape[1]
    cc = jnp.concatenate([c, c_ctx[None, :], jnp.zeros((MOD_ROWS - B - 1, D), F32)], axis=0)
    mod = _modulation(cc, w_mod, b_mod)

    tabs_s, tabs_c = _dft_tables(S), _dft_tables(Lc)
    filt = (filt_w1, filt_b1, filt_w2, filt_b2, filt_w3, filt_b3, filt_freq)
    spec_s = _filter_spectra(S, tabs_s, *filt)
    spec_c = _filter_spectra(Lc, tabs_c, *filt)

    cos, sin = _rope_tables(S)
    wts = _bf16_weights(w_in, w_branch, w_out)
    ones = jnp.ones((HEAD_DIM, HEAD_DIM), F32)
    bdq = jnp.kron(jnp.eye(N_HEADS, dtype=F32), ones).astype(BF16)
    bdk = jnp.kron(jnp.eye(N_KV_HEADS, dtype=F32), ones).astype(BF16)

    for i in range(DEPTH):
        last = i == DEPTH - 1
        consts = dict(gq=jnp.tile(q_norm_g[i], N_HEADS)[None, :],
                      gk=jnp.tile(k_norm_g[i], N_KV_HEADS)[None, :],
                      bdq=bdq, bdk=bdk, cos=cos, sin=sin)
        score_bound = (HEAD_DIM * Q_SCALE) * jnp.max(jnp.abs(q_norm_g[i])) * jnp.max(jnp.abs(k_norm_g[i]))
        consts["shift_free"] = score_bound <= SAFE_LOG2_RANGE
        split = lambda rows: [rows[:, None, j * D:(j + 1) * D] for j in range(3)]
        sh, sc, gt = split(mod[i, :B])
        sh_c, sc_c, gt_c = [jnp.broadcast_to(v, (B, 1, D)) for v in split(mod[i, B:B + 1])]
        if last:
            ctx_kv = tuple(_inproj(ctx, sc_c, sh_c, norm_g[i], wts, i, consts, rope=False, kv_only=True))
        else:
            ctx_next, ctx_kv = _mixers(ctx, sc_c, sh_c, gt_c, norm_g[i], wts, consts, conv_a[i], conv_h[i],
                                       hyena_bias[i], tabs_c, spec_c, i, False, None)
        x, _ = _mixers(x, sc, sh, gt, norm_g[i], wts, consts, conv_a[i], conv_h[i],
                       hyena_bias[i], tabs_s, spec_s, i, True, ctx_kv)
        if not last:
            ctx = ctx_next
    return x
```

```python
import functools
import math

import jax
import jax.numpy as jnp
from jax import lax
from jax.experimental import pallas as pl
from jax.experimental.pallas import tpu as pltpu

F32 = jnp.float32
BF16 = jnp.bfloat16

D_MODEL = 1024
DEPTH = 4
GRID_W = 64
W_BRANCH = 512
N_BRANCH = 3
N_HEADS = 8
N_KV_HEADS = 2
HEAD_DIM = 64
ROPE_FREQS = HEAD_DIM // 4
ROPE_BASE = 10000.0
ATTN_SCALE = HEAD_DIM ** -0.5
HYENA_ORDER = 2
FILTER_EMB = 33
FILTER_BANDS = (FILTER_EMB - 1) // 2
FILTER_HIDDEN = 64
HYENA_TARGET = 1e-2
MIN_DECAY = math.log(HYENA_TARGET) / 1.5
MAX_DECAY = math.log(HYENA_TARGET) / 0.3
EPS = 1e-6

A_COLS = 4 * W_BRANCH
H_OFF = A_COLS
C_OFF = H_OFF + 4 * W_BRANCH
Q_COLS = N_HEADS * HEAD_DIM
KV_COLS = N_KV_HEADS * HEAD_DIM
K_OFF = C_OFF + Q_COLS
V_OFF = K_OFF + KV_COLS
Z_OFF = V_OFF + KV_COLS
G_OFF = Z_OFF + W_BRANCH

LANES = 128
VMEM_LIMIT_BYTES = 58 * 1024 * 1024
KV_DUP_COLS = 2 * KV_COLS
V_TILE_COLS = 2 * KV_DUP_COLS
Q_SCALE = ATTN_SCALE * math.log2(math.e)
SAFE_LOG2_RANGE = 64.0
Q_SPLIT = 512
KEY_CHUNK = 512
MOD_ROWS = 24
FILT_PAD = 64
HALF = 256
RADIX = 4
ROW_SPLIT = 256


def _params(n_axes):
    return pltpu.CompilerParams(dimension_semantics=("arbitrary",) * n_axes,
                                vmem_limit_bytes=VMEM_LIMIT_BYTES)


def _const_spec(shape):
    nd = len(shape)
    return pl.BlockSpec(shape, lambda *_: (0,) * nd, pipeline_mode=pl.Buffered(1))


def _layer_spec(shape, layer):
    nd = len(shape)
    return pl.BlockSpec((1,) + tuple(shape[1:]), lambda *_: (layer,) + (0,) * (nd - 1),
                        pipeline_mode=pl.Buffered(1))


def _w_in_spec(layer, start, width):
    return pl.BlockSpec((pl.Element(1), pl.Element(D_MODEL), pl.Element(width)), lambda *_: (layer, 0, start),
                        pipeline_mode=pl.Buffered(1))


def _silu(v):
    return v * jax.nn.sigmoid(v)


def _dot(a, b):
    return jnp.dot(a, b, preferred_element_type=F32)


def _split(a):
    hi = a.astype(BF16)
    lo = (a - hi.astype(F32)).astype(BF16)
    return hi, lo


def _dot3(a, b):
    ah, al = _split(a)
    bh, bl = _split(b)
    return _dot(ah, bh) + (_dot(ah, bl) + _dot(al, bh))


def _rms_mod(x, g, sc, sh):
    y = x * lax.rsqrt(jnp.mean(x * x, axis=-1, keepdims=True) + EPS)
    return (y * g) * (1.0 + sc) + sh


def _shift_rows(u, first_row, last_row):
    n = u.shape[0]
    row = lax.broadcasted_iota(jnp.int32, u.shape, 0)
    prev = jnp.where(row == 0, first_row, pltpu.roll(u, 1, 0))
    nxt = jnp.where(row == n - 1, last_row, pltpu.roll(u, n - 1, 0))
    return prev, nxt


def _conv3(u, w, first_row=0.0, last_row=0.0):
    prev, nxt = _shift_rows(u, first_row, last_row)
    return prev * w[0:1] + u * w[1:2] + nxt * w[2:3]


def _mod_kernel(cc_ref, w_ref, b_ref, o_ref):
    o_ref[0] = _dot3(_silu(cc_ref[...]), w_ref[0]) + b_ref[0]


def _modulation(cc, w_mod, b_mod):
    return pl.pallas_call(
        _mod_kernel,
        grid=(DEPTH, 3),
        in_specs=[pl.BlockSpec((MOD_ROWS, D_MODEL), lambda i, j: (0, 0)),
                  pl.BlockSpec((1, D_MODEL, D_MODEL), lambda i, j: (i, 0, j)),
                  pl.BlockSpec((1, 1, D_MODEL), lambda i, j: (i, 0, j))],
        out_specs=pl.BlockSpec((1, MOD_ROWS, D_MODEL), lambda i, j: (i, 0, j)),
        out_shape=jax.ShapeDtypeStruct((DEPTH, MOD_ROWS, 3 * D_MODEL), F32),
        compiler_params=_params(2),
        name="modulation",
    )(cc, w_mod, b_mod.reshape(DEPTH, 1, 3 * D_MODEL))


def _dft_tables(L):
    n2, m_sub = 4 * L, L // RADIX
    idx = jnp.arange(m_sub, dtype=jnp.int32)
    sample = RADIX * idx[None, None, :] + jnp.arange(RADIX, dtype=jnp.int32)[:, None, None]
    q = ((2 * idx[None, :, None] + 1) * sample) % n2
    ang = q.astype(F32) * (2.0 * math.pi / n2)
    cm, sm = jnp.cos(ang).astype(BF16), (-jnp.sin(ang)).astype(BF16)
    tr = lambda v: jnp.swapaxes(v, 1, 2) * (1.0 / L)
    return dict(cm=cm, sm=sm, icm=tr(cm), ism=tr(sm))


def _butterfly4(t0, t1, t2, t3):
    pr, pi, mr, mi = t0[0] + t2[0], t0[1] + t2[1], t0[0] - t2[0], t0[1] - t2[1]
    qr, qi, nr, ni = t1[0] + t3[0], t1[1] + t3[1], t1[0] - t3[0], t1[1] - t3[1]
    return (pr + qr, pi + qi), (mr - ni, mi + nr), (mr + ni, mi - nr), (pr - qr, pi - qi)


def _split_dft(parts, cm_ref, sm_ref):
    t = []
    for s in range(RADIX):
        u = parts[s].astype(BF16)
        t.append((_dot(cm_ref[s], u), _dot(sm_ref[s], u)))
    return _butterfly4(*t)


def _filter_hidden_kernel(feats_ref, w1_ref, b1_ref, w2_ref, b2_ref, fq_ref, h_ref):
    fq = fq_ref[0]
    h = jnp.sin(fq[0:1] * (_dot3(feats_ref[...], w1_ref[0]) + b1_ref[0]))
    h_ref[0] = jnp.sin(fq[1:2] * (_dot3(h, w2_ref[0]) + b2_ref[0]))


def _filter_kernel(h_ref, w3f_ref, w3b_ref, b3f_ref, b3b_ref, dl_ref, cm_ref, sm_ref, k_ref, *, L):
    m_sub = L // RADIX
    c = dl_ref.shape[1]
    h = h_ref[0]
    row = lax.broadcasted_iota(jnp.int32, (L, c), 0)
    lag = RADIX * (row % m_sub) + row // m_sub
    win = jnp.exp(-(lag.astype(F32) * (1.0 / (L - 1))) * dl_ref[...])
    hf = (_dot3(h, w3f_ref[0]) + b3f_ref[0]) * win
    hb = jnp.where(lag == 0, 0.0, (_dot3(h, w3b_ref[0]) + b3b_ref[0]) * win)
    nrm = jnp.sum(jnp.abs(hf), axis=0, keepdims=True) + jnp.sum(jnp.abs(hb), axis=0, keepdims=True)
    inv = 1.0 / nrm
    hf, hb = hf * inv, hb * inv
    parts = [jnp.concatenate([hf[s * m_sub:(s + 1) * m_sub], hb[s * m_sub:(s + 1) * m_sub]], axis=1)
             for s in range(RADIX)]
    spec = _split_dft(parts, cm_ref, sm_ref)
    for j, (re, im) in enumerate(spec):
        k_ref[0, 0, 2 * j] = re[:, :c] + re[:, c:]
        k_ref[0, 0, 2 * j + 1] = im[:, :c] - im[:, c:]


def _filter_spectra(L, tabs, filt_w1, filt_b1, filt_w2, filt_b2, filt_w3, filt_b3, filt_freq):
    m_sub = L // RADIX
    t = jnp.linspace(0.0, 1.0, L, dtype=F32)[:, None]
    bands = jnp.linspace(1e-4, FILTER_BANDS - 1, FILTER_BANDS, dtype=F32)
    w = (2.0 * math.pi / L) * jnp.arange(L, dtype=F32)[:, None]
    feats = jnp.concatenate([t, jnp.cos(w * bands), jnp.sin(w * bands)], axis=-1)
    feats = jnp.concatenate([feats[s::RADIX] for s in range(RADIX)], axis=0)
    feats = jnp.pad(feats, ((0, 0), (0, FILT_PAD - FILTER_EMB)))
    w1 = jnp.pad(filt_w1, ((0, 0), (0, FILT_PAD - FILTER_EMB), (0, 0)))
    deltas = jnp.abs(jnp.linspace(MIN_DECAY, MAX_DECAY, W_BRANCH, dtype=F32))[None, :]
    lay1 = lambda i: (i, 0, 0)
    hidden = pl.pallas_call(
        _filter_hidden_kernel,
        grid=(DEPTH,),
        in_specs=[_const_spec((L, FILT_PAD)),
                  pl.BlockSpec((1, FILT_PAD, FILTER_HIDDEN), lay1),
                  pl.BlockSpec((1, 1, FILTER_HIDDEN), lay1),
                  pl.BlockSpec((1, FILTER_HIDDEN, FILTER_HIDDEN), lay1),
                  pl.BlockSpec((1, 1, FILTER_HIDDEN), lay1),
                  pl.BlockSpec((1, 2, FILTER_HIDDEN), lay1)],
        out_specs=pl.BlockSpec((1, L, FILTER_HIDDEN), lay1),
        out_shape=jax.ShapeDtypeStruct((DEPTH, L, FILTER_HIDDEN), F32),
        compiler_params=_params(1),
        name=f"filter_hidden_{L}",
    )(feats, w1, filt_b1[:, None], filt_w2, filt_b2[:, None], filt_freq)
    nc = W_BRANCH // HALF
    fwd = lambda i, o, c: (i, 0, o * 2 * nc + c)
    bwd = lambda i, o, c: (i, 0, o * 2 * nc + nc + c)
    tab = _const_spec((RADIX, m_sub, m_sub))
    return pl.pallas_call(
        functools.partial(_filter_kernel, L=L),
        grid=(DEPTH, HYENA_ORDER, nc),
        in_specs=[pl.BlockSpec((1, L, FILTER_HIDDEN), lambda i, o, c: (i, 0, 0)),
                  pl.BlockSpec((1, FILTER_HIDDEN, HALF), fwd),
                  pl.BlockSpec((1, FILTER_HIDDEN, HALF), bwd),
                  pl.BlockSpec((1, 1, HALF), fwd),
                  pl.BlockSpec((1, 1, HALF), bwd),
                  pl.BlockSpec((1, HALF), lambda i, o, c: (0, c)),
                  tab, tab],
        out_specs=pl.BlockSpec((1, 1, 2 * RADIX, m_sub, HALF), lambda i, o, c: (i, o, 0, 0, c)),
        out_shape=jax.ShapeDtypeStruct((DEPTH, HYENA_ORDER, 2 * RADIX, m_sub, W_BRANCH), F32),
        compiler_params=_params(3),
        name=f"filter_spectra_{L}",
    )(hidden, filt_w3, filt_w3, filt_b3[:, None], filt_b3[:, None], deltas,
      tabs["cm"], tabs["sm"])


def _head_norm(v, bd_ref, g):
    ss = _dot((v * v).astype(BF16), bd_ref[...])
    return v * lax.rsqrt(ss * (1.0 / HEAD_DIM) + EPS) * g


def _rope(v, cos, sin):
    n = v.shape[1] // LANES
    lane = lax.broadcasted_iota(jnp.int32, (v.shape[0], LANES), 1)
    first = (lane % 32) < 16
    out = []
    for i in range(n):
        c = v[:, i * LANES:(i + 1) * LANES]
        partner = jnp.where(first, pltpu.roll(c, LANES - 16, 1), pltpu.roll(c, 16, 1))
        out.append(c * cos + partner * sin)
    return jnp.concatenate(out, axis=1)


def _kv_tiles(k, v):
    low = lax.broadcasted_iota(jnp.int32, k.shape, 1) < HEAD_DIM
    ks, vs = pltpu.roll(k, HEAD_DIM, 1), pltpu.roll(v, HEAD_DIM, 1)
    kk = jnp.concatenate([jnp.where(low, k, ks), jnp.where(low, ks, k)], axis=1)
    vv = jnp.concatenate([jnp.where(low, v, 1.0), jnp.where(low, 1.0, vs),
                          jnp.where(low, vs, 1.0), jnp.where(low, 1.0, v)], axis=1)
    return kk.astype(BF16), vv.astype(BF16)


def _inproj_kernel(*refs, rope, kv_only):
    it = iter(refs)
    x_ref, sc_ref, sh_ref, g_ref = next(it), next(it), next(it), next(it)
    if not kv_only:
        wa_ref, wh_ref = next(it), next(it)
    wq_ref = next(it)
    if not kv_only:
        gq_ref, bdq_ref = next(it), next(it)
    gk_ref, bdk_ref = next(it), next(it)
    if rope:
        cos_ref, sin_ref = next(it), next(it)
    outs = list(it)

    h = _rms_mod(x_ref[0], g_ref[...], sc_ref[0], sh_ref[0]).astype(BF16)
    w = W_BRANCH
    if kv_only:
        kk_ref, vv_ref = outs
        acc = _dot(h, wq_ref[0])
        k = _head_norm(acc[:, :KV_COLS], bdk_ref, gk_ref[...])
        kk_ref[0], vv_ref[0] = _kv_tiles(k, acc[:, KV_COLS:])
        return

    ua_ref, ga_ref, hp_ref, sgh_ref, q_ref, kk_ref, vv_ref, sz_ref = outs
    for r0 in range(0, h.shape[0], ROW_SPLIT):
        rows = slice(r0, r0 + ROW_SPLIT)
        hr = h[rows]
        acc = _dot(hr, wq_ref[0])
        q = _head_norm(acc[:, :Q_COLS], bdq_ref, gq_ref[...])
        k = _head_norm(acc[:, Q_COLS:Q_COLS + KV_COLS], bdk_ref, gk_ref[...])
        if rope:
            q = _rope(q, cos_ref[rows], sin_ref[rows])
            k = _rope(k, cos_ref[rows], sin_ref[rows])
        q_ref[0, rows] = (q * Q_SCALE).astype(BF16)
        kk_ref[0, rows], vv_ref[0, rows] = _kv_tiles(k, acc[:, Q_COLS + KV_COLS:Q_COLS + 2 * KV_COLS])
        sz_ref[0, rows] = _silu(acc[:, Q_COLS + 2 * KV_COLS:]).astype(BF16)
        acc = _dot(hr, wa_ref[0])
        ua_ref[0, rows] = (acc[:, 2 * w:3 * w] * acc[:, :w]).astype(BF16)
        ga_ref[0, rows] = (acc[:, w:2 * w] * _silu(acc[:, 3 * w:])).astype(BF16)
        acc = _dot(hr, wh_ref[0])
        hp_ref[0, rows] = acc[:, :3 * w].astype(BF16)
        sgh_ref[0, rows] = _silu(acc[:, 3 * w:]).astype(BF16)


def _inproj(x, sc, sh, g, wts, layer, consts, *, rope, kv_only):
    B, T, D = x.shape
    tm = min(T, 1024)
    row = lambda b, t: (b, t, 0)
    vec = pl.BlockSpec((1, 1, D), lambda b, t: (b, 0, 0))
    args = [x, sc, sh, g[None, :]]
    specs = [pl.BlockSpec((1, tm, D), row), vec, vec, _const_spec((1, D))]
    if not kv_only:
        args += [wts["w_in"], wts["w_in"], wts["w_in"], consts["gq"], consts["bdq"]]
        specs += [_w_in_spec(layer, 0, A_COLS), _w_in_spec(layer, H_OFF, C_OFF - H_OFF),
                  _w_in_spec(layer, C_OFF, G_OFF - C_OFF), _const_spec(consts["gq"].shape),
                  _const_spec(consts["bdq"].shape)]
    else:
        args += [wts["w_in"]]
        specs += [_w_in_spec(layer, K_OFF, Z_OFF - K_OFF)]
    args += [consts["gk"], consts["bdk"]]
    specs += [_const_spec(consts["gk"].shape), _const_spec(consts["bdk"].shape)]
    if rope:
        args += [consts["cos"], consts["sin"]]
        specs += [pl.BlockSpec((tm, LANES), lambda b, t: (t, 0))] * 2
    widths = ([KV_DUP_COLS, V_TILE_COLS] if kv_only else
              [W_BRANCH, W_BRANCH, 3 * W_BRANCH, W_BRANCH, Q_COLS, KV_DUP_COLS, V_TILE_COLS, W_BRANCH])
    return pl.pallas_call(
        functools.partial(_inproj_kernel, rope=rope, kv_only=kv_only),
        grid=(B, T // tm),
        in_specs=specs,
        out_specs=[pl.BlockSpec((1, tm, n), row) for n in widths],
        out_shape=[jax.ShapeDtypeStruct((B, T, n), BF16) for n in widths],
        compiler_params=_params(2),
        name=f"inproj_{T}" + ("_kv" if kv_only else ""),
    )(*args)


def _attn_kernel(*refs, n_src, shifted):
    q_ref, sz_ref = refs[0], refs[1]
    kv = refs[2:2 + 2 * n_src]
    o_ref = refs[-1]
    tq = min(q_ref.shape[1], Q_SPLIT)
    low = lax.broadcasted_iota(jnp.int32, (tq, LANES), 1) < HEAD_DIM
    nt = (((1,), (1,)), ((), ()))
    for r0, j in [(r0, j) for r0 in range(0, q_ref.shape[1], tq) for j in range(2)]:
        qrows = slice(r0, r0 + tq)
        cols = slice(j * LANES, (j + 1) * LANES)
        qp = q_ref[0, qrows, cols].astype(F32)
        halves = []
        for half, sel in enumerate((low, jnp.logical_not(low))):
            qm = jnp.where(sel, qp, 0.0).astype(BF16)
            vt = slice(half * LANES, (half + 1) * LANES)
            if shifted:
                s = [lax.dot_general(qm, kv[2 * i][0], nt, preferred_element_type=F32) for i in range(n_src)]
                m = functools.reduce(jnp.maximum, [jnp.max(v, axis=-1, keepdims=True) for v in s])
                terms = [_dot(jnp.exp2(s[i] - m).astype(BF16), kv[2 * i + 1][0, :, vt]) for i in range(n_src)]
            else:
                terms = []
                for i in range(n_src):
                    n_keys = kv[2 * i].shape[1]
                    for c0 in range(0, n_keys, KEY_CHUNK):
                        rows = slice(c0, min(c0 + KEY_CHUNK, n_keys))
                        s = lax.dot_general(qm, kv[2 * i][0, rows, :], nt, preferred_element_type=F32)
                        terms.append(_dot(jnp.exp2(s).astype(BF16), kv[2 * i + 1][0, rows, vt]))
            halves.append(functools.reduce(jnp.add, terms))
        num = jnp.where(low, halves[0], halves[1])
        den = pltpu.roll(jnp.where(low, halves[1], halves[0]), HEAD_DIM, 1)
        o_ref[0, qrows, cols] = (num / den * sz_ref[0, qrows, cols].astype(F32)).astype(BF16)


def _attention(q, sz, sources, shift_free):
    B, T, _ = q.shape
    gw = Q_COLS // N_KV_HEADS
    args = [q, sz] + [a for src in sources for a in src]

    def call(shifted):
        tq = min(T, 256 if shifted else 2048)
        qspec = pl.BlockSpec((1, tq, gw), lambda b, g, t: (b, t, g))
        specs = [qspec, qspec]
        for kk, vv in sources:
            specs += [pl.BlockSpec((1, kk.shape[1], LANES), lambda b, g, t: (b, 0, g)),
                      pl.BlockSpec((1, vv.shape[1], 2 * LANES), lambda b, g, t: (b, 0, g))]
        return pl.pallas_call(
            functools.partial(_attn_kernel, n_src=len(sources), shifted=shifted),
            grid=(B, N_KV_HEADS, T // tq),
            in_specs=specs,
            out_specs=qspec,
            out_shape=jax.ShapeDtypeStruct((B, T, Q_COLS), BF16),
            compiler_params=_params(3),
            name=f"attention_{T}" + ("_shifted" if shifted else ""),
        )
    return lax.cond(shift_free, call(False), call(True), *args)


def _deinterleave(x, scr):
    n, c = x.shape
    slabs = range(c // LANES)
    for j in slabs:
        scr[j][...] = x[:, j * LANES:(j + 1) * LANES]
    return [jnp.concatenate([scr[j][pl.ds(s, n // RADIX, stride=RADIX), :] for j in slabs], axis=1)
            for s in range(RADIX)]


def _interleave(parts, scr):
    m, c = parts[0].shape
    slabs = range(c // LANES)
    for j in slabs:
        for s in range(RADIX):
            scr[j][pl.ds(s, m, stride=RADIX), :] = parts[s][:, j * LANES:(j + 1) * LANES]
    return jnp.concatenate([scr[j][...] for j in slabs], axis=1)


def _conv3_split(parts, w):
    m = parts[0].shape[0]
    row = lax.broadcasted_iota(jnp.int32, parts[0].shape, 0)
    before = jnp.where(row == 0, 0.0, pltpu.roll(parts[-1], 1, 0))
    after = jnp.where(row == m - 1, 0.0, pltpu.roll(parts[0], m - 1, 0))
    ext = [before] + list(parts) + [after]
    return [ext[s] * w[0:1] + ext[s + 1] * w[1:2] + ext[s + 2] * w[2:3] for s in range(RADIX)]


def _long_conv_split(parts, k, cm_ref, sm_ref, icm_ref, ism_ref):
    spec = _split_dft(parts, cm_ref, sm_ref)
    ya, yb, yc, yd = [(xr * k[2 * j] - xi * k[2 * j + 1], xr * k[2 * j + 1] + xi * k[2 * j])
                      for j, (xr, xi) in enumerate(spec)]
    g0, g1, g3, g2 = _butterfly4(ya, yc, yd, yb)
    return [_dot(icm_ref[s], gr.astype(BF16)) + _dot(ism_ref[s], gi.astype(BF16))
            for s, (gr, gi) in enumerate((g0, g1, g2, g3))]


def _hyena_kernel(hv_ref, hx1_ref, hx2_ref, sg_ref, wv_ref, wx1_ref, wx2_ref, bias_ref,
                  cm_ref, sm_ref, icm_ref, ism_ref, k_ref, o_ref, *scratch):
    c = o_ref.shape[2]
    ns = c // LANES
    s_v, s_x1, s_x2, s_g, s_o = [scratch[i * ns:(i + 1) * ns] for i in range(5)]
    tabs = (cm_ref, sm_ref, icm_ref, ism_ref)
    split = lambda ref, scr: _deinterleave(ref[0].astype(F32), scr)
    v = _conv3_split(split(hv_ref, s_v), wv_ref[...])
    x1 = _conv3_split(split(hx1_ref, s_x1), wx1_ref[...])
    x2 = _conv3_split(split(hx2_ref, s_x2), wx2_ref[...])
    gate = split(sg_ref, s_g)
    b0, b1 = bias_ref[0:1], bias_ref[1:2]
    n_k = 2 * RADIX
    y = _long_conv_split(v, [k_ref[0, 0, j] for j in range(n_k)], *tabs)
    z1 = [x1[s] * (y[s] + v[s] * b0) for s in range(RADIX)]
    y = _long_conv_split(z1, [k_ref[0, 1, j] for j in range(n_k)], *tabs)
    out = [x2[s] * (y[s] + z1[s] * b1) * gate[s] for s in range(RADIX)]
    o_ref[0] = _interleave(out, s_o).astype(BF16)


def _hyena(hpre, sgh, conv_h, bias, tabs, spectra, layer):
    B, L, _ = hpre.shape
    m_sub = L // RADIX
    lanes = HALF if L * HALF * 4 > (1 << 20) else W_BRANCH
    nc = W_BRANCH // lanes
    blk = lambda j: pl.BlockSpec((1, L, lanes), lambda c, b: (b, 0, j * nc + c))
    cw = lambda j: pl.BlockSpec((3, lanes), lambda c, b: (0, j * nc + c))
    tab = _const_spec((RADIX, m_sub, m_sub))
    kspec = pl.BlockSpec((1, HYENA_ORDER, 2 * RADIX, m_sub, lanes), lambda c, b: (layer, 0, 0, 0, c),
                         pipeline_mode=pl.Buffered(1))
    return pl.pallas_call(
        _hyena_kernel,
        grid=(nc, B),
        in_specs=[blk(0), blk(1), blk(2), blk(0), cw(0), cw(1), cw(2),
                  pl.BlockSpec((HYENA_ORDER, lanes), lambda c, b: (0, c)),
                  tab, tab, tab, tab, kspec],
        out_specs=blk(0),
        out_shape=jax.ShapeDtypeStruct((B, L, W_BRANCH), BF16),
        scratch_shapes=[pltpu.VMEM((L, LANES), F32)] * (5 * lanes // LANES),
        compiler_params=_params(2),
        name=f"hyena_{L}",
    )(hpre, hpre, hpre, sgh, conv_h, conv_h, conv_h, bias,
      tabs["cm"], tabs["sm"], tabs["icm"], tabs["ism"], spectra)


def _merge_kernel(x_ref, sc_ref, sh_ref, gt_ref, g_ref, ua_ref, up_ref, un_ref, ga_ref, ca_ref,
                  yb_ref, yc_ref, wg_ref, wb_ref, wo_ref, o_ref):
    t, nt = pl.program_id(1), pl.num_programs(1)
    x = x_ref[0]
    h = _rms_mod(x, g_ref[...], sc_ref[0], sh_ref[0]).astype(BF16)
    halo = up_ref.shape[1]
    first = jnp.where(t == 0, 0.0, up_ref[0, halo - 1:halo, :].astype(F32))
    last = jnp.where(t == nt - 1, 0.0, un_ref[0, 0:1, :].astype(F32))
    ya = (ga_ref[0].astype(F32) * _conv3(ua_ref[0].astype(F32), ca_ref[...], first, last)).astype(BF16)
    for r0 in range(0, x.shape[0], ROW_SPLIT):
        rows = slice(r0, r0 + ROW_SPLIT)
        ys = (ya[rows], yb_ref[0, rows], yc_ref[0, rows])
        merged = None
        for n in range(N_BRANCH):
            gate = jax.nn.sigmoid(_dot(h[rows], wg_ref[0, :, n * D_MODEL:(n + 1) * D_MODEL]))
            term = gate * _dot(ys[n], wb_ref[0, n])
            merged = term if merged is None else merged + term
        out = _dot(merged.astype(BF16), wo_ref[0])
        o_ref[0, rows] = x[rows] + gt_ref[0] * out


def _merge(x, sc, sh, gt, g, ua, ga, conv_a, yb, yc, wts, layer):
    B, T, D = x.shape
    tm = min(T, 1024)
    halo = 16
    r = tm // halo
    row = lambda b, t: (b, t, 0)
    vec = pl.BlockSpec((1, 1, D), lambda b, t: (b, 0, 0))
    br = pl.BlockSpec((1, tm, W_BRANCH), row)
    prev = pl.BlockSpec((1, halo, W_BRANCH), lambda b, t: (b, jnp.maximum(t * r - 1, 0), 0))
    nxt = pl.BlockSpec((1, halo, W_BRANCH), lambda b, t: (b, jnp.minimum((t + 1) * r, T // halo - 1), 0))
    return pl.pallas_call(
        _merge_kernel,
        grid=(B, T // tm),
        in_specs=[pl.BlockSpec((1, tm, D), row), vec, vec, vec, _const_spec((1, D)),
                  br, prev, nxt, br, _const_spec((3, W_BRANCH)), br, br,
                  _w_in_spec(layer, G_OFF, N_BRANCH * D_MODEL), _layer_spec(wts["wb"].shape, layer),
                  _layer_spec(wts["wo"].shape, layer)],
        out_specs=pl.BlockSpec((1, tm, D), row),
        out_shape=jax.ShapeDtypeStruct((B, T, D), F32),
        compiler_params=_params(2),
        name=f"merge_{T}",
    )(x, sc, sh, gt, g[None, :], ua, ua, ua, ga, conv_a, yb, yc, wts["w_in"], wts["wb"], wts["wo"])


def _rope_tables(S):
    pos = jnp.arange(S, dtype=jnp.int32)
    coord = jnp.stack([(pos // GRID_W).astype(F32), (pos % GRID_W).astype(F32)], axis=1)
    inv = ROPE_BASE ** (-jnp.arange(ROPE_FREQS, dtype=F32) / ROPE_FREQS)
    ang = coord[:, :, None] * inv
    cos = jnp.repeat(jnp.cos(ang)[:, :, None, :], 2, axis=2).reshape(S, HEAD_DIM)
    sin = jnp.sin(ang)
    sin = jnp.stack([-sin, sin], axis=2).reshape(S, HEAD_DIM)
    return jnp.tile(cos, (1, 2)), jnp.tile(sin, (1, 2))


def _bf16_weights(w_in, w_branch, w_out):
    return dict(
        w_in=w_in.astype(BF16),
        wb=w_branch.astype(BF16),
        wo=w_out.astype(BF16),
    )


def _mixers(x, sc, sh, gt, g, wts, consts, conv_a, conv_h, bias, tabs, spectra, layer, rope, ctx_kv):
    ua, ga, hpre, sgh, q, kk, vv, sz = _inproj(x, sc, sh, g, wts, layer, consts, rope=rope, kv_only=False)
    sources = ([ctx_kv] if ctx_kv is not None else []) + [(kk, vv)]
    yc = _attention(q, sz, sources, consts["shift_free"])
    yb = _hyena(hpre, sgh, conv_h, bias, tabs, spectra, layer)
    return _merge(x, sc, sh, gt, g, ua, ga, conv_a, yb, yc, wts, layer), (kk, vv)


def kernel(x, c, ctx, c_ctx, norm_g, w_mod, b_mod, w_in, conv_a, conv_h, filt_w1, filt_b1, filt_w2,
           filt_b2, filt_w3, filt_b3, filt_freq, hyena_bias, q_norm_g, k_norm_g, w_branch, w_out):
    B, S, D = x.shape
    Lc = ctx.shape[1]
    cc = jnp.concatenate([c, c_ctx[None, :], jnp.zeros((MOD_ROWS - B - 1, D), F32)], axis=0)
    mod = _modulation(cc, w_mod, b_mod)

    tabs_s, tabs_c = _dft_tables(S), _dft_tables(Lc)
    filt = (filt_w1, filt_b1, filt_w2, filt_b2, filt_w3, filt_b3, filt_freq)
    spec_s = _filter_spectra(S, tabs_s, *filt)
    spec_c = _filter_spectra(Lc, tabs_c, *filt)

    cos, sin = _rope_tables(S)
    wts = _bf16_weights(w_in, w_branch, w_out)
    ones = jnp.ones((HEAD_DIM, HEAD_DIM), F32)
    bdq = jnp.kron(jnp.eye(N_HEADS, dtype=F32), ones).astype(BF16)
    bdk = jnp.kron(jnp.eye(N_KV_HEADS, dtype=F32), ones).astype(BF16)

    for i in range(DEPTH):
        last = i == DEPTH - 1
        consts = dict(gq=jnp.tile(q_norm_g[i], N_HEADS)[None, :],
                      gk=jnp.tile(k_norm_g[i], N_KV_HEADS)[None, :],
                      bdq=bdq, bdk=bdk, cos=cos, sin=sin)
        score_bound = (HEAD_DIM * Q_SCALE) * jnp.max(jnp.abs(q_norm_g[i])) * jnp.max(jnp.abs(k_norm_g[i]))
        consts["shift_free"] = score_bound <= SAFE_LOG2_RANGE
        split = lambda rows: [rows[:, None, j * D:(j + 1) * D] for j in range(3)]
        sh, sc, gt = split(mod[i, :B])
        sh_c, sc_c, gt_c = [jnp.broadcast_to(v, (B, 1, D)) for v in split(mod[i, B:B + 1])]
        if last:
            ctx_kv = tuple(_inproj(ctx, sc_c, sh_c, norm_g[i], wts, i, consts, rope=False, kv_only=True))
        else:
            ctx_next, ctx_kv = _mixers(ctx, sc_c, sh_c, gt_c, norm_g[i], wts, consts, conv_a[i], conv_h[i],
                                       hyena_bias[i], tabs_c, spec_c, i, False, None)
        x, _ = _mixers(x, sc, sh, gt, norm_g[i], wts, consts, conv_a[i], conv_h[i],
                       hyena_bias[i], tabs_s, spec_s, i, True, ctx_kv)
        if not last:
            ctx = ctx_next
    return x
```
